```python
import math
import jax, jax.numpy as jnp
from jax import lax
import numpy as np

D_MODEL = 1024
BATCH = 8
SEQ = 2048
DEPTH = 2
DEC_BATCH = 128
DEC_SEQ = 8
PAST_LEN = 16384
PAGE_SIZE = 128

N_BRANCH = 4
BRANCH_W = D_MODEL // 2
CONV_W = 4
CHUNK = 64
D_FF = 2816
EPS = 1e-6
ROPE_BASE = 10000.0

MB_HEADDIM = 64
MB_HEADS = BRANCH_W // MB_HEADDIM
MB_GROUPS = 2
MB_STATE = 64
MB_CONV_DIM = BRANCH_W + 2 * MB_GROUPS * MB_STATE

HG_DK = 128
HG_HEADS = BRANCH_W // HG_DK
HG_DV = BRANCH_W // HG_HEADS

RET_HEADS = 4
RET_DK = 64
RET_DV = BRANCH_W // RET_HEADS

GDN_HEADS = 4
GDN_DK = 128
GDN_DV = BRANCH_W // GDN_HEADS
GDN_CONV_DIM = 2 * GDN_HEADS * GDN_DK + GDN_HEADS * GDN_DV

IN_SPLITS = (
    BRANCH_W, MB_CONV_DIM, MB_HEADS,
    HG_HEADS * HG_DK, HG_HEADS * HG_DK, BRANCH_W, BRANCH_W,
    RET_HEADS * RET_DK, RET_HEADS * RET_DK, BRANCH_W, BRANCH_W,
    GDN_CONV_DIM, BRANCH_W, GDN_HEADS, GDN_HEADS,
    N_BRANCH * D_MODEL,
)
D_IN = sum(IN_SPLITS)
IN_OFFSETS = tuple(int(v) for v in np.cumsum(IN_SPLITS)[:-1])

kernel_name = 'hybrid_ssd_hgrn2_retnet_gdn_step'


def rmsnorm(x, g):
    xf = x.astype(jnp.float32)
    y = xf * lax.rsqrt(jnp.mean(xf * xf, axis=-1, keepdims=True) + EPS)
    if g is not None:
        y = y * g.astype(jnp.float32)
    return y.astype(x.dtype)


def l2norm(x):
    xf = x.astype(jnp.float32)
    return xf * lax.rsqrt(jnp.sum(xf * xf, axis=-1, keepdims=True) + EPS)


def swiglu(x, w_in, w_out):
    gate, up = jnp.split(x @ w_in, 2, axis=-1)
    return (jax.nn.silu(gate) * up) @ w_out


def rotary(x, pos):
    half = x.shape[-1] // 2
    inv_freq = 1.0 / (ROPE_BASE ** jnp.linspace(0.0, 1.0, half, dtype=jnp.float32))
    ang = pos.astype(jnp.float32)[:, None] * inv_freq[None, :]
    cos = jnp.cos(ang)[None, :, None, :]
    sin = jnp.sin(ang)[None, :, None, :]
    xf = x.astype(jnp.float32)
    x1, x2 = xf[..., :half], xf[..., half:]
    return jnp.concatenate([x1 * cos - x2 * sin, x1 * sin + x2 * cos], axis=-1)


def causal_conv(x, buf, w):
    seq = x.shape[1]
    xp = jnp.concatenate([buf.astype(x.dtype), x], axis=1)
    y = xp[:, 0:seq] * w[0]
    for j in range(1, CONV_W):
        y = y + xp[:, j:j + seq] * w[j]
    return y, xp[:, seq:]


def _to_chunks(a, c):
    b, l = a.shape[0], a.shape[1]
    return jnp.moveaxis(a.reshape((b, l // c, c) + a.shape[2:]), 1, 0)


def _from_chunks(a):
    n, b, c = a.shape[0], a.shape[1], a.shape[2]
    return jnp.moveaxis(a, 0, 1).reshape((b, n * c) + a.shape[3:])


def chunked_scalar_decay(q, k, v, log_a, s0):
    f32 = jnp.float32
    c = math.gcd(q.shape[1], CHUNK)
    incl = jnp.tril(jnp.ones((c, c), dtype=bool))[None, :, :, None]

    def step(s, blk):
        qc, kc, vc, lac = blk
        cum = jnp.cumsum(lac, axis=1)
        dec = jnp.exp(jnp.where(incl, cum[:, :, None] - cum[:, None], -jnp.inf))
        scores = jnp.einsum('bihd,bjhd->bijh', qc, kc) * dec
        o = (jnp.einsum('bijh,bjhv->bihv', scores, vc)
             + jnp.einsum('bihd,bhdv->bihv', qc * jnp.exp(cum)[..., None], s))
        k_end = kc * jnp.exp(cum[:, -1:] - cum)[..., None]
        s = s * jnp.exp(cum[:, -1])[..., None, None] + jnp.einsum('bjhd,bjhv->bhdv', k_end, vc)
        return s, o

    xs = tuple(_to_chunks(a.astype(f32), c) for a in (q, k, v, log_a))
    s, o = lax.scan(step, s0.astype(f32), xs)
    return _from_chunks(o), s


def chunked_vector_decay(q, k, v, log_f, s0):
    f32 = jnp.float32
    c = math.gcd(q.shape[1], CHUNK)
    incl = jnp.tril(jnp.ones((c, c), dtype=bool))[None, :, :, None, None]

    def step(s, blk):
        qc, kc, vc, lfc = blk
        cum = jnp.cumsum(lfc, axis=1)
        dec = jnp.exp(jnp.where(incl, cum[:, :, None] - cum[:, None], -jnp.inf))
        scores = jnp.einsum('bihd,bjhd,bijhd->bijh', qc, kc, dec)
        o = (jnp.einsum('bijh,bjhv->bihv', scores, vc)
             + jnp.einsum('bihd,bhdv->bihv', qc * jnp.exp(cum), s))
        k_end = kc * jnp.exp(cum[:, -1:] - cum)
        s = s * jnp.exp(cum[:, -1])[..., None] + jnp.einsum('bjhd,bjhv->bhdv', k_end, vc)
        return s, o

    xs = tuple(_to_chunks(a.astype(f32), c) for a in (q, k, v, log_f))
    s, o = lax.scan(step, s0.astype(f32), xs)
    return _from_chunks(o), s


def chunked_gated_delta(q, k, v, log_g, beta, s0):
    f32 = jnp.float32
    c = math.gcd(q.shape[1], CHUNK)
    incl = jnp.tril(jnp.ones((c, c), dtype=bool))[None, None]
    strict = jnp.tril(jnp.ones((c, c), dtype=bool), -1)[None, None]
    dv = v.shape[-1]

    def step(s, blk):
        qc, kc, vc, lgc, bc = blk
        cum = jnp.cumsum(lgc, axis=1)
        cum_h = jnp.swapaxes(cum, 1, 2)
        dec = jnp.exp(jnp.where(incl, cum_h[..., :, None] - cum_h[..., None, :], -jnp.inf))
        b_h = jnp.swapaxes(bc, 1, 2)
        a_mat = jnp.where(strict, b_h[..., :, None] * dec * jnp.einsum('bihd,bjhd->bhij', kc, kc), 0.0)
        rhs = jnp.concatenate([vc, kc * jnp.exp(cum)[..., None]], axis=-1) * bc[..., None]
        sol = lax.linalg.triangular_solve(a_mat + jnp.eye(c, dtype=f32), jnp.swapaxes(rhs, 1, 2),
                                          left_side=True, lower=True, unit_diagonal=True)
        w_new = sol[..., :dv] - jnp.einsum('bhcd,bhdv->bhcv', sol[..., dv:], s)
        scores = jnp.einsum('bihd,bjhd->bhij', qc, kc) * dec
        o = (jnp.einsum('bhij,bhjv->bihv', scores, w_new)
             + jnp.einsum('bihd,bhdv->bihv', qc * jnp.exp(cum)[..., None], s))
        k_end = kc * jnp.exp(cum[:, -1:] - cum)[..., None]
        s = s * jnp.exp(cum[:, -1])[..., None, None] + jnp.einsum('bjhd,bhjv->bhdv', k_end, w_new)
        return s, o

    xs = tuple(_to_chunks(a.astype(f32), c) for a in (q, k, v, log_g, beta))
    s, o = lax.scan(step, s0.astype(f32), xs)
    return _from_chunks(o), s


def hybrid_layer(x, pos, states, lw):
    (s_ssm, s_ssm_conv, s_hg, s_ret, s_gdn, s_gdn_conv) = states
    (norm_g, ffn_w_in, ffn_w_out, w_in, w_branch, w_out,
     mb_conv_w, mb_conv_b, mb_a_log, mb_dt_bias, mb_d, mb_norm,
     hg_lb, hg_norm, gdn_conv_w, gdn_a_log, gdn_dt_bias, gdn_norm) = lw
    f32 = jnp.float32
    bsz, seq = x.shape[0], x.shape[1]

    h = x + 0.5 * rmsnorm(swiglu(rmsnorm(x, norm_g[0]), ffn_w_in[0], ffn_w_out[0]), norm_g[1])

    u = rmsnorm(h, norm_g[2])
    (mb_z, mb_xbc, mb_dt, hg_q, hg_f, hg_i, hg_g, ret_q, ret_k, ret_v, ret_g,
     gdn_qkv, gdn_z, gdn_a, gdn_b, gates) = jnp.split(u @ w_in, IN_OFFSETS, axis=-1)

    xbc, new_ssm_conv = causal_conv(mb_xbc, s_ssm_conv, mb_conv_w)
    xbc = jax.nn.silu(xbc + mb_conv_b)
    xs, bm, cm = jnp.split(xbc, (BRANCH_W, BRANCH_W + MB_GROUPS * MB_STATE), axis=-1)
    xs = xs.reshape(bsz, seq, MB_HEADS, MB_HEADDIM)
    bm = jnp.repeat(bm.reshape(bsz, seq, MB_GROUPS, MB_STATE), MB_HEADS // MB_GROUPS, axis=2)
    cm = jnp.repeat(cm.reshape(bsz, seq, MB_GROUPS, MB_STATE), MB_HEADS // MB_GROUPS, axis=2)
    dt = jax.nn.softplus(mb_dt.astype(f32) + mb_dt_bias.astype(f32))
    log_a = -jnp.exp(mb_a_log.astype(f32)) * dt
    y_mb, new_ssm = chunked_scalar_decay(cm, bm * dt[..., None], xs, log_a, s_ssm)
    y_mb = y_mb + mb_d.astype(f32)[:, None] * xs
    y_mb = rmsnorm(y_mb.reshape(bsz, seq, BRANCH_W) * jax.nn.silu(mb_z.astype(f32)), mb_norm)

    lb = hg_lb.astype(f32)
    f_gate = lb + (1.0 - lb) * jax.nn.sigmoid(hg_f.astype(f32))
    shp = (bsz, seq, HG_HEADS, HG_DK)
    y_hg, new_hg = chunked_vector_decay(jax.nn.silu(hg_q).reshape(shp), (1.0 - f_gate).reshape(shp),
                                        hg_i.reshape(bsz, seq, HG_HEADS, HG_DV),
                                        jnp.log(f_gate).reshape(shp), s_hg)
    y_hg = rmsnorm(y_hg, hg_norm).reshape(bsz, seq, BRANCH_W) * jax.nn.silu(hg_g.astype(f32))

    q_r = rotary(ret_q.reshape(bsz, seq, RET_HEADS, RET_DK), pos)
    k_r = rotary(ret_k.reshape(bsz, seq, RET_HEADS, RET_DK), pos) * (RET_DK ** -0.5)
    log_gamma = jnp.log(1.0 - jnp.exp2(-5.0 - jnp.arange(RET_HEADS, dtype=f32)))
    y_r, new_ret = chunked_scalar_decay(q_r, k_r, ret_v.reshape(bsz, seq, RET_HEADS, RET_DV),
                                        jnp.broadcast_to(log_gamma, (bsz, seq, RET_HEADS)), s_ret)
    y_r = rmsnorm(y_r, None).reshape(bsz, seq, BRANCH_W) * jax.nn.silu(ret_g.astype(f32))

    qkv, new_gdn_conv = causal_conv(gdn_qkv, s_gdn_conv, gdn_conv_w)
    qkv = jax.nn.silu(qkv)
    q_d, k_d, v_d = jnp.split(qkv, (GDN_HEADS * GDN_DK, 2 * GDN_HEADS * GDN_DK), axis=-1)
    q_d = l2norm(q_d.reshape(bsz, seq, GDN_HEADS, GDN_DK)) * (GDN_DK ** -0.5)
    k_d = l2norm(k_d.reshape(bsz, seq, GDN_HEADS, GDN_DK))
    beta = jax.nn.sigmoid(gdn_b.astype(f32))
    log_g = -jnp.exp(gdn_a_log.astype(f32)) * jax.nn.softplus(gdn_a.astype(f32) + gdn_dt_bias.astype(f32))
    y_d, new_gdn = chunked_gated_delta(q_d, k_d, v_d.reshape(bsz, seq, GDN_HEADS, GDN_DV), log_g, beta, s_gdn)
    y_d = rmsnorm(y_d, gdn_norm).reshape(bsz, seq, BRANCH_W) * jax.nn.silu(gdn_z.astype(f32))

    gate = jax.nn.sigmoid(gates.astype(f32)).reshape(bsz, seq, N_BRANCH, D_MODEL)
    merged = jnp.zeros((bsz, seq, D_MODEL), f32)
    for n, yb in enumerate((y_mb, y_hg, y_r, y_d)):
        merged = merged + gate[:, :, n] * (yb.astype(x.dtype) @ w_branch[n])
    mixed = merged.astype(x.dtype) @ w_out
    h = h + rmsnorm(mixed, norm_g[3])

    h = h + 0.5 * rmsnorm(swiglu(rmsnorm(h, norm_g[4]), ffn_w_in[1], ffn_w_out[1]), norm_g[5])

    sd = x.dtype
    new_states = (new_ssm.astype(sd), new_ssm_conv.astype(sd), new_hg.astype(sd),
                  new_ret.astype(sd), new_gdn.astype(sd), new_gdn_conv.astype(sd))
    return h, new_states


def setup_inputs(seed: int = 0) -> dict:
    key = jax.random.key(seed)
    ks = jax.random.split(key, 26)
    f32 = jnp.float32

    def nrm(k, shape, scale):
        return jax.random.normal(k, shape, f32) * scale

    def dt_bias_init(k, shape):
        dt0 = jnp.exp(jax.random.uniform(k, shape, f32, math.log(1e-3), math.log(1e-1)))
        return dt0 + jnp.log(-jnp.expm1(-dt0))

    return {
        'x_prompt': nrm(ks[0], (BATCH, SEQ, D_MODEL), 1.0),
        'x_sample': nrm(ks[1], (DEC_BATCH, DEC_SEQ, D_MODEL), 1.0),
        'state_ssm': nrm(ks[2], (DEPTH, DEC_BATCH, MB_HEADS, MB_STATE, MB_HEADDIM), 0.5),
        'state_ssm_conv': nrm(ks[3], (DEPTH, DEC_BATCH, CONV_W - 1, MB_CONV_DIM), 1.0),
        'state_hgrn': nrm(ks[4], (DEPTH, DEC_BATCH, HG_HEADS, HG_DK, HG_DV), 0.5),
        'state_ret': nrm(ks[5], (DEPTH, DEC_BATCH, RET_HEADS, RET_DK, RET_DV), 0.5),
        'state_gdn': nrm(ks[6], (DEPTH, DEC_BATCH, GDN_HEADS, GDN_DK, GDN_DV), 0.1),
        'state_gdn_conv': nrm(ks[7], (DEPTH, DEC_BATCH, CONV_W - 1, GDN_CONV_DIM), 1.0),
        'norm_g': 1.0 + nrm(ks[8], (DEPTH, 6, D_MODEL), 0.02),
        'ffn_w_in': nrm(ks[9], (DEPTH, 2, D_MODEL, 2 * D_FF), D_MODEL ** -0.5),
        'ffn_w_out': nrm(ks[10], (DEPTH, 2, D_FF, D_MODEL), D_FF ** -0.5),
        'w_in': nrm(ks[11], (DEPTH, D_MODEL, D_IN), D_MODEL ** -0.5),
        'w_branch': nrm(ks[12], (DEPTH, N_BRANCH, BRANCH_W, D_MODEL), BRANCH_W ** -0.5),
        'w_out': nrm(ks[13], (DEPTH, D_MODEL, D_MODEL), D_MODEL ** -0.5),
        'mb_conv_w': nrm(ks[14], (DEPTH, CONV_W, MB_CONV_DIM), CONV_W ** -0.5),
        'mb_conv_b': nrm(ks[15], (DEPTH, MB_CONV_DIM), 0.02),
        'mb_a_log': jnp.log(jax.random.uniform(ks[16], (DEPTH, MB_HEADS), f32, 1.0, 16.0)),
        'mb_dt_bias': dt_bias_init(ks[17], (DEPTH, MB_HEADS)),
        'mb_d': 1.0 + nrm(ks[18], (DEPTH, MB_HEADS), 0.1),
        'mb_norm': 1.0 + nrm(ks[19], (DEPTH, BRANCH_W), 0.02),
        'hg_lb_logits': nrm(ks[20], (DEPTH, HG_HEADS * HG_DK), 1.0),
        'hg_norm': 1.0 + nrm(ks[21], (DEPTH, HG_DV), 0.02),
        'gdn_conv_w': nrm(ks[22], (DEPTH, CONV_W, GDN_CONV_DIM), CONV_W ** -0.5),
        'gdn_a_log': jnp.log(jax.random.uniform(ks[23], (DEPTH, GDN_HEADS), f32, 1.0, 16.0)),
        'gdn_dt_bias': dt_bias_init(ks[24], (DEPTH, GDN_HEADS)),
        'gdn_norm': 1.0 + nrm(ks[25], (DEPTH, GDN_DV), 0.02),
    }


def reference(x_prompt, x_sample, state_ssm, state_ssm_conv, state_hgrn, state_ret, state_gdn,
              state_gdn_conv, norm_g, ffn_w_in, ffn_w_out, w_in, w_branch, w_out,
              mb_conv_w, mb_conv_b, mb_a_log, mb_dt_bias, mb_d, mb_norm,
              hg_lb_logits, hg_norm, gdn_conv_w, gdn_a_log, gdn_dt_bias, gdn_norm):
    f32 = jnp.float32
    lb_all = jnp.cumsum(jax.nn.softmax(hg_lb_logits.astype(f32), axis=0), axis=0)
    lb_all = lb_all - lb_all[0:1]

    pos_p = jnp.arange(x_prompt.shape[1])
    pos_s = PAST_LEN + jnp.arange(x_sample.shape[1])
    bp = x_prompt.shape[0]
    zero_states = (
        jnp.zeros((bp, MB_HEADS, MB_STATE, MB_HEADDIM), f32),
        jnp.zeros((bp, CONV_W - 1, MB_CONV_DIM), x_prompt.dtype),
        jnp.zeros((bp, HG_HEADS, HG_DK, HG_DV), f32),
        jnp.zeros((bp, RET_HEADS, RET_DK, RET_DV), f32),
        jnp.zeros((bp, GDN_HEADS, GDN_DK, GDN_DV), f32),
        jnp.zeros((bp, CONV_W - 1, GDN_CONV_DIM), x_prompt.dtype),
    )

    hp, hs = x_prompt, x_sample
    st_prompt, st_sample = [], []
    for l in range(DEPTH):
        lw = (norm_g[l], ffn_w_in[l], ffn_w_out[l], w_in[l], w_branch[l], w_out[l],
              mb_conv_w[l], mb_conv_b[l], mb_a_log[l], mb_dt_bias[l], mb_d[l], mb_norm[l],
              lb_all[l], hg_norm[l], gdn_conv_w[l], gdn_a_log[l], gdn_dt_bias[l], gdn_norm[l])
        hp, sp = hybrid_layer(hp, pos_p, zero_states, lw)
        past = (state_ssm[l], state_ssm_conv[l], state_hgrn[l], state_ret[l], state_gdn[l], state_gdn_conv[l])
        hs, ss = hybrid_layer(hs, pos_s, past, lw)
        st_prompt.append(sp)
        st_sample.append(ss)

    def stack(sts, i):
        return jnp.stack([s[i] for s in sts], axis=0)

    return (hp, hs,
            stack(st_prompt, 0), stack(st_prompt, 1), stack(st_prompt, 2),
            stack(st_prompt, 3), stack(st_prompt, 4), stack(st_prompt, 5),
            stack(st_sample, 0), stack(st_sample, 1), stack(st_sample, 2),
            stack(st_sample, 3), stack(st_sample, 4), stack(st_sample, 5))
```

```python
import functools
import math

import numpy as np
import jax
import jax.numpy as jnp
from jax import lax
from jax.experimental import pallas as pl
from jax.experimental.pallas import tpu as pltpu

f32 = jnp.float32
bf16 = jnp.bfloat16

D_MODEL = 1024
D_FF = 2816
BRANCH_W = 512
N_BRANCH = 4
CONV_W = 4
EPS = 1e-6
ROPE_BASE = 10000.0
PAST_LEN = 16384

MB_HEADS, MB_HEADDIM, MB_STATE, MB_GROUPS = 8, 64, 64, 2
MB_CONV_DIM = BRANCH_W + 2 * MB_GROUPS * MB_STATE
HG_HEADS, HG_DK, HG_DV = 4, 128, 128
RET_HEADS, RET_DK, RET_DV = 4, 64, 128
GDN_HEADS, GDN_DK, GDN_DV = 4, 128, 128
GDN_CONV_DIM = 2 * GDN_HEADS * GDN_DK + GDN_HEADS * GDN_DV

IN_SPLITS = (
    BRANCH_W, MB_CONV_DIM, MB_HEADS,
    HG_HEADS * HG_DK, HG_HEADS * HG_DK, BRANCH_W, BRANCH_W,
    RET_HEADS * RET_DK, RET_HEADS * RET_DK, BRANCH_W, BRANCH_W,
    GDN_CONV_DIM, BRANCH_W, GDN_HEADS, GDN_HEADS,
    N_BRANCH * D_MODEL,
)
IN_OFFSETS = tuple(int(v) for v in np.cumsum((0,) + IN_SPLITS))

LANES = 128
SUBLANES = 8
CONV_PAD = SUBLANES
VMEM_LIMIT = 56 * 1024 * 1024
FFN_TM = 512
FFN_TF = D_FF // 2
MERGE_TM = 512
HG_SUB = 16


def _dot(a, b):
    return jnp.dot(a.astype(bf16), b.astype(bf16), preferred_element_type=f32)


def _dot_nt(a, b):
    return lax.dot_general(a.astype(bf16), b.astype(bf16), (((1,), (1,)), ((), ())), preferred_element_type=f32)


def _dot_tn(a, b):
    return lax.dot_general(a.astype(bf16), b.astype(bf16), (((0,), (0,)), ((), ())), preferred_element_type=f32)


def _split2(x):
    hi = x.astype(bf16)
    lo = (x - hi.astype(f32)).astype(bf16)
    return hi, lo


def _split3(x):
    hi = x.astype(bf16)
    r = x - hi.astype(f32)
    mid = r.astype(bf16)
    lo = (r - mid.astype(f32)).astype(bf16)
    return hi, mid, lo


def _dot_sel_left(m01, x):
    hi, mid, lo = _split3(x)
    return (jnp.dot(m01, hi, preferred_element_type=f32) + jnp.dot(m01, mid, preferred_element_type=f32)
            + jnp.dot(m01, lo, preferred_element_type=f32))


def _dot_sel_right(x, m01):
    hi, lo = _split2(x)
    return jnp.dot(hi, m01, preferred_element_type=f32) + jnp.dot(lo, m01, preferred_element_type=f32)


def _dot_hi(a, b):
    ah, al = _split2(a)
    bh, bl = _split2(b)
    return (jnp.dot(ah, bh, preferred_element_type=f32) + jnp.dot(ah, bl, preferred_element_type=f32)
            + jnp.dot(al, bh, preferred_element_type=f32))


def _rms(x):
    return x * lax.rsqrt(jnp.mean(x * x, axis=-1, keepdims=True) + EPS)


def _silu(x):
    return x * jax.nn.sigmoid(x)


def _iota2(n, m, axis):
    return lax.broadcasted_iota(jnp.int32, (n, m), axis)


def _ltri(c):
    return jnp.where(_iota2(c, c, 0) >= _iota2(c, c, 1), 1.0, 0.0).astype(bf16)


def _row_bcast(col, c):
    dg = jnp.where(_iota2(c, c, 0) == _iota2(c, c, 1), jnp.broadcast_to(col, (c, c)), 0.0)
    return _dot_sel_left(jnp.ones((c, c), bf16), dg)


def _decay_matrix(col, c):
    diff = col - _row_bcast(col, c)
    return jnp.exp(jnp.where(_iota2(c, c, 0) >= _iota2(c, c, 1), diff, -jnp.inf))


def _causal_conv(xp_ref, w_ref, c):
    y = xp_ref[:, pl.ds(CONV_PAD - 3, c), :] * w_ref[0:1, :]
    for j in range(1, CONV_W):
        y = y + xp_ref[:, pl.ds(CONV_PAD - 3 + j, c), :] * w_ref[j:j + 1, :]
    return y


def _ffn_body(x_ref, g_ref, wg_ref, wu_ref, wo_ref, o_ref, xn_ref, acc_ref):
    j = pl.program_id(1)

    @pl.when(j == 0)
    def _():
        xn_ref[...] = (_rms(x_ref[...]) * g_ref[0:1, :]).astype(bf16)
        acc_ref[...] = jnp.zeros_like(acc_ref)

    xn = xn_ref[...]
    gate = jnp.dot(xn, wg_ref[...], preferred_element_type=f32)
    up = jnp.dot(xn, wu_ref[...], preferred_element_type=f32)
    act = (_silu(gate) * up).astype(bf16)
    acc_ref[...] += jnp.dot(act, wo_ref[...], preferred_element_type=f32)

    @pl.when(j == pl.num_programs(1) - 1)
    def _():
        o_ref[...] = x_ref[...] + 0.5 * (_rms(acc_ref[...]) * g_ref[1:2, :])


def _ffn(x, g2, w_in, w_out):
    t = x.shape[0]
    tm = min(FFN_TM, t)
    nf = D_FF // FFN_TF
    return pl.pallas_call(
        _ffn_body,
        grid=(t // tm, nf),
        in_specs=[
            pl.BlockSpec((tm, D_MODEL), lambda i, j: (i, 0)),
            pl.BlockSpec((2, D_MODEL), lambda i, j: (0, 0)),
            pl.BlockSpec((D_MODEL, FFN_TF), lambda i, j: (0, j)),
            pl.BlockSpec((D_MODEL, FFN_TF), lambda i, j: (0, j + nf)),
            pl.BlockSpec((FFN_TF, D_MODEL), lambda i, j: (j, 0)),
        ],
        out_specs=pl.BlockSpec((tm, D_MODEL), lambda i, j: (i, 0)),
        out_shape=jax.ShapeDtypeStruct((t, D_MODEL), f32),
        scratch_shapes=[pltpu.VMEM((tm, D_MODEL), bf16), pltpu.VMEM((tm, D_MODEL), f32)],
        compiler_params=pltpu.CompilerParams(dimension_semantics=("arbitrary", "arbitrary"),
                                             vmem_limit_bytes=VMEM_LIMIT),
        name="ffn",
    )(x, g2, w_in, w_in, w_out)


def _merge_body(h_ref, g_ref, y0_ref, y1_ref, y2_ref, y3_ref, wg_ref, wb_ref, wo_ref, o_ref):
    h = h_ref[...]
    u = (_rms(h) * g_ref[0:1, :]).astype(bf16)
    merged = jnp.zeros(h.shape, f32)
    for n, y_ref in enumerate((y0_ref, y1_ref, y2_ref, y3_ref)):
        gate = jax.nn.sigmoid(jnp.dot(u, wg_ref[:, n * D_MODEL:(n + 1) * D_MODEL], preferred_element_type=f32))
        merged = merged + gate * jnp.dot(y_ref[...], wb_ref[n], preferred_element_type=f32)
    mixed = jnp.dot(merged.astype(bf16), wo_ref[...], preferred_element_type=f32)
    o_ref[...] = h + _rms(mixed) * g_ref[1:2, :]


def _merge(h, g2, ys, w_gate, w_branch, w_out):
    t = h.shape[0]
    tm = min(MERGE_TM, t)
    row = lambda i: (i, 0)
    return pl.pallas_call(
        _merge_body,
        grid=(t // tm,),
        in_specs=[
            pl.BlockSpec((tm, D_MODEL), row),
            pl.BlockSpec((2, D_MODEL), lambda i: (0, 0)),
            pl.BlockSpec((tm, BRANCH_W), row), pl.BlockSpec((tm, BRANCH_W), row),
            pl.BlockSpec((tm, BRANCH_W), row), pl.BlockSpec((tm, BRANCH_W), row),
            pl.BlockSpec((D_MODEL, N_BRANCH * D_MODEL), lambda i: (0, 0)),
            pl.BlockSpec((N_BRANCH, BRANCH_W, D_MODEL), lambda i: (0, 0, 0)),
            pl.BlockSpec((D_MODEL, D_MODEL), lambda i: (0, 0)),
        ],
        out_specs=pl.BlockSpec((tm, D_MODEL), row),
        out_shape=jax.ShapeDtypeStruct((t, D_MODEL), f32),
        compiler_params=pltpu.CompilerParams(dimension_semantics=("arbitrary",), vmem_limit_bytes=VMEM_LIMIT),
        name="merge",
    )(h, g2, *ys, w_gate, w_branch, w_out)


def _normed_input(h_ref, g_ref):
    bt, c, d = h_ref.shape
    return (_rms(h_ref[...].reshape(bt * c, d)) * g_ref[...]).astype(bf16)


def _const_spec(a):
    nd = a.ndim
    return pl.BlockSpec(a.shape, lambda i, j, _nd=nd: (0,) * _nd)


def _batch_spec(bt, a):
    nd = a.ndim
    return pl.BlockSpec((bt,) + a.shape[1:], lambda i, j, _nd=nd: (i,) + (0,) * (_nd - 1))


def _mixer_call(body, name, h, consts, batch_ins, c, bt, batch_outs, scratch):
    b, l, d = h.shape
    seq = lambda i, j: (i, j, 0)
    out_shape = [jax.ShapeDtypeStruct((b, l, BRANCH_W), bf16)] + [jax.ShapeDtypeStruct(s, f32) for s in batch_outs]
    out_specs = [pl.BlockSpec((bt, c, BRANCH_W), seq)] + [
        pl.BlockSpec((bt,) + s[1:], lambda i, j, _nd=len(s): (i,) + (0,) * (_nd - 1)) for s in batch_outs]
    return pl.pallas_call(
        body,
        grid=(b // bt, l // c),
        in_specs=[pl.BlockSpec((bt, c, d), seq)] + [_const_spec(a) for a in consts]
                 + [_batch_spec(bt, a) for a in batch_ins],
        out_specs=out_specs,
        out_shape=out_shape,
        scratch_shapes=scratch,
        compiler_params=pltpu.CompilerParams(dimension_semantics=("arbitrary", "arbitrary"),
                                             vmem_limit_bytes=VMEM_LIMIT),
        name=name,
    )(h, *consts, *batch_ins)


def _mamba_body(h_ref, g_ref, wz_ref, wx_ref, wdt_ref, cw_ref, cb_ref, dtb_ref, alog_ref, d_ref, nrm_ref,
                s0_ref, c0_ref, y_ref, so_ref, co_ref, s_scr, xp_scr, xa_scr, z_scr, dt_scr, la_scr):
    bt, c, _ = h_ref.shape
    j = pl.program_id(1)
    npair = MB_HEADS // 2

    @pl.when(j == 0)
    def _():
        s_scr[...] = jnp.zeros_like(s_scr)
        for p in range(npair):
            g = (2 * p) // (MB_HEADS // MB_GROUPS)
            s_scr[:, p, g * MB_STATE:(g + 1) * MB_STATE, :] = s0_ref[:, p]
        xp_scr[:, 0:CONV_PAD, :] = c0_ref[...]

    u = _normed_input(h_ref, g_ref)
    z_scr[...] = jnp.dot(u, wz_ref[...], preferred_element_type=f32).reshape(bt, c, BRANCH_W)
    xp_scr[:, CONV_PAD:CONV_PAD + c, :] = jnp.dot(u, wx_ref[...], preferred_element_type=f32).reshape(
        bt, c, MB_CONV_DIM)
    dt = jax.nn.softplus(jnp.dot(u, wdt_ref[...], preferred_element_type=f32) + dtb_ref[...])
    dt_scr[...] = dt.reshape(bt, c, LANES)
    la_scr[...] = (-jnp.exp(alog_ref[...]) * dt).reshape(bt, c, LANES)
    xa_scr[...] = _silu(_causal_conv(xp_scr, cw_ref, c) + cb_ref[...])
    tail = xp_scr[:, c:c + CONV_PAD, :]
    xp_scr[:, 0:CONV_PAD, :] = tail
    co_ref[...] = tail

    lane = _iota2(c, LANES, 1)
    lo_half = lane < MB_HEADDIM
    ltri = _ltri(c)

    def per_seq(b, carry):
        xa = xa_scr[b]
        xs = xa[:, :BRANCH_W]
        bm = xa[:, BRANCH_W:BRANCH_W + LANES]
        cm = xa[:, BRANCH_W + LANES:BRANCH_W + 2 * LANES]
        dt_b = dt_scr[b]
        cum = _dot_sel_left(ltri, la_scr[b])
        clast = cum[c - 1:c, :]
        cm_g = [jnp.where((lane >> (MB_STATE.bit_length() - 1)) == g, cm, 0.0) for g in range(MB_GROUPS)]
        gmat = [_dot_nt(cm_g[g], bm) for g in range(MB_GROUPS)]
        ys = []
        for p in range(npair):
            h0, h1 = 2 * p, 2 * p + 1
            g = h0 // (MB_HEADS // MB_GROUPS)
            xblk = xs[:, p * LANES:(p + 1) * LANES]
            pair = lambda a: jnp.where(lo_half, a[:, h0:h0 + 1], a[:, h1:h1 + 1])
            cum_p = pair(cum)
            clast_p = jnp.where(lo_half[0:1], clast[:, h0:h0 + 1], clast[:, h1:h1 + 1])
            v = xblk * pair(dt_b)
            sc0 = gmat[g] * _decay_matrix(cum[:, h0:h0 + 1], c)
            sc1 = gmat[g] * _decay_matrix(cum[:, h1:h1 + 1], c)
            s_old = s_scr[b, p]
            o = (_dot(sc0, jnp.where(lo_half, v, 0.0)) + _dot(sc1, jnp.where(lo_half, 0.0, v))
                 + _dot(cm_g[g], s_old) * jnp.exp(cum_p))
            s_scr[b, p] = s_old * jnp.exp(clast_p) + _dot_tn(bm, v * jnp.exp(clast_p - cum_p))
            ys.append(o + d_ref[:, p * LANES:(p + 1) * LANES] * xblk)
        y = jnp.concatenate(ys, axis=-1) * _silu(z_scr[b])
        y_ref[b] = (_rms(y) * nrm_ref[...]).astype(bf16)
        return carry

    lax.fori_loop(0, bt, per_seq, 0)

    @pl.when(j == pl.num_programs(1) - 1)
    def _():
        for p in range(npair):
            g = (2 * p) // (MB_HEADS // MB_GROUPS)
            so_ref[:, p] = s_scr[:, p, g * MB_STATE:(g + 1) * MB_STATE, :]


def _mamba(h, g2, w, s0_pairs, conv0, c, bt):
    b, l, _ = h.shape
    npair = MB_HEADS // 2
    consts = [g2, w["wz"], w["wx"], w["wdt"], w["cw"], w["cb"], w["dtb"], w["alog"], w["d"], w["nrm"]]
    scratch = [
        pltpu.VMEM((bt, npair, LANES, LANES), f32),
        pltpu.VMEM((bt, CONV_PAD + c, MB_CONV_DIM), f32),
        pltpu.VMEM((bt, c, MB_CONV_DIM), f32),
        pltpu.VMEM((bt, c, BRANCH_W), f32),
        pltpu.VMEM((bt, c, LANES), f32),
        pltpu.VMEM((bt, c, LANES), f32),
    ]
    return _mixer_call(_mamba_body, "mamba", h, consts, [s0_pairs, conv0], c, bt,
                       [(b, npair, MB_STATE, LANES), (b, CONV_PAD, MB_CONV_DIM)], scratch)


def _ret_body(h_ref, g_ref, wq_ref, wk_ref, wv_ref, wg_ref, cos_ref, sin_ref, lg_ref,
              s0_ref, y_ref, so_ref, s_scr, q_scr, k_scr, v_scr, gt_scr):
    bt, c, _ = h_ref.shape
    j = pl.program_id(1)
    kw = RET_HEADS * RET_DK
    hw = RET_DK // 2

    @pl.when(j == 0)
    def _():
        s_scr[...] = jnp.zeros_like(s_scr)
        for h in range(RET_HEADS):
            s_scr[:, h, h * hw:(h + 1) * hw, :] = s0_ref[:, h, 0:hw, :]
            s_scr[:, h, LANES + h * hw:LANES + (h + 1) * hw, :] = s0_ref[:, h, hw:2 * hw, :]

    u = _normed_input(h_ref, g_ref)
    cos = cos_ref[...]
    sin = sin_ref[...]

    def rot(x):
        x = x.reshape(bt, c, kw)
        x1, x2 = x[:, :, :LANES], x[:, :, LANES:]
        return jnp.concatenate([x1 * cos - x2 * sin, x1 * sin + x2 * cos], axis=-1)

    q_scr[...] = rot(jnp.dot(u, wq_ref[...], preferred_element_type=f32))
    k_scr[...] = rot(jnp.dot(u, wk_ref[...], preferred_element_type=f32)) * (RET_DK ** -0.5)
    v_scr[...] = jnp.dot(u, wv_ref[...], preferred_element_type=f32).reshape(bt, c, BRANCH_W)
    gt_scr[...] = jnp.dot(u, wg_ref[...], preferred_element_type=f32).reshape(bt, c, BRANCH_W)

    ii = _iota2(c, c, 0)
    jj = _iota2(c, c, 1)
    rowi = _iota2(c, 1, 0)
    klane = _iota2(c, kw, 1)
    dec, e_in, e_out, e_all, hmask = [], [], [], [], []
    for h in range(RET_HEADS):
        lg = lg_ref[:, h:h + 1]
        dec.append(jnp.exp(jnp.where(ii >= jj, (ii - jj).astype(f32) * lg, -jnp.inf)))
        e_in.append(jnp.exp((rowi + 1).astype(f32) * lg))
        e_out.append(jnp.exp((c - 1 - rowi).astype(f32) * lg))
        e_all.append(jnp.exp(float(c) * lg))
        hmask.append(((klane & (LANES - 1)) >> (hw.bit_length() - 1)) == h)

    def per_seq(b, carry):
        q = q_scr[b]
        k = k_scr[b]
        v = v_scr[b]
        ys = []
        for h in range(RET_HEADS):
            vh = v[:, h * RET_DV:(h + 1) * RET_DV]
            qm = jnp.where(hmask[h], q, 0.0)
            km = jnp.where(hmask[h], k, 0.0)
            s_old = s_scr[b, h]
            o = _dot(_dot_nt(qm, k) * dec[h], vh) + _dot(qm * e_in[h], s_old)
            s_scr[b, h] = s_old * e_all[h] + _dot_tn(km * e_out[h], vh)
            ys.append(_rms(o))
        y_ref[b] = (jnp.concatenate(ys, axis=-1) * _silu(gt_scr[b])).astype(bf16)
        return carry

    lax.fori_loop(0, bt, per_seq, 0)

    @pl.when(j == pl.num_programs(1) - 1)
    def _():
        for h in range(RET_HEADS):
            so_ref[:, h, 0:hw, :] = s_scr[:, h, h * hw:(h + 1) * hw, :]
            so_ref[:, h, hw:2 * hw, :] = s_scr[:, h, LANES + h * hw:LANES + (h + 1) * hw, :]


def _ret(h, g2, w, cos, sin, lg, s0, c, bt):
    b, l, _ = h.shape
    kw = RET_HEADS * RET_DK
    consts = [g2, w["wq"], w["wk"], w["wv"], w["wg"]]
    scratch = [
        pltpu.VMEM((bt, RET_HEADS, kw, RET_DV), f32),
        pltpu.VMEM((bt, c, kw), f32),
        pltpu.VMEM((bt, c, kw), f32),
        pltpu.VMEM((bt, c, BRANCH_W), f32),
        pltpu.VMEM((bt, c, BRANCH_W), f32),
    ]
    seq = lambda i, j: (i, j, 0)
    out_shape = [jax.ShapeDtypeStruct((b, l, BRANCH_W), bf16), jax.ShapeDtypeStruct(s0.shape, f32)]
    st_spec = pl.BlockSpec((bt,) + s0.shape[1:], lambda i, j: (i, 0, 0, 0))
    return pl.pallas_call(
        _ret_body,
        grid=(b // bt, l // c),
        in_specs=[pl.BlockSpec((bt, c, D_MODEL), seq)] + [_const_spec(a) for a in consts]
                 + [pl.BlockSpec((c, LANES), lambda i, j: (j, 0)), pl.BlockSpec((c, LANES), lambda i, j: (j, 0)),
                    _const_spec(lg), st_spec],
        out_specs=[pl.BlockSpec((bt, c, BRANCH_W), seq), st_spec],
        out_shape=out_shape,
        scratch_shapes=scratch,
        compiler_params=pltpu.CompilerParams(dimension_semantics=("arbitrary", "arbitrary"),
                                             vmem_limit_bytes=VMEM_LIMIT),
        name="ret",
    )(h, *consts, cos, sin, lg, s0)


def _hgrn_body(h_ref, g_ref, wq_ref, wf_ref, wi_ref, wg_ref, lb_ref, nrm_ref, oh_ref,
               s0_ref, y_ref, so_ref, st_scr, q_scr, k_scr, lf_scr, v_scr, gt_scr):
    bt, c, _ = h_ref.shape
    j = pl.program_id(1)
    sc = min(HG_SUB, c)
    ns = c // sc

    @pl.when(j == 0)
    def _():
        def init(b, carry):
            for h in range(HG_HEADS):
                st_scr[b, h] = s0_ref[b, h].T
            return carry
        lax.fori_loop(0, bt, init, 0)

    u = _normed_input(h_ref, g_ref)
    shp = (bt, c, BRANCH_W)
    q_scr[...] = _silu(jnp.dot(u, wq_ref[...], preferred_element_type=f32)).reshape(shp)
    lb = lb_ref[...]
    fg = lb + (1.0 - lb) * jax.nn.sigmoid(jnp.dot(u, wf_ref[...], preferred_element_type=f32))
    k_scr[...] = (1.0 - fg).reshape(shp)
    lf_scr[...] = jnp.log(fg).reshape(shp)
    v_scr[...] = jnp.dot(u, wi_ref[...], preferred_element_type=f32).reshape(shp)
    gt_scr[...] = jnp.dot(u, wg_ref[...], preferred_element_type=f32).reshape(shp)

    ltri = _ltri(c)
    sub_i = _iota2(sc, HG_DK, 0)
    row_j = _iota2(c, HG_DK, 0)

    def per_seq(b, carry):
        cum_all = _dot_sel_left(ltri, lf_scr[b])
        q_all = q_scr[b]
        k_all = k_scr[b]
        v_all = v_scr[b]
        ys = []
        for h in range(HG_HEADS):
            sl = slice(h * HG_DK, (h + 1) * HG_DK)
            q, k, v, cum = q_all[:, sl], k_all[:, sl], v_all[:, sl], cum_all[:, sl]
            rows = []
            for s in range(ns):
                r0 = s * sc
                q_s = q[r0:r0 + sc]
                cum_s = cum[r0:r0 + sc]
                pieces = []
                for t in range(sc):
                    kt = k[r0 + t:r0 + t + 1]
                    ct = cum[r0 + t:r0 + t + 1]
                    pieces.append(jnp.where(sub_i >= t, q_s * kt * jnp.exp(cum_s - ct), 0.0))
                blk = _dot_sel_right(jnp.concatenate(pieces, axis=-1), oh_ref[s])
                if s > 0:
                    bnd = cum[r0 - 1:r0]
                    q_t = q_s * jnp.exp(cum_s - bnd)
                    k_t = jnp.where(row_j < r0, k * jnp.exp(jnp.minimum(bnd - cum, 0.0)), 0.0)
                    blk = blk + _dot_nt(q_t, k_t)
                rows.append(blk)
            scores = rows[0] if ns == 1 else jnp.concatenate(rows, axis=0)
            st_old = st_scr[b, h]
            clast = cum[c - 1:c]
            o = _dot(scores, v) + _dot_nt(q * jnp.exp(cum), st_old)
            st_scr[b, h] = st_old * jnp.exp(clast) + _dot_tn(v, k * jnp.exp(clast - cum))
            ys.append(_rms(o) * nrm_ref[...])
        y_ref[b] = (jnp.concatenate(ys, axis=-1) * _silu(gt_scr[b])).astype(bf16)
        return carry

    lax.fori_loop(0, bt, per_seq, 0)

    @pl.when(j == pl.num_programs(1) - 1)
    def _():
        def fin(b, carry):
            for h in range(HG_HEADS):
                so_ref[b, h] = st_scr[b, h].T
            return carry
        lax.fori_loop(0, bt, fin, 0)


def _hgrn_onehot(c):
    sc = min(HG_SUB, c)
    ns = c // sc
    oh = np.zeros((ns, sc * HG_DK, c), np.float32)
    for s in range(ns):
        for t in range(sc):
            oh[s, t * HG_DK:(t + 1) * HG_DK, s * sc + t] = 1.0
    return jnp.asarray(oh, bf16)


def _hgrn(h, g2, w, s0, c, bt):
    b, l, _ = h.shape
    consts = [g2, w["wq"], w["wf"], w["wi"], w["wg"], w["lb"], w["nrm"], _hgrn_onehot(c)]
    scratch = [pltpu.VMEM((bt, HG_HEADS, HG_DV, HG_DK), f32)] + [pltpu.VMEM((bt, c, BRANCH_W), f32)] * 5
    return _mixer_call(_hgrn_body, "hgrn", h, consts, [s0], c, bt, [s0.shape], scratch)


def _unit_lower_inverse(a, c):
    ii = _iota2(c, c, 0)
    jj = _iota2(c, c, 1)
    t = jnp.where(ii == jj, 1.0, 0.0)
    m = 1
    while m < c:
        lm = m.bit_length() - 1
        same_block = (ii >> (lm + 1)) == (jj >> (lm + 1))
        f = jnp.where(same_block & (((ii >> lm) & 1) == 1) & (((jj >> lm) & 1) == 0), a, 0.0)
        if m == 1:
            t = t - f
        else:
            t = t - _dot_hi(t, _dot_hi(f, t))
        m *= 2
    return t


def _gdn_body(h_ref, g_ref, wqkv_ref, wz_ref, wab_ref, cw_ref, dtb_ref, alog_ref, nrm_ref,
              s0_ref, c0_ref, y_ref, so_ref, co_ref, s_scr, xp_scr, qkv_scr, z_scr, lg_scr, beta_scr):
    bt, c, _ = h_ref.shape
    j = pl.program_id(1)
    hk = GDN_HEADS * GDN_DK

    @pl.when(j == 0)
    def _():
        s_scr[...] = s0_ref[...]
        xp_scr[:, 0:CONV_PAD, :] = c0_ref[...]

    u = _normed_input(h_ref, g_ref)
    xp_scr[:, CONV_PAD:CONV_PAD + c, :] = jnp.dot(u, wqkv_ref[...], preferred_element_type=f32).reshape(
        bt, c, GDN_CONV_DIM)
    z_scr[...] = jnp.dot(u, wz_ref[...], preferred_element_type=f32).reshape(bt, c, BRANCH_W)
    ab = jnp.dot(u, wab_ref[...], preferred_element_type=f32)
    lg_scr[...] = (-jnp.exp(alog_ref[...]) * jax.nn.softplus(ab + dtb_ref[...])).reshape(bt, c, LANES)
    beta_scr[...] = jax.nn.sigmoid(ab).reshape(bt, c, LANES)
    qkv_scr[...] = _silu(_causal_conv(xp_scr, cw_ref, c))
    tail = xp_scr[:, c:c + CONV_PAD, :]
    xp_scr[:, 0:CONV_PAD, :] = tail
    co_ref[...] = tail

    ltri = _ltri(c)
    ii = _iota2(c, c, 0)
    jj = _iota2(c, c, 1)

    def per_seq(b, carry):
        qkv = qkv_scr[b]
        beta_b = beta_scr[b]
        cum_b = _dot_sel_left(ltri, lg_scr[b])
        ys = []
        for h in range(GDN_HEADS):
            q = qkv[:, h * GDN_DK:(h + 1) * GDN_DK]
            k = qkv[:, hk + h * GDN_DK:hk + (h + 1) * GDN_DK]
            v = qkv[:, 2 * hk + h * GDN_DV:2 * hk + (h + 1) * GDN_DV]
            q = q * lax.rsqrt(jnp.sum(q * q, axis=-1, keepdims=True) + EPS) * (GDN_DK ** -0.5)
            k = k * lax.rsqrt(jnp.sum(k * k, axis=-1, keepdims=True) + EPS)
            cum = cum_b[:, h:h + 1]
            beta = beta_b[:, GDN_HEADS + h:GDN_HEADS + h + 1]
            dec = _decay_matrix(cum, c)
            a_mat = jnp.where(ii > jj, beta * dec * _dot_nt(k, k), 0.0)
            t_inv = _unit_lower_inverse(a_mat, c)
            e_in = jnp.exp(cum)
            rhs = jnp.concatenate([v * beta, k * (e_in * beta)], axis=-1)
            sol = _dot_hi(t_inv, rhs)
            s_old = s_scr[b, h]
            w_new = sol[:, :GDN_DV] - _dot(sol[:, GDN_DV:], s_old)
            o = _dot(_dot_nt(q, k) * dec, w_new) + _dot(q * e_in, s_old)
            clast = cum[c - 1:c]
            s_scr[b, h] = s_old * jnp.exp(clast) + _dot_tn(k * jnp.exp(clast - cum), w_new)
            ys.append(_rms(o) * nrm_ref[...])
        y_ref[b] = (jnp.concatenate(ys, axis=-1) * _silu(z_scr[b])).astype(bf16)
        return carry

    lax.fori_loop(0, bt, per_seq, 0)

    @pl.when(j == pl.num_programs(1) - 1)
    def _():
        so_ref[...] = s_scr[...]


def _gdn(h, g2, w, s0, conv0, c, bt):
    b, l, _ = h.shape
    consts = [g2, w["wqkv"], w["wz"], w["wab"], w["cw"], w["dtb"], w["alog"], w["nrm"]]
    scratch = [
        pltpu.VMEM((bt, GDN_HEADS, GDN_DK, GDN_DV), f32),
        pltpu.VMEM((bt, CONV_PAD + c, GDN_CONV_DIM), f32),
        pltpu.VMEM((bt, c, GDN_CONV_DIM), f32),
        pltpu.VMEM((bt, c, BRANCH_W), f32),
        pltpu.VMEM((bt, c, LANES), f32),
        pltpu.VMEM((bt, c, LANES), f32),
    ]
    return _mixer_call(_gdn_body, "gdn", h, consts, [s0, conv0], c, bt,
                       [s0.shape, (b, CONV_PAD, GDN_CONV_DIM)], scratch)


def _pad_lanes(v, offset=0):
    row = jnp.zeros((1, LANES), f32)
    return row.at[0, offset:offset + v.shape[0]].set(v.astype(f32))


def _layer_weights(l, norm_g, ffn_w_in, ffn_w_out, w_in, w_branch, w_out, mb_conv_w, mb_conv_b, mb_a_log,
                   mb_dt_bias, mb_d, mb_norm, lb_all, hg_norm, gdn_conv_w, gdn_a_log, gdn_dt_bias, gdn_norm):
    wi = w_in[l]
    col = lambda n: wi[:, IN_OFFSETS[n]:IN_OFFSETS[n + 1]]
    colb = lambda n: col(n).astype(bf16)
    pad_cols = lambda a: jnp.pad(a, ((0, 0), (0, LANES - a.shape[1]))).astype(bf16)
    hw = RET_DK // 2
    perm = np.arange(RET_HEADS * RET_DK).reshape(RET_HEADS, 2, hw).transpose(1, 0, 2).reshape(-1)
    row = lambda v: v.astype(f32).reshape(1, -1)
    return {
        "g_ffn1": norm_g[l, 0:2], "g_mix": norm_g[l, 2:4], "g_ffn2": norm_g[l, 4:6], "g_u": norm_g[l, 2:3],
        "ffn_in": ffn_w_in[l].astype(bf16), "ffn_out": ffn_w_out[l].astype(bf16),
        "mamba": {"wz": colb(0), "wx": colb(1), "wdt": pad_cols(col(2)), "cw": mb_conv_w[l], "cb": row(mb_conv_b[l]),
                  "dtb": _pad_lanes(mb_dt_bias[l]), "alog": _pad_lanes(mb_a_log[l]),
                  "d": row(jnp.repeat(mb_d[l], MB_HEADDIM)), "nrm": row(mb_norm[l])},
        "hgrn": {"wq": colb(3), "wf": colb(4), "wi": colb(5), "wg": colb(6), "lb": row(lb_all[l]),
                 "nrm": row(hg_norm[l])},
        "ret": {"wq": col(7)[:, perm].astype(bf16), "wk": col(8)[:, perm].astype(bf16), "wv": colb(9), "wg": colb(10)},
        "gdn": {"wqkv": colb(11), "wz": colb(12), "wab": pad_cols(jnp.concatenate([col(13), col(14)], axis=1)),
                "cw": gdn_conv_w[l], "dtb": _pad_lanes(gdn_dt_bias[l]), "alog": _pad_lanes(gdn_a_log[l]),
                "nrm": row(gdn_norm[l])},
        "w_gate": colb(15), "w_branch": w_branch[l].astype(bf16), "w_out": w_out[l].astype(bf16),
    }


def _rope_tables(pos):
    half = RET_DK // 2
    inv_freq = 1.0 / (ROPE_BASE ** jnp.linspace(0.0, 1.0, half, dtype=f32))
    ang = pos.astype(f32)[:, None] * inv_freq[None, :]
    reps = LANES // half
    return jnp.tile(jnp.cos(ang), (1, reps)), jnp.tile(jnp.sin(ang), (1, reps))


def _ssm_to_pairs(s):
    b = s.shape[0]
    return s.reshape(b, MB_HEADS // 2, 2, MB_STATE, MB_HEADDIM).transpose(0, 1, 3, 2, 4).reshape(
        b, MB_HEADS // 2, MB_STATE, 2 * MB_HEADDIM)


def _ssm_from_pairs(s):
    b = s.shape[0]
    return s.reshape(b, MB_HEADS // 2, MB_STATE, 2, MB_HEADDIM).transpose(0, 1, 3, 2, 4).reshape(
        b, MB_HEADS, MB_STATE, MB_HEADDIM)


def _pad_conv(buf):
    return jnp.pad(buf, ((0, 0), (CONV_PAD - (CONV_W - 1), 0), (0, 0)))


def _layer(h, w, states, rope, lg, chunks, bt):
    b, l, d = h.shape
    s_ssm, s_ssm_conv, s_hg, s_ret, s_gdn, s_gdn_conv = states
    h = _ffn(h.reshape(b * l, d), w["g_ffn1"], w["ffn_in"], w["ffn_out"]).reshape(b, l, d)
    y_mb, n_ssm, n_ssm_conv = _mamba(h, w["g_u"], w["mamba"], _ssm_to_pairs(s_ssm), _pad_conv(s_ssm_conv),
                                     chunks["mamba"], bt)
    y_hg, n_hg = _hgrn(h, w["g_u"], w["hgrn"], s_hg, chunks["hgrn"], bt)
    y_rt, n_ret = _ret(h, w["g_u"], w["ret"], rope[0], rope[1], lg, s_ret, chunks["ret"], bt)
    y_gd, n_gdn, n_gdn_conv = _gdn(h, w["g_u"], w["gdn"], s_gdn, _pad_conv(s_gdn_conv), chunks["gdn"], bt)
    ys = [y.reshape(b * l, BRANCH_W) for y in (y_mb, y_hg, y_rt, y_gd)]
    hf = _merge(h.reshape(b * l, d), w["g_mix"], ys, w["w_gate"], w["w_branch"], w["w_out"])
    hf = _ffn(hf, w["g_ffn2"], w["ffn_in2"], w["ffn_out2"])
    new_states = (_ssm_from_pairs(n_ssm), n_ssm_conv[:, CONV_PAD - (CONV_W - 1):], n_hg, n_ret, n_gdn,
                  n_gdn_conv[:, CONV_PAD - (CONV_W - 1):])
    return hf.reshape(b, l, d), new_states


def _chunks(l):
    pick = lambda pref: pref if l % pref == 0 else math.gcd(l, pref)
    return {"mamba": pick(128), "ret": pick(128), "hgrn": pick(64), "gdn": pick(64)}


def kernel(x_prompt, x_sample, state_ssm, state_ssm_conv, state_hgrn, state_ret, state_gdn, state_gdn_conv,
           norm_g, ffn_w_in, ffn_w_out, w_in, w_branch, w_out, mb_conv_w, mb_conv_b, mb_a_log, mb_dt_bias, mb_d,
           mb_norm, hg_lb_logits, hg_norm, gdn_conv_w, gdn_a_log, gdn_dt_bias, gdn_norm):
    depth = norm_g.shape[0]
    lb_all = jnp.cumsum(jax.nn.softmax(hg_lb_logits.astype(f32), axis=0), axis=0)
    lb_all = lb_all - lb_all[0:1]
    lg = _pad_lanes(jnp.log(1.0 - jnp.exp2(-5.0 - jnp.arange(RET_HEADS, dtype=f32))))

    bp, lp, _ = x_prompt.shape
    bs, ls, _ = x_sample.shape
    rope_p = _rope_tables(jnp.arange(lp))
    rope_s = _rope_tables(PAST_LEN + jnp.arange(ls))
    zero_states = (
        jnp.zeros((bp, MB_HEADS, MB_STATE, MB_HEADDIM), f32),
        jnp.zeros((bp, CONV_W - 1, MB_CONV_DIM), f32),
        jnp.zeros((bp, HG_HEADS, HG_DK, HG_DV), f32),
        jnp.zeros((bp, RET_HEADS, RET_DK, RET_DV), f32),
        jnp.zeros((bp, GDN_HEADS, GDN_DK, GDN_DV), f32),
        jnp.zeros((bp, CONV_W - 1, GDN_CONV_DIM), f32),
    )
    bt_p = math.gcd(bp, 8)
    bt_s = math.gcd(bs, 16)

    hp, hs = x_prompt, x_sample
    st_p, st_s = [], []
    for l in range(depth):
        w = _layer_weights(l, norm_g, ffn_w_in[:, 0], ffn_w_out[:, 0], w_in, w_branch, w_out, mb_conv_w, mb_conv_b,
                           mb_a_log, mb_dt_bias, mb_d, mb_norm, lb_all, hg_norm, gdn_conv_w, gdn_a_log,
                           gdn_dt_bias, gdn_norm)
        w["ffn_in2"] = ffn_w_in[l, 1].astype(bf16)
        w["ffn_out2"] = ffn_w_out[l, 1].astype(bf16)
        hp, sp = _layer(hp, w, zero_states, rope_p, lg, _chunks(lp), bt_p)
        past = (state_ssm[l], state_ssm_conv[l], state_hgrn[l], state_ret[l], state_gdn[l], state_gdn_conv[l])
        hs, ss = _layer(hs, w, past, rope_s, lg, _chunks(ls), bt_s)
        st_p.append(sp)
        st_s.append(ss)

    stack = lambda sts, i: jnp.stack([s[i] for s in sts], axis=0)
    return (hp, hs,
            stack(st_p, 0), stack(st_p, 1), stack(st_p, 2), stack(st_p, 3), stack(st_p, 4), stack(st_p, 5),
            stack(st_s, 0), stack(st_s, 1), stack(st_s, 2), stack(st_s, 3), stack(st_s, 4), stack(st_s, 5))
```

```python
import functools
import math

import numpy as np
import jax
import jax.numpy as jnp
from jax import lax
from jax.experimental import pallas as pl
from jax.experimental.pallas import tpu as pltpu

f32 = jnp.float32
bf16 = jnp.bfloat16

D_MODEL = 1024
D_FF = 2816
BRANCH_W = 512
N_BRANCH = 4
CONV_W = 4
EPS = 1e-6
ROPE_BASE = 10000.0
PAST_LEN = 16384

MB_HEADS, MB_HEADDIM, MB_STATE, MB_GROUPS = 8, 64, 64, 2
MB_CONV_DIM = BRANCH_W + 2 * MB_GROUPS * MB_STATE
HG_HEADS, HG_DK, HG_DV = 4, 128, 128
RET_HEADS, RET_DK, RET_DV = 4, 64, 128
GDN_HEADS, GDN_DK, GDN_DV = 4, 128, 128
GDN_CONV_DIM = 2 * GDN_HEADS * GDN_DK + GDN_HEADS * GDN_DV

IN_SPLITS = (
    BRANCH_W, MB_CONV_DIM, MB_HEADS,
    HG_HEADS * HG_DK, HG_HEADS * HG_DK, BRANCH_W, BRANCH_W,
    RET_HEADS * RET_DK, RET_HEADS * RET_DK, BRANCH_W, BRANCH_W,
    GDN_CONV_DIM, BRANCH_W, GDN_HEADS, GDN_HEADS,
    N_BRANCH * D_MODEL,
)
IN_OFFSETS = tuple(int(v) for v in np.cumsum((0,) + IN_SPLITS))

LANES = 128
SUBLANES = 8
MXU_DIM = 256
CONV_PAD = SUBLANES
VMEM_LIMIT = 56 * 1024 * 1024
FFN_TM = 512
FFN_TF = D_FF // 2
MERGE_TM = 512
HG_SUB = SUBLANES


def _dot(a, b):
    return jnp.dot(a.astype(bf16), b.astype(bf16), preferred_element_type=f32)


def _dot_nt(a, b):
    return lax.dot_general(a.astype(bf16), b.astype(bf16), (((1,), (1,)), ((), ())), preferred_element_type=f32)


def _dot_tn(a, b):
    return lax.dot_general(a.astype(bf16), b.astype(bf16), (((0,), (0,)), ((), ())), preferred_element_type=f32)


def _split3(x):
    hi = x.astype(bf16)
    r = x - hi.astype(f32)
    mid = r.astype(bf16)
    lo = (r - mid.astype(f32)).astype(bf16)
    return hi, mid, lo


def _dot_sel_left(m01, x):
    hi, mid, lo = _split3(x)
    return (jnp.dot(m01, hi, preferred_element_type=f32) + jnp.dot(m01, mid, preferred_element_type=f32)
            + jnp.dot(m01, lo, preferred_element_type=f32))


def _rms(x):
    return x * lax.rsqrt(jnp.mean(x * x, axis=-1, keepdims=True) + EPS)


def _silu(x):
    return x * jax.nn.sigmoid(x)


def _iota2(n, m, axis):
    return lax.broadcasted_iota(jnp.int32, (n, m), axis)


def _log2(n):
    assert n > 0 and n & (n - 1) == 0, n
    return n.bit_length() - 1


def _ltri(c):
    return jnp.where(_iota2(c, c, 0) >= _iota2(c, c, 1), 1.0, 0.0).astype(bf16)


def _blockdiag_ltri(n, c):
    ii, jj = _iota2(n, n, 0), _iota2(n, n, 1)
    lc = _log2(c)
    return jnp.where(((ii >> lc) == (jj >> lc)) & (ii >= jj), 1.0, 0.0).astype(bf16)


def _row_bcast(col, c):
    dg = jnp.where(_iota2(c, c, 0) == _iota2(c, c, 1), jnp.broadcast_to(col, (c, c)), 0.0)
    return _dot_sel_left(jnp.ones((c, c), bf16), dg)


def _decay_matrix(col, c):
    diff = col - _row_bcast(col, c)
    return jnp.exp(jnp.where(_iota2(c, c, 0) >= _iota2(c, c, 1), diff, -jnp.inf))


def _causal_conv(xp_ref, w_ref, c):
    y = xp_ref[:, pl.ds(CONV_PAD - 3, c), :] * w_ref[0:1, :]
    for j in range(1, CONV_W):
        y = y + xp_ref[:, pl.ds(CONV_PAD - 3 + j, c), :] * w_ref[j:j + 1, :]
    return y


def _stack_geometry(bt, c, heads):
    gs = min(bt, max(1, MXU_DIM // (heads * c)))
    assert bt % gs == 0 and (gs * heads * c) % LANES == 0
    return gs, bt // gs, gs * heads * c


def _widen(x, r):
    return x if r == LANES else jnp.concatenate([x] * (r // LANES), axis=-1)


def _ffn_body(x_ref, g_ref, wg_ref, wu_ref, wo_ref, o_ref, xn_ref, acc_ref):
    j = pl.program_id(1)

    @pl.when(j == 0)
    def _():
        xn_ref[...] = (_rms(x_ref[...]) * g_ref[0:1, :]).astype(bf16)
        acc_ref[...] = jnp.zeros_like(acc_ref)

    xn = xn_ref[...]
    gate = jnp.dot(xn, wg_ref[...], preferred_element_type=f32)
    up = jnp.dot(xn, wu_ref[...], preferred_element_type=f32)
    act = (_silu(gate) * up).astype(bf16)
    acc_ref[...] += jnp.dot(act, wo_ref[...], preferred_element_type=f32)

    @pl.when(j == pl.num_programs(1) - 1)
    def _():
        o_ref[...] = x_ref[...] + 0.5 * (_rms(acc_ref[...]) * g_ref[1:2, :])


def _ffn(x, g2, w_in, w_out):
    t = x.shape[0]
    tm = min(FFN_TM, t)
    nf = D_FF // FFN_TF
    return pl.pallas_call(
        _ffn_body,
        grid=(t // tm, nf),
        in_specs=[
            pl.BlockSpec((tm, D_MODEL), lambda i, j: (i, 0)),
            pl.BlockSpec((2, D_MODEL), lambda i, j: (0, 0)),
            pl.BlockSpec((D_MODEL, FFN_TF), lambda i, j: (0, j)),
            pl.BlockSpec((D_MODEL, FFN_TF), lambda i, j: (0, j + nf)),
            pl.BlockSpec((FFN_TF, D_MODEL), lambda i, j: (j, 0)),
        ],
        out_specs=pl.BlockSpec((tm, D_MODEL), lambda i, j: (i, 0)),
        out_shape=jax.ShapeDtypeStruct((t, D_MODEL), f32),
        scratch_shapes=[pltpu.VMEM((tm, D_MODEL), bf16), pltpu.VMEM((tm, D_MODEL), f32)],
        compiler_params=pltpu.CompilerParams(dimension_semantics=("arbitrary", "arbitrary"),
                                             vmem_limit_bytes=VMEM_LIMIT),
        name="ffn",
    )(x, g2, w_in, w_in, w_out)


def _merge_body(h_ref, g_ref, y0_ref, y1_ref, y2_ref, y3_ref, wg_ref, wb_ref, wo_ref, o_ref):
    h = h_ref[...]
    u = (_rms(h) * g_ref[0:1, :]).astype(bf16)
    merged = jnp.zeros(h.shape, f32)
    for n, y_ref in enumerate((y0_ref, y1_ref, y2_ref, y3_ref)):
        gate = jax.nn.sigmoid(jnp.dot(u, wg_ref[:, n * D_MODEL:(n + 1) * D_MODEL], preferred_element_type=f32))
        merged = merged + gate * jnp.dot(y_ref[...], wb_ref[n], preferred_element_type=f32)
    mixed = jnp.dot(merged.astype(bf16), wo_ref[...], preferred_element_type=f32)
    o_ref[...] = h + _rms(mixed) * g_ref[1:2, :]


def _merge(h, g2, ys, w_gate, w_branch, w_out):
    t = h.shape[0]
    tm = min(MERGE_TM, t)
    row = lambda i: (i, 0)
    return pl.pallas_call(
        _merge_body,
        grid=(t // tm,),
        in_specs=[
            pl.BlockSpec((tm, D_MODEL), row),
            pl.BlockSpec((2, D_MODEL), lambda i: (0, 0)),
            pl.BlockSpec((tm, BRANCH_W), row), pl.BlockSpec((tm, BRANCH_W), row),
            pl.BlockSpec((tm, BRANCH_W), row), pl.BlockSpec((tm, BRANCH_W), row),
            pl.BlockSpec((D_MODEL, N_BRANCH * D_MODEL), lambda i: (0, 0)),
            pl.BlockSpec((N_BRANCH, BRANCH_W, D_MODEL), lambda i: (0, 0, 0)),
            pl.BlockSpec((D_MODEL, D_MODEL), lambda i: (0, 0)),
        ],
        out_specs=pl.BlockSpec((tm, D_MODEL), row),
        out_shape=jax.ShapeDtypeStruct((t, D_MODEL), f32),
        compiler_params=pltpu.CompilerParams(dimension_semantics=("arbitrary",), vmem_limit_bytes=VMEM_LIMIT),
        name="merge",
    )(h, g2, *ys, w_gate, w_branch, w_out)


def _normed_input(h_ref, g_ref):
    bt, c, d = h_ref.shape
    return (_rms(h_ref[...].reshape(bt * c, d)) * g_ref[...]).astype(bf16)


def _const_spec(a):
    nd = a.ndim
    return pl.BlockSpec(a.shape, lambda i, j, _nd=nd: (0,) * _nd)


def _batch_spec(bt, a):
    nd = a.ndim
    return pl.BlockSpec((bt,) + a.shape[1:], lambda i, j, _nd=nd: (i,) + (0,) * (_nd - 1))


def _mixer_call(body, name, h, consts, batch_ins, c, bt, batch_outs, scratch):
    b, l, d = h.shape
    seq = lambda i, j: (i, j, 0)
    out_shape = [jax.ShapeDtypeStruct((b, l, BRANCH_W), bf16)] + [jax.ShapeDtypeStruct(s, f32) for s in batch_outs]
    out_specs = [pl.BlockSpec((bt, c, BRANCH_W), seq)] + [
        pl.BlockSpec((bt,) + s[1:], lambda i, j, _nd=len(s): (i,) + (0,) * (_nd - 1)) for s in batch_outs]
    return pl.pallas_call(
        body,
        grid=(b // bt, l // c),
        in_specs=[pl.BlockSpec((bt, c, d), seq)] + [_const_spec(a) for a in consts]
                 + [_batch_spec(bt, a) for a in batch_ins],
        out_specs=out_specs,
        out_shape=out_shape,
        scratch_shapes=scratch,
        compiler_params=pltpu.CompilerParams(dimension_semantics=("arbitrary", "arbitrary"),
                                             vmem_limit_bytes=VMEM_LIMIT),
        name=name,
    )(h, *consts, *batch_ins)


def _mamba_body(h_ref, g_ref, wz_ref, wx_ref, wdt_ref, cw_ref, cb_ref, dtb_ref, alog_ref, d_ref, nrm_ref,
                s0_ref, c0_ref, y_ref, so_ref, co_ref, s_scr, xp_scr, xa_scr, z_scr, dt_scr, la_scr):
    bt, c, _ = h_ref.shape
    j = pl.program_id(1)
    npair = MB_HEADS // 2

    @pl.when(j == 0)
    def _():
        s_scr[...] = jnp.zeros_like(s_scr)
        for p in range(npair):
            g = (2 * p) // (MB_HEADS // MB_GROUPS)
            s_scr[:, p, g * MB_STATE:(g + 1) * MB_STATE, :] = s0_ref[:, p]
        xp_scr[:, 0:CONV_PAD, :] = c0_ref[...]

    u = _normed_input(h_ref, g_ref)
    z_scr[...] = jnp.dot(u, wz_ref[...], preferred_element_type=f32).reshape(bt, c, BRANCH_W)
    xp_scr[:, CONV_PAD:CONV_PAD + c, :] = jnp.dot(u, wx_ref[...], preferred_element_type=f32).reshape(
        bt, c, MB_CONV_DIM)
    dt = jax.nn.softplus(jnp.dot(u, wdt_ref[...], preferred_element_type=f32) + dtb_ref[...])
    dt_scr[...] = dt.reshape(bt, c, LANES)
    la_scr[...] = (-jnp.exp(alog_ref[...]) * dt).reshape(bt, c, LANES)
    xa_scr[...] = _silu(_causal_conv(xp_scr, cw_ref, c) + cb_ref[...])
    tail = xp_scr[:, c:c + CONV_PAD, :]
    xp_scr[:, 0:CONV_PAD, :] = tail
    co_ref[...] = tail

    lane = _iota2(c, LANES, 1)
    lo_half = lane < MB_HEADDIM
    ltri = _ltri(c)

    def per_seq(b, carry):
        xa = xa_scr[b]
        xs = xa[:, :BRANCH_W]
        bm = xa[:, BRANCH_W:BRANCH_W + LANES]
        cm = xa[:, BRANCH_W + LANES:BRANCH_W + 2 * LANES]
        dt_b = dt_scr[b]
        cum = _dot_sel_left(ltri, la_scr[b])
        clast = cum[c - 1:c, :]
        cm_g = [jnp.where((lane >> _log2(MB_STATE)) == g, cm, 0.0) for g in range(MB_GROUPS)]
        gmat = [_dot_nt(cm_g[g], bm) for g in range(MB_GROUPS)]
        ys = []
        for p in range(npair):
            h0, h1 = 2 * p, 2 * p + 1
            g = h0 // (MB_HEADS // MB_GROUPS)
            xblk = xs[:, p * LANES:(p + 1) * LANES]
            pair = lambda a: jnp.where(lo_half, a[:, h0:h0 + 1], a[:, h1:h1 + 1])
            cum_p = pair(cum)
            clast_p = jnp.where(lo_half[0:1], clast[:, h0:h0 + 1], clast[:, h1:h1 + 1])
            v = xblk * pair(dt_b)
            sc0 = gmat[g] * _decay_matrix(cum[:, h0:h0 + 1], c)
            sc1 = gmat[g] * _decay_matrix(cum[:, h1:h1 + 1], c)
            s_old = s_scr[b, p]
            o = (_dot(sc0, jnp.where(lo_half, v, 0.0)) + _dot(sc1, jnp.where(lo_half, 0.0, v))
                 + _dot(cm_g[g], s_old) * jnp.exp(cum_p))
            s_scr[b, p] = s_old * jnp.exp(clast_p) + _dot_tn(bm, v * jnp.exp(clast_p - cum_p))
            ys.append(o + d_ref[:, p * LANES:(p + 1) * LANES] * xblk)
        y = jnp.concatenate(ys, axis=-1) * _silu(z_scr[b])
        y_ref[b] = (_rms(y) * nrm_ref[...]).astype(bf16)
        return carry

    lax.fori_loop(0, bt, per_seq, 0)

    @pl.when(j == pl.num_programs(1) - 1)
    def _():
        for p in range(npair):
            g = (2 * p) // (MB_HEADS // MB_GROUPS)
            so_ref[:, p] = s_scr[:, p, g * MB_STATE:(g + 1) * MB_STATE, :]


def _mamba(h, g2, w, s0_pairs, conv0, c, bt):
    b, l, _ = h.shape
    npair = MB_HEADS // 2
    consts = [g2, w["wz"], w["wx"], w["wdt"], w["cw"], w["cb"], w["dtb"], w["alog"], w["d"], w["nrm"]]
    scratch = [
        pltpu.VMEM((bt, npair, LANES, LANES), f32),
        pltpu.VMEM((bt, CONV_PAD + c, MB_CONV_DIM), f32),
        pltpu.VMEM((bt, c, MB_CONV_DIM), f32),
        pltpu.VMEM((bt, c, BRANCH_W), f32),
        pltpu.VMEM((bt, c, LANES), f32),
        pltpu.VMEM((bt, c, LANES), f32),
    ]
    return _mixer_call(_mamba_body, "mamba", h, consts, [s0_pairs, conv0], c, bt,
                       [(b, npair, MB_STATE, LANES), (b, CONV_PAD, MB_CONV_DIM)], scratch)


def _ret_body(h_ref, g_ref, wq_ref, wk_ref, wv_ref, wg_ref, cos_ref, sin_ref, lg_ref,
              s0_ref, y_ref, so_ref, s_scr, q_scr, k_scr, v_scr, gt_scr):
    bt, c, _ = h_ref.shape
    j = pl.program_id(1)
    kw = RET_HEADS * RET_DK
    hw = RET_DK // 2

    @pl.when(j == 0)
    def _():
        s_scr[...] = jnp.zeros_like(s_scr)
        for h in range(RET_HEADS):
            s_scr[:, h, h * hw:(h + 1) * hw, :] = s0_ref[:, h, 0:hw, :]
            s_scr[:, h, LANES + h * hw:LANES + (h + 1) * hw, :] = s0_ref[:, h, hw:2 * hw, :]

    u = _normed_input(h_ref, g_ref)
    cos = cos_ref[...]
    sin = sin_ref[...]

    def rot(x):
        x = x.reshape(bt, c, kw)
        x1, x2 = x[:, :, :LANES], x[:, :, LANES:]
        return jnp.concatenate([x1 * cos - x2 * sin, x1 * sin + x2 * cos], axis=-1)

    q_scr[...] = rot(jnp.dot(u, wq_ref[...], preferred_element_type=f32))
    k_scr[...] = rot(jnp.dot(u, wk_ref[...], preferred_element_type=f32)) * (RET_DK ** -0.5)
    v_scr[...] = jnp.dot(u, wv_ref[...], preferred_element_type=f32).reshape(bt, c, BRANCH_W)
    gt_scr[...] = jnp.dot(u, wg_ref[...], preferred_element_type=f32).reshape(bt, c, BRANCH_W)

    ii = _iota2(c, c, 0)
    jj = _iota2(c, c, 1)
    rowi = _iota2(c, 1, 0)
    klane = _iota2(c, kw, 1)
    dec, e_in, e_out, e_all, hmask = [], [], [], [], []
    for h in range(RET_HEADS):
        lg = lg_ref[:, h:h + 1]
        dec.append(jnp.exp(jnp.where(ii >= jj, (ii - jj).astype(f32) * lg, -jnp.inf)))
        e_in.append(jnp.exp((rowi + 1).astype(f32) * lg))
        e_out.append(jnp.exp((c - 1 - rowi).astype(f32) * lg))
        e_all.append(jnp.exp(float(c) * lg))
        hmask.append(((klane & (LANES - 1)) >> _log2(hw)) == h)

    def per_seq(b, carry):
        q = q_scr[b]
        k = k_scr[b]
        v = v_scr[b]
        ys = []
        for h in range(RET_HEADS):
            vh = v[:, h * RET_DV:(h + 1) * RET_DV]
            qm = jnp.where(hmask[h], q, 0.0)
            km = jnp.where(hmask[h], k, 0.0)
            s_old = s_scr[b, h]
            o = _dot(_dot_nt(qm, k) * dec[h], vh) + _dot(qm * e_in[h], s_old)
            s_scr[b, h] = s_old * e_all[h] + _dot_tn(km * e_out[h], vh)
            ys.append(_rms(o))
        y_ref[b] = (jnp.concatenate(ys, axis=-1) * _silu(gt_scr[b])).astype(bf16)
        return carry

    lax.fori_loop(0, bt, per_seq, 0)

    @pl.when(j == pl.num_programs(1) - 1)
    def _():
        for h in range(RET_HEADS):
            so_ref[:, h, 0:hw, :] = s_scr[:, h, h * hw:(h + 1) * hw, :]
            so_ref[:, h, hw:2 * hw, :] = s_scr[:, h, LANES + h * hw:LANES + (h + 1) * hw, :]


def _ret(h, g2, w, cos, sin, lg, s0, c, bt):
    b, l, _ = h.shape
    kw = RET_HEADS * RET_DK
    consts = [g2, w["wq"], w["wk"], w["wv"], w["wg"]]
    scratch = [
        pltpu.VMEM((bt, RET_HEADS, kw, RET_DV), f32),
        pltpu.VMEM((bt, c, kw), f32),
        pltpu.VMEM((bt, c, kw), f32),
        pltpu.VMEM((bt, c, BRANCH_W), f32),
        pltpu.VMEM((bt, c, BRANCH_W), f32),
    ]
    seq = lambda i, j: (i, j, 0)
    out_shape = [jax.ShapeDtypeStruct((b, l, BRANCH_W), bf16), jax.ShapeDtypeStruct(s0.shape, f32)]
    st_spec = pl.BlockSpec((bt,) + s0.shape[1:], lambda i, j: (i, 0, 0, 0))
    return pl.pallas_call(
        _ret_body,
        grid=(b // bt, l // c),
        in_specs=[pl.BlockSpec((bt, c, D_MODEL), seq)] + [_const_spec(a) for a in consts]
                 + [pl.BlockSpec((c, LANES), lambda i, j: (j, 0)), pl.BlockSpec((c, LANES), lambda i, j: (j, 0)),
                    _const_spec(lg), st_spec],
        out_specs=[pl.BlockSpec((bt, c, BRANCH_W), seq), st_spec],
        out_shape=out_shape,
        scratch_shapes=scratch,
        compiler_params=pltpu.CompilerParams(dimension_semantics=("arbitrary", "arbitrary"),
                                             vmem_limit_bytes=VMEM_LIMIT),
        name="ret",
    )(h, *consts, cos, sin, lg, s0)


def _hgrn_body(h_ref, g_ref, wq_ref, wf_ref, wi_ref, wg_ref, lb_ref, nrm_ref,
               s0_ref, y_ref, so_ref, st_scr, q_scr, k_scr, v_scr, cum_scr, gt_scr):
    bt, c, _ = h_ref.shape
    j = pl.program_id(1)
    nh = HG_HEADS
    gs, ng, r = _stack_geometry(bt, c, nh)
    n = bt * nh * c
    sub = HG_SUB

    @pl.when(j == 0)
    def _():
        def init(b, carry):
            for h in range(nh):
                st_scr[b, h] = s0_ref[b, h].T
            return carry
        lax.fori_loop(0, bt, init, 0)

    u = _normed_input(h_ref, g_ref)
    q = _silu(jnp.dot(u, wq_ref[...], preferred_element_type=f32))
    lb = lb_ref[...]
    fg = lb + (1.0 - lb) * jax.nn.sigmoid(jnp.dot(u, wf_ref[...], preferred_element_type=f32))
    v = jnp.dot(u, wi_ref[...], preferred_element_type=f32)
    gt_scr[...] = jnp.dot(u, wg_ref[...], preferred_element_type=f32).reshape(bt, c, BRANCH_W)
    cum = _dot_sel_left(_blockdiag_ltri(bt * c, c), jnp.log(fg))
    k = 1.0 - fg
    for h in range(nh):
        sl = slice(h * HG_DK, (h + 1) * HG_DK)
        q_scr[:, h] = q[:, sl].reshape(bt, c, HG_DK)
        k_scr[:, h] = k[:, sl].reshape(bt, c, HG_DK)
        v_scr[:, h] = v[:, sl].reshape(bt, c, HG_DV)
        cum_scr[:, h] = cum[:, sl].reshape(bt, c, HG_DK)
    qf = q_scr[...].reshape(n, HG_DK)
    kf = k_scr[...].reshape(n, HG_DK)
    vf = v_scr[...].reshape(n, HG_DV)
    cf = cum_scr[...].reshape(n, HG_DK)

    ii, jj = _iota2(r, r, 0), _iota2(r, r, 1)
    scores = [jnp.zeros((r, r), f32) for _ in range(ng)]

    q3, k3, c3 = (a.reshape(n // sub, sub, HG_DK) for a in (qf, kf, cf))
    sub_i = lax.broadcasted_iota(jnp.int32, q3.shape, 1)
    ones = jnp.ones((HG_DK, r), bf16)
    same_sub = (ii >> _log2(sub)) == (jj >> _log2(sub))
    for t in range(sub):
        pt = jnp.where(sub_i >= t, q3 * k3[:, t:t + 1, :] * jnp.exp(c3 - c3[:, t:t + 1, :]), 0.0)
        row_sum = jnp.dot(pt.reshape(n, HG_DK).astype(bf16), ones, preferred_element_type=f32)
        place = same_sub & ((jj & (sub - 1)) == t)
        scores = [jnp.where(place, row_sum[g * r:(g + 1) * r], scores[g]) for g in range(ng)]

    m = sub
    while m < c:
        q4, k4, c4 = (a.reshape(n // (2 * m), 2, m, HG_DK) for a in (qf, kf, cf))
        bnd = c4[:, 0:1, m - 1:m, :]
        odd = lax.broadcasted_iota(jnp.int32, q4.shape, 1) == 1
        e = jnp.exp(jnp.where(odd, c4 - bnd, bnd - c4))
        q_t = jnp.where(odd, q4 * e, 0.0).reshape(n, HG_DK)
        k_t = jnp.where(odd, 0.0, k4 * e).reshape(n, HG_DK)
        same_pair = (ii >> _log2(2 * m)) == (jj >> _log2(2 * m))
        scores = [scores[g] + jnp.where(same_pair, _dot_nt(q_t[g * r:(g + 1) * r], k_t[g * r:(g + 1) * r]), 0.0)
                  for g in range(ng)]
        m *= 2

    cb3 = cf.reshape(n // c, c, HG_DK)
    clast3 = cb3[:, c - 1:c, :]
    q_in = qf * jnp.exp(cf)
    k_out = (kf.reshape(n // c, c, HG_DK) * jnp.exp(clast3 - cb3)).reshape(n, HG_DK)
    e_all = jnp.exp(clast3)
    blocks = [(b, h) for b in range(bt) for h in range(nh)]
    rows = lambda a, i: a[i * c:(i + 1) * c]
    st_old = [st_scr[b, h] for b, h in blocks]
    o_state = [_dot_nt(rows(q_in, i), st_old[i]) for i in range(len(blocks))]
    for i, (b, h) in enumerate(blocks):
        st_scr[b, h] = st_old[i] * e_all[i] + _dot_tn(rows(vf, i), rows(k_out, i))
    for g in range(ng):
        o = _dot(scores[g], vf[g * r:(g + 1) * r])
        for s in range(gs):
            b = g * gs + s
            y = jnp.concatenate([_rms(rows(o, s * nh + h) + o_state[b * nh + h]) * nrm_ref[...] for h in range(nh)],
                                axis=-1)
            y_ref[b] = (y * _silu(gt_scr[b])).astype(bf16)

    @pl.when(j == pl.num_programs(1) - 1)
    def _():
        def fin(b, carry):
            for h in range(nh):
                so_ref[b, h] = st_scr[b, h].T
            return carry
        lax.fori_loop(0, bt, fin, 0)


def _hgrn(h, g2, w, s0, c, bt):
    consts = [g2, w["wq"], w["wf"], w["wi"], w["wg"], w["lb"], w["nrm"]]
    scratch = ([pltpu.VMEM((bt, HG_HEADS, HG_DV, HG_DK), f32)] + [pltpu.VMEM((bt, HG_HEADS, c, HG_DK), f32)] * 4
               + [pltpu.VMEM((bt, c, BRANCH_W), f32)])
    return _mixer_call(_hgrn_body, "hgrn", h, consts, [s0], c, bt, [s0.shape], scratch)


def _unit_lower_inverses(a_list, r, c):
    ii, jj = _iota2(r, r, 0), _iota2(r, r, 1)
    eye = jnp.where(ii == jj, 1.0, 0.0)
    t_list = [eye for _ in a_list]
    m = 1
    while m < c:
        lm = _log2(m)
        join = ((ii >> (lm + 1)) == (jj >> (lm + 1))) & (((ii >> lm) & 1) == 1) & (((jj >> lm) & 1) == 0)
        f_list = [jnp.where(join, a, 0.0) for a in a_list]
        if m == 1:
            t_list = [eye - f for f in f_list]
        else:
            ft_list = [_dot(f, t) for f, t in zip(f_list, t_list)]
            t_list = [t - _dot(t, ft) for t, ft in zip(t_list, ft_list)]
        m *= 2
    return t_list


def _gdn_body(h_ref, g_ref, wqkv_ref, wz_ref, wab_ref, cw_ref, dtb_ref, alog_ref, nrm_ref,
              s0_ref, c0_ref, y_ref, so_ref, co_ref, s_scr, xp_scr, z_scr, q_scr, k_scr, v_scr, cum_scr, beta_scr):
    bt, c, _ = h_ref.shape
    j = pl.program_id(1)
    nh = GDN_HEADS
    hk = nh * GDN_DK
    gs, ng, r = _stack_geometry(bt, c, nh)

    @pl.when(j == 0)
    def _():
        s_scr[...] = s0_ref[...]
        xp_scr[:, 0:CONV_PAD, :] = c0_ref[...]

    u = _normed_input(h_ref, g_ref)
    xp_scr[:, CONV_PAD:CONV_PAD + c, :] = jnp.dot(u, wqkv_ref[...], preferred_element_type=f32).reshape(
        bt, c, GDN_CONV_DIM)
    z_scr[...] = jnp.dot(u, wz_ref[...], preferred_element_type=f32).reshape(bt, c, BRANCH_W)
    ab = jnp.dot(u, wab_ref[...], preferred_element_type=f32)
    log_g = -jnp.exp(alog_ref[...]) * jax.nn.softplus(ab + dtb_ref[...])
    cum = _dot_sel_left(_blockdiag_ltri(bt * c, c), log_g)
    beta = jax.nn.sigmoid(ab)
    qkv = _silu(_causal_conv(xp_scr, cw_ref, c))
    tail = xp_scr[:, c:c + CONV_PAD, :]
    xp_scr[:, 0:CONV_PAD, :] = tail
    co_ref[...] = tail
    lane_bcast = lambda col: jnp.broadcast_to(col, (bt * c, LANES)).reshape(bt, c, LANES)
    for h in range(nh):
        q = qkv[:, :, h * GDN_DK:(h + 1) * GDN_DK]
        k = qkv[:, :, hk + h * GDN_DK:hk + (h + 1) * GDN_DK]
        q_scr[:, h] = q * lax.rsqrt(jnp.sum(q * q, axis=-1, keepdims=True) + EPS) * (GDN_DK ** -0.5)
        k_scr[:, h] = k * lax.rsqrt(jnp.sum(k * k, axis=-1, keepdims=True) + EPS)
        v_scr[:, h] = qkv[:, :, 2 * hk + h * GDN_DV:2 * hk + (h + 1) * GDN_DV]
        cum_scr[:, h] = lane_bcast(cum[:, h:h + 1])
        beta_scr[:, h] = lane_bcast(beta[:, nh + h:nh + h + 1])

    grp = lambda ref, g: ref[g * gs:(g + 1) * gs].reshape(r, LANES)
    groups = range(ng)
    qs = [grp(q_scr, g) for g in groups]
    ks = [grp(k_scr, g) for g in groups]
    vs = [grp(v_scr, g) for g in groups]
    cb = [grp(cum_scr, g) for g in groups]
    bb = [grp(beta_scr, g) for g in groups]
    ii, jj = _iota2(r, r, 0), _iota2(r, r, 1)
    same = (ii >> _log2(c)) == (jj >> _log2(c))
    incl = same & (ii >= jj)
    strict = same & (ii > jj)
    dec = [jnp.exp(jnp.where(incl, _widen(cb[g], r) - cb[g].T[0:1, :], -jnp.inf)) for g in groups]
    a_mat = [jnp.where(strict, _widen(bb[g], r) * dec[g] * _dot_nt(ks[g], ks[g]), 0.0) for g in groups]
    t_inv = _unit_lower_inverses(a_mat, r, c)
    e_in = [jnp.exp(cb[g]) for g in groups]
    sol = [_dot(t_inv[g], jnp.concatenate([vs[g] * bb[g], ks[g] * (e_in[g] * bb[g])], axis=-1)) for g in groups]
    scd = [_dot_nt(qs[g], ks[g]) * dec[g] for g in groups]
    q_in = [qs[g] * e_in[g] for g in groups]

    rows = lambda a, i: a[i * c:(i + 1) * c]
    for g in groups:
        blocks = [(g * gs + s, h) for s in range(gs) for h in range(nh)]
        nb = len(blocks)
        s_old = [s_scr[b, h] for b, h in blocks]
        proj = [_dot(jnp.concatenate([rows(sol[g], i)[:, GDN_DV:], rows(q_in[g], i)], axis=0), s_old[i])
                for i in range(nb)]
        w_new = [rows(sol[g], i)[:, :GDN_DV] - proj[i][:c] for i in range(nb)]
        for i, (b, h) in enumerate(blocks):
            cum_i = rows(cb[g], i)
            clast = cum_i[c - 1:c]
            s_scr[b, h] = s_old[i] * jnp.exp(clast) + _dot_tn(rows(ks[g], i) * jnp.exp(clast - cum_i), w_new[i])
        o = _dot(scd[g], jnp.concatenate(w_new, axis=0)) + jnp.concatenate([p[c:] for p in proj], axis=0)
        for s in range(gs):
            b = g * gs + s
            y = jnp.concatenate([_rms(rows(o, s * nh + h)) * nrm_ref[...] for h in range(nh)], axis=-1)
            y_ref[b] = (y * _silu(z_scr[b])).astype(bf16)

    @pl.when(j == pl.num_programs(1) - 1)
    def _():
        so_ref[...] = s_scr[...]


def _gdn(h, g2, w, s0, conv0, c, bt):
    b, l, _ = h.shape
    consts = [g2, w["wqkv"], w["wz"], w["wab"], w["cw"], w["dtb"], w["alog"], w["nrm"]]
    scratch = [
        pltpu.VMEM((bt, GDN_HEADS, GDN_DK, GDN_DV), f32),
        pltpu.VMEM((bt, CONV_PAD + c, GDN_CONV_DIM), f32),
        pltpu.VMEM((bt, c, BRANCH_W), f32),
    ] + [pltpu.VMEM((bt, GDN_HEADS, c, LANES), f32)] * 5
    return _mixer_call(_gdn_body, "gdn", h, consts, [s0, conv0], c, bt,
                       [s0.shape, (b, CONV_PAD, GDN_CONV_DIM)], scratch)


def _pad_lanes(v, offset=0):
    row = jnp.zeros((1, LANES), f32)
    return row.at[0, offset:offset + v.shape[0]].set(v.astype(f32))


def _layer_weights(l, norm_g, ffn_w_in, ffn_w_out, w_in, w_branch, w_out, mb_conv_w, mb_conv_b, mb_a_log,
                   mb_dt_bias, mb_d, mb_norm, lb_all, hg_norm, gdn_conv_w, gdn_a_log, gdn_dt_bias, gdn_norm):
    wi = w_in[l]
    col = lambda n: wi[:, IN_OFFSETS[n]:IN_OFFSETS[n + 1]]
    colb = lambda n: col(n).astype(bf16)
    pad_cols = lambda a: jnp.pad(a, ((0, 0), (0, LANES - a.shape[1]))).astype(bf16)
    hw = RET_DK // 2
    perm = np.arange(RET_HEADS * RET_DK).reshape(RET_HEADS, 2, hw).transpose(1, 0, 2).reshape(-1)
    row = lambda v: v.astype(f32).reshape(1, -1)
    return {
        "g_ffn1": norm_g[l, 0:2], "g_mix": norm_g[l, 2:4], "g_ffn2": norm_g[l, 4:6], "g_u": norm_g[l, 2:3],
        "ffn_in": ffn_w_in[l, 0].astype(bf16), "ffn_out": ffn_w_out[l, 0].astype(bf16),
        "ffn_in2": ffn_w_in[l, 1].astype(bf16), "ffn_out2": ffn_w_out[l, 1].astype(bf16),
        "mamba": {"wz": colb(0), "wx": colb(1), "wdt": pad_cols(col(2)), "cw": mb_conv_w[l], "cb": row(mb_conv_b[l]),
                  "dtb": _pad_lanes(mb_dt_bias[l]), "alog": _pad_lanes(mb_a_log[l]),
                  "d": row(jnp.repeat(mb_d[l], MB_HEADDIM)), "nrm": row(mb_norm[l])},
        "hgrn": {"wq": colb(3), "wf": colb(4), "wi": colb(5), "wg": colb(6), "lb": row(lb_all[l]),
                 "nrm": row(hg_norm[l])},
        "ret": {"wq": col(7)[:, perm].astype(bf16), "wk": col(8)[:, perm].astype(bf16), "wv": colb(9), "wg": colb(10)},
        "gdn": {"wqkv": colb(11), "wz": colb(12), "wab": pad_cols(jnp.concatenate([col(13), col(14)], axis=1)),
                "cw": gdn_conv_w[l], "dtb": _pad_lanes(gdn_dt_bias[l]), "alog": _pad_lanes(gdn_a_log[l]),
                "nrm": row(gdn_norm[l])},
        "w_gate": colb(15), "w_branch": w_branch[l].astype(bf16), "w_out": w_out[l].astype(bf16),
    }


def _rope_tables(pos):
    half = RET_DK // 2
    inv_freq = 1.0 / (ROPE_BASE ** jnp.linspace(0.0, 1.0, half, dtype=f32))
    ang = pos.astype(f32)[:, None] * inv_freq[None, :]
    reps = LANES // half
    return jnp.tile(jnp.cos(ang), (1, reps)), jnp.tile(jnp.sin(ang), (1, reps))


def _ssm_to_pairs(s):
    b = s.shape[0]
    return s.reshape(b, MB_HEADS // 2, 2, MB_STATE, MB_HEADDIM).transpose(0, 1, 3, 2, 4).reshape(
        b, MB_HEADS // 2, MB_STATE, 2 * MB_HEADDIM)


def _ssm_from_pairs(s):
    b = s.shape[0]
    return s.reshape(b, MB_HEADS // 2, MB_STATE, 2, MB_HEADDIM).transpose(0, 1, 3, 2, 4).reshape(
        b, MB_HEADS, MB_STATE, MB_HEADDIM)


def _pad_conv(buf):
    return jnp.pad(buf, ((0, 0), (CONV_PAD - (CONV_W - 1), 0), (0, 0)))


def _layer(h, w, states, rope, lg, chunks, bt):
    b, l, d = h.shape
    s_ssm, s_ssm_conv, s_hg, s_ret, s_gdn, s_gdn_conv = states
    h = _ffn(h.reshape(b * l, d), w["g_ffn1"], w["ffn_in"], w["ffn_out"]).reshape(b, l, d)
    y_mb, n_ssm, n_ssm_conv = _mamba(h, w["g_u"], w["mamba"], _ssm_to_pairs(s_ssm), _pad_conv(s_ssm_conv),
                                     chunks["mamba"], bt)
    y_hg, n_hg = _hgrn(h, w["g_u"], w["hgrn"], s_hg, chunks["hgrn"], bt)
    y_rt, n_ret = _ret(h, w["g_u"], w["ret"], rope[0], rope[1], lg, s_ret, chunks["ret"], bt)
    y_gd, n_gdn, n_gdn_conv = _gdn(h, w["g_u"], w["gdn"], s_gdn, _pad_conv(s_gdn_conv), chunks["gdn"], bt)
    ys = [y.reshape(b * l, BRANCH_W) for y in (y_mb, y_hg, y_rt, y_gd)]
    hf = _merge(h.reshape(b * l, d), w["g_mix"], ys, w["w_gate"], w["w_branch"], w["w_out"])
    hf = _ffn(hf, w["g_ffn2"], w["ffn_in2"], w["ffn_out2"])
    new_states = (_ssm_from_pairs(n_ssm), n_ssm_conv[:, CONV_PAD - (CONV_W - 1):], n_hg, n_ret, n_gdn,
                  n_gdn_conv[:, CONV_PAD - (CONV_W - 1):])
    return hf.reshape(b, l, d), new_states


def _chunks(l):
    pick = lambda pref: pref if l % pref == 0 else math.gcd(l, pref)
    return {"mamba": pick(128), "ret": pick(128), "hgrn": pick(64), "gdn": pick(64)}


def kernel(x_prompt, x_sample, state_ssm, state_ssm_conv, state_hgrn, state_ret, state_gdn, state_gdn_conv,
           norm_g, ffn_w_in, ffn_w_out, w_in, w_branch, w_out, mb_conv_w, mb_conv_b, mb_a_log, mb_dt_bias, mb_d,
           mb_norm, hg_lb_logits, hg_norm, gdn_conv_w, gdn_a_log, gdn_dt_bias, gdn_norm):
    depth = norm_g.shape[0]
    lb_all = jnp.cumsum(jax.nn.softmax(hg_lb_logits.astype(f32), axis=0), axis=0)
    lb_all = lb_all - lb_all[0:1]
    lg = _pad_lanes(jnp.log(1.0 - jnp.exp2(-5.0 - jnp.arange(RET_HEADS, dtype=f32))))

    bp, lp, _ = x_prompt.shape
    bs, ls, _ = x_sample.shape
    rope_p = _rope_tables(jnp.arange(lp))
    rope_s = _rope_tables(PAST_LEN + jnp.arange(ls))
    zero_states = (
        jnp.zeros((bp, MB_HEADS, MB_STATE, MB_HEADDIM), f32),
        jnp.zeros((bp, CONV_W - 1, MB_CONV_DIM), f32),
        jnp.zeros((bp, HG_HEADS, HG_DK, HG_DV), f32),
        jnp.zeros((bp, RET_HEADS, RET_DK, RET_DV), f32),
        jnp.zeros((bp, GDN_HEADS, GDN_DK, GDN_DV), f32),
        jnp.zeros((bp, CONV_W - 1, GDN_CONV_DIM), f32),
    )
    bt_p = math.gcd(bp, 8)
    bt_s = math.gcd(bs, 16)

    hp, hs = x_prompt, x_sample
    st_p, st_s = [], []
    for l in range(depth):
        w = _layer_weights(l, norm_g, ffn_w_in, ffn_w_out, w_in, w_branch, w_out, mb_conv_w, mb_conv_b,
                           mb_a_log, mb_dt_bias, mb_d, mb_norm, lb_all, hg_norm, gdn_conv_w, gdn_a_log,
                           gdn_dt_bias, gdn_norm)
        hp, sp = _layer(hp, w, zero_states, rope_p, lg, _chunks(lp), bt_p)
        past = (state_ssm[l], state_ssm_conv[l], state_hgrn[l], state_ret[l], state_gdn[l], state_gdn_conv[l])
        hs, ss = _layer(hs, w, past, rope_s, lg, _chunks(ls), bt_s)
        st_p.append(sp)
        st_s.append(ss)

    stack = lambda sts, i: jnp.stack([s[i] for s in sts], axis=0)
    return (hp, hs,
            stack(st_p, 0), stack(st_p, 1), stack(st_p, 2), stack(st_p, 3), stack(st_p, 4), stack(st_p, 5),
            stack(st_s, 0), stack(st_s, 1), stack(st_s, 2), stack(st_s, 3), stack(st_s, 4), stack(st_s, 5))
```

```python
import functools
import math

import numpy as np
import jax
import jax.numpy as jnp
from jax import lax
from jax.experimental import pallas as pl
from jax.experimental.pallas import tpu as pltpu

f32 = jnp.float32
bf16 = jnp.bfloat16

D_MODEL = 1024
D_FF = 2816
BRANCH_W = 512
N_BRANCH = 4
CONV_W = 4
EPS = 1e-6
ROPE_BASE = 10000.0
PAST_LEN = 16384

MB_HEADS, MB_HEADDIM, MB_STATE, MB_GROUPS = 8, 64, 64, 2
MB_CONV_DIM = BRANCH_W + 2 * MB_GROUPS * MB_STATE
HG_HEADS, HG_DK, HG_DV = 4, 128, 128
RET_HEADS, RET_DK, RET_DV = 4, 64, 128
GDN_HEADS, GDN_DK, GDN_DV = 4, 128, 128
GDN_CONV_DIM = 2 * GDN_HEADS * GDN_DK + GDN_HEADS * GDN_DV

IN_SPLITS = (
    BRANCH_W, MB_CONV_DIM, MB_HEADS,
    HG_HEADS * HG_DK, HG_HEADS * HG_DK, BRANCH_W, BRANCH_W,
    RET_HEADS * RET_DK, RET_HEADS * RET_DK, BRANCH_W, BRANCH_W,
    GDN_CONV_DIM, BRANCH_W, GDN_HEADS, GDN_HEADS,
    N_BRANCH * D_MODEL,
)
IN_OFFSETS = tuple(int(v) for v in np.cumsum((0,) + IN_SPLITS))

LANES = 128
SUBLANES = 8
MXU_DIM = 256
CONV_PAD = SUBLANES
VMEM_LIMIT = 56 * 1024 * 1024
FFN_TM = 1024
FFN_TF = D_FF // 2
MERGE_TM = 512
HG_SUB = SUBLANES


def _dot(a, b):
    return jnp.dot(a.astype(bf16), b.astype(bf16), preferred_element_type=f32)


def _dot_nt(a, b):
    return lax.dot_general(a.astype(bf16), b.astype(bf16), (((1,), (1,)), ((), ())), preferred_element_type=f32)


def _dot_tn(a, b):
    return lax.dot_general(a.astype(bf16), b.astype(bf16), (((0,), (0,)), ((), ())), preferred_element_type=f32)


def _split3(x):
    hi = x.astype(bf16)
    r = x - hi.astype(f32)
    mid = r.astype(bf16)
    lo = (r - mid.astype(f32)).astype(bf16)
    return hi, mid, lo


def _dot_sel_left(m01, x):
    hi, mid, lo = _split3(x)
    return (jnp.dot(m01, hi, preferred_element_type=f32) + jnp.dot(m01, mid, preferred_element_type=f32)
            + jnp.dot(m01, lo, preferred_element_type=f32))


def _rms(x):
    return x * lax.rsqrt(jnp.mean(x * x, axis=-1, keepdims=True) + EPS)


def _silu(x):
    return x * jax.nn.sigmoid(x)


def _iota2(n, m, axis):
    return lax.broadcasted_iota(jnp.int32, (n, m), axis)


def _log2(n):
    assert n > 0 and n & (n - 1) == 0, n
    return n.bit_length() - 1


def _ltri(c):
    return jnp.where(_iota2(c, c, 0) >= _iota2(c, c, 1), 1.0, 0.0).astype(bf16)


def _blockdiag_ltri(n, c):
    ii, jj = _iota2(n, n, 0), _iota2(n, n, 1)
    lc = _log2(c)
    return jnp.where(((ii >> lc) == (jj >> lc)) & (ii >= jj), 1.0, 0.0).astype(bf16)


def _row_bcast(col, c):
    dg = jnp.where(_iota2(c, c, 0) == _iota2(c, c, 1), jnp.broadcast_to(col, (c, c)), 0.0)
    return _dot_sel_left(jnp.ones((c, c), bf16), dg)


def _decay_matrix(col, c):
    diff = col - _row_bcast(col, c)
    return jnp.exp(jnp.where(_iota2(c, c, 0) >= _iota2(c, c, 1), diff, -jnp.inf))


def _causal_conv(xp_ref, w_ref, c):
    y = xp_ref[:, pl.ds(CONV_PAD - 3, c), :] * w_ref[0:1, :]
    for j in range(1, CONV_W):
        y = y + xp_ref[:, pl.ds(CONV_PAD - 3 + j, c), :] * w_ref[j:j + 1, :]
    return y


def _stack_geometry(bt, c, heads):
    gs = min(bt, max(1, MXU_DIM // (heads * c)))
    assert bt % gs == 0 and (gs * heads * c) % LANES == 0
    return gs, bt // gs, gs * heads * c


def _widen(x, r):
    return x if r == LANES else jnp.concatenate([x] * (r // LANES), axis=-1)


def _ffn_body(x_ref, g_ref, wg_ref, wu_ref, wo_ref, o_ref, xn_ref, acc_ref):
    j = pl.program_id(1)

    @pl.when(j == 0)
    def _():
        xn_ref[...] = (_rms(x_ref[...]) * g_ref[0:1, :]).astype(bf16)
        acc_ref[...] = jnp.zeros_like(acc_ref)

    xn = xn_ref[...]
    gate = jnp.dot(xn, wg_ref[...].astype(bf16), preferred_element_type=f32)
    up = jnp.dot(xn, wu_ref[...].astype(bf16), preferred_element_type=f32)
    act = (_silu(gate) * up).astype(bf16)
    acc_ref[...] += jnp.dot(act, wo_ref[...].astype(bf16), preferred_element_type=f32)

    @pl.when(j == pl.num_programs(1) - 1)
    def _():
        o_ref[...] = x_ref[...] + 0.5 * (_rms(acc_ref[...]) * g_ref[1:2, :])


def _ffn(x, g2, w_in, w_out, l, k):
    t = x.shape[0]
    tm = min(FFN_TM, t)
    nf = D_FF // FFN_TF
    return pl.pallas_call(
        _ffn_body,
        grid=(t // tm, nf),
        in_specs=[
            pl.BlockSpec((tm, D_MODEL), lambda i, j: (i, 0), pipeline_mode=pl.Buffered(1)),
            pl.BlockSpec((2, D_MODEL), lambda i, j: (0, 0)),
            pl.BlockSpec((None, None, D_MODEL, FFN_TF), lambda i, j: (l, k, 0, j)),
            pl.BlockSpec((None, None, D_MODEL, FFN_TF), lambda i, j: (l, k, 0, j + nf)),
            pl.BlockSpec((None, None, FFN_TF, D_MODEL), lambda i, j: (l, k, j, 0)),
        ],
        out_specs=pl.BlockSpec((tm, D_MODEL), lambda i, j: (i, 0)),
        out_shape=jax.ShapeDtypeStruct((t, D_MODEL), f32),
        scratch_shapes=[pltpu.VMEM((tm, D_MODEL), bf16), pltpu.VMEM((tm, D_MODEL), f32)],
        compiler_params=pltpu.CompilerParams(dimension_semantics=("arbitrary", "arbitrary"),
                                             vmem_limit_bytes=VMEM_LIMIT),
        name="ffn",
    )(x, g2, w_in, w_in, w_out)


def _merge_body(h_ref, g_ref, y0_ref, y1_ref, y2_ref, y3_ref, wg_ref, wb_ref, wo_ref, o_ref):
    h = h_ref[...]
    u = (_rms(h) * g_ref[0:1, :]).astype(bf16)
    merged = jnp.zeros(h.shape, f32)
    for n, y_ref in enumerate((y0_ref, y1_ref, y2_ref, y3_ref)):
        gate = jax.nn.sigmoid(jnp.dot(u, wg_ref[:, n * D_MODEL:(n + 1) * D_MODEL], preferred_element_type=f32))
        merged = merged + gate * jnp.dot(y_ref[...], wb_ref[n], preferred_element_type=f32)
    mixed = jnp.dot(merged.astype(bf16), wo_ref[...], preferred_element_type=f32)
    o_ref[...] = h + _rms(mixed) * g_ref[1:2, :]


def _merge(h, g2, ys, w_gate, w_branch, w_out):
    t = h.shape[0]
    tm = min(MERGE_TM, t)
    row = lambda i: (i, 0)
    return pl.pallas_call(
        _merge_body,
        grid=(t // tm,),
        in_specs=[
            pl.BlockSpec((tm, D_MODEL), row),
            pl.BlockSpec((2, D_MODEL), lambda i: (0, 0)),
            pl.BlockSpec((tm, BRANCH_W), row), pl.BlockSpec((tm, BRANCH_W), row),
            pl.BlockSpec((tm, BRANCH_W), row), pl.BlockSpec((tm, BRANCH_W), row),
            pl.BlockSpec((D_MODEL, N_BRANCH * D_MODEL), lambda i: (0, 0)),
            pl.BlockSpec((N_BRANCH, BRANCH_W, D_MODEL), lambda i: (0, 0, 0)),
            pl.BlockSpec((D_MODEL, D_MODEL), lambda i: (0, 0)),
        ],
        out_specs=pl.BlockSpec((tm, D_MODEL), row),
        out_shape=jax.ShapeDtypeStruct((t, D_MODEL), f32),
        compiler_params=pltpu.CompilerParams(dimension_semantics=("arbitrary",), vmem_limit_bytes=VMEM_LIMIT),
        name="merge",
    )(h, g2, *ys, w_gate, w_branch, w_out)


def _normed_input(h_ref, g_ref):
    bt, c, d = h_ref.shape
    return (_rms(h_ref[...].reshape(bt * c, d)) * g_ref[...]).astype(bf16)


def _const_spec(a):
    nd = a.ndim
    return pl.BlockSpec(a.shape, lambda i, j, _nd=nd: (0,) * _nd)


def _batch_spec(bt, a):
    nd = a.ndim
    return pl.BlockSpec((bt,) + a.shape[1:], lambda i, j, _nd=nd: (i,) + (0,) * (_nd - 1))


def _mixer_call(body, name, h, consts, batch_ins, c, bt, batch_outs, scratch):
    b, l, d = h.shape
    seq = lambda i, j: (i, j, 0)
    out_shape = [jax.ShapeDtypeStruct((b, l, BRANCH_W), bf16)] + [jax.ShapeDtypeStruct(s, f32) for s in batch_outs]
    out_specs = [pl.BlockSpec((bt, c, BRANCH_W), seq)] + [
        pl.BlockSpec((bt,) + s[1:], lambda i, j, _nd=len(s): (i,) + (0,) * (_nd - 1)) for s in batch_outs]
    return pl.pallas_call(
        body,
        grid=(b // bt, l // c),
        in_specs=[pl.BlockSpec((bt, c, d), seq)] + [_const_spec(a) for a in consts]
                 + [_batch_spec(bt, a) for a in batch_ins],
        out_specs=out_specs,
        out_shape=out_shape,
        scratch_shapes=scratch,
        compiler_params=pltpu.CompilerParams(dimension_semantics=("arbitrary", "arbitrary"),
                                             vmem_limit_bytes=VMEM_LIMIT),
        name=name,
    )(h, *consts, *batch_ins)


def _mamba_body(h_ref, g_ref, wz_ref, wx_ref, wdt_ref, cw_ref, cb_ref, dtb_ref, alog_ref, d_ref, nrm_ref,
                s0_ref, c0_ref, y_ref, so_ref, co_ref, s_scr, xp_scr, z_scr, xs_scr, c_scr, b_scr, v_scr, cum_scr):
    bt, c, _ = h_ref.shape
    j = pl.program_id(1)
    nh = MB_HEADS
    npair = nh // 2
    hpg = nh // MB_GROUPS
    n = bt * nh * c
    r = min(MXU_DIM, n)
    assert n % r == 0 and r % (2 * c) == 0
    ng = n // r

    @pl.when(j == 0)
    def _():
        s_scr[...] = jnp.zeros_like(s_scr)
        for p in range(npair):
            g = (2 * p) // hpg
            s_scr[:, p, g * MB_STATE:(g + 1) * MB_STATE, :] = jnp.concatenate(
                [s0_ref[:, 2 * p], s0_ref[:, 2 * p + 1]], axis=-1)
        xp_scr[:, 0:CONV_PAD, :] = c0_ref[...]

    u = _normed_input(h_ref, g_ref)
    z_scr[...] = jnp.dot(u, wz_ref[...], preferred_element_type=f32).reshape(bt, c, BRANCH_W)
    xp_scr[:, CONV_PAD:CONV_PAD + c, :] = jnp.dot(u, wx_ref[...], preferred_element_type=f32).reshape(
        bt, c, MB_CONV_DIM)
    dt = jax.nn.softplus(jnp.dot(u, wdt_ref[...], preferred_element_type=f32) + dtb_ref[...])
    cum = _dot_sel_left(_blockdiag_ltri(bt * c, c), -jnp.exp(alog_ref[...]) * dt)
    xa = _silu(_causal_conv(xp_scr, cw_ref, c) + cb_ref[...])
    tail = xp_scr[:, c:c + CONV_PAD, :]
    xp_scr[:, 0:CONV_PAD, :] = tail
    co_ref[...] = tail
    xs = xa[:, :, :BRANCH_W]
    bm = xa[:, :, BRANCH_W:BRANCH_W + LANES]
    cm = xa[:, :, BRANCH_W + LANES:BRANCH_W + 2 * LANES]
    xs_scr[...] = xs
    lane3 = lax.broadcasted_iota(jnp.int32, (bt, c, LANES), 2)
    lane_bcast = lambda col: jnp.broadcast_to(col, (bt * c, LANES)).reshape(bt, c, LANES)
    for h in range(nh):
        own_half = (lane3 >> _log2(MB_HEADDIM)) == (h % 2)
        v_scr[:, h] = jnp.where(own_half, xs[:, :, (h // 2) * LANES:(h // 2 + 1) * LANES] * lane_bcast(dt[:, h:h + 1]),
                                0.0)
        c_scr[:, h] = jnp.where((lane3 >> _log2(MB_STATE)) == (h // hpg), cm, 0.0)
        b_scr[:, h] = bm
        cum_scr[:, h] = lane_bcast(cum[:, h:h + 1])
    cf = c_scr[...].reshape(n, LANES)
    bf = b_scr[...].reshape(n, LANES)
    vf = v_scr[...].reshape(n, LANES)
    cbf = cum_scr[...].reshape(n, LANES)

    ii, jj = _iota2(r, r, 0), _iota2(r, r, 1)
    incl = ((ii >> _log2(c)) == (jj >> _log2(c))) & (ii >= jj)
    grp = lambda a, g: a[g * r:(g + 1) * r]
    dec = [jnp.exp(jnp.where(incl, _widen(grp(cbf, g), r) - grp(cbf, g).T[0:1, :], -jnp.inf)) for g in range(ng)]
    o_loc = [_dot(_dot_nt(grp(cf, g), grp(bf, g)) * dec[g], grp(vf, g)) for g in range(ng)]
    o_loc = o_loc[0] if ng == 1 else jnp.concatenate(o_loc, axis=0)

    cb3 = cbf.reshape(n // c, c, LANES)
    k_out = (vf.reshape(n // c, c, LANES) * jnp.exp(cb3[:, c - 1:c, :] - cb3)).reshape(n, LANES)
    e_in = jnp.exp(cbf)
    lo_half = _iota2(1, LANES, 1) < MB_HEADDIM
    row_lo = _iota2(2 * c, LANES, 0) < c
    own = row_lo == (_iota2(2 * c, LANES, 1) < MB_HEADDIM)
    pairs = [(b, p) for b in range(bt) for p in range(npair)]
    prow = lambda a, i: a[i * 2 * c:(i + 1) * 2 * c]
    s_old = [s_scr[b, p] for b, p in pairs]
    o_st = [jnp.where(own, _dot(prow(cf, i), s_old[i]) * prow(e_in, i), 0.0) for i in range(len(pairs))]
    for i, (b, p) in enumerate(pairs):
        cum_i = prow(cbf, i)
        e_all = jnp.exp(jnp.where(lo_half, cum_i[c - 1:c], cum_i[2 * c - 1:2 * c]))
        s_scr[b, p] = s_old[i] * e_all + _dot_tn(prow(bf, i), prow(k_out, i))
    for b in range(bt):
        ys = []
        for p in range(npair):
            o2 = prow(o_loc, b * npair + p) + o_st[b * npair + p]
            sl = slice(p * LANES, (p + 1) * LANES)
            ys.append(o2[:c] + o2[c:] + d_ref[:, sl] * xs_scr[b, :, sl])
        y = jnp.concatenate(ys, axis=-1) * _silu(z_scr[b])
        y_ref[b] = (_rms(y) * nrm_ref[...]).astype(bf16)

    @pl.when(j == pl.num_programs(1) - 1)
    def _():
        for p in range(npair):
            g = (2 * p) // hpg
            so_ref[:, 2 * p] = s_scr[:, p, g * MB_STATE:(g + 1) * MB_STATE, 0:MB_HEADDIM]
            so_ref[:, 2 * p + 1] = s_scr[:, p, g * MB_STATE:(g + 1) * MB_STATE, MB_HEADDIM:2 * MB_HEADDIM]


def _mamba(h, g2, w, s0, conv0, c, bt):
    b, l, _ = h.shape
    npair = MB_HEADS // 2
    consts = [g2, w["wz"], w["wx"], w["wdt"], w["cw"], w["cb"], w["dtb"], w["alog"], w["d"], w["nrm"]]
    scratch = [
        pltpu.VMEM((bt, npair, LANES, LANES), f32),
        pltpu.VMEM((bt, CONV_PAD + c, MB_CONV_DIM), f32),
        pltpu.VMEM((bt, c, BRANCH_W), f32),
        pltpu.VMEM((bt, c, BRANCH_W), f32),
    ] + [pltpu.VMEM((bt, MB_HEADS, c, LANES), f32)] * 4
    return _mixer_call(_mamba_body, "mamba", h, consts, [s0, conv0], c, bt,
                       [s0.shape, (b, CONV_PAD, MB_CONV_DIM)], scratch)


def _ret_body(h_ref, g_ref, wq_ref, wk_ref, wv_ref, wg_ref, cos_ref, sin_ref, lg_ref,
              s0_ref, y_ref, so_ref, s_scr, q_scr, k_scr, v_scr, gt_scr):
    bt, c, _ = h_ref.shape
    j = pl.program_id(1)
    kw = RET_HEADS * RET_DK
    hw = RET_DK // 2

    @pl.when(j == 0)
    def _():
        s_scr[...] = jnp.zeros_like(s_scr)
        for h in range(RET_HEADS):
            s_scr[:, h, h * hw:(h + 1) * hw, :] = s0_ref[:, h, 0:hw, :]
            s_scr[:, h, LANES + h * hw:LANES + (h + 1) * hw, :] = s0_ref[:, h, hw:2 * hw, :]

    u = _normed_input(h_ref, g_ref)
    cos = cos_ref[...]
    sin = sin_ref[...]

    def rot(x):
        x = x.reshape(bt, c, kw)
        x1, x2 = x[:, :, :LANES], x[:, :, LANES:]
        return jnp.concatenate([x1 * cos - x2 * sin, x1 * sin + x2 * cos], axis=-1)

    nh = RET_HEADS
    n = bt * nh * c
    r = min(MXU_DIM, n)
    assert n % r == 0 and r % c == 0
    ng = n // r
    q = rot(jnp.dot(u, wq_ref[...], preferred_element_type=f32))
    k = rot(jnp.dot(u, wk_ref[...], preferred_element_type=f32)) * (RET_DK ** -0.5)
    v = jnp.dot(u, wv_ref[...], preferred_element_type=f32).reshape(bt, c, BRANCH_W)
    gt_scr[...] = jnp.dot(u, wg_ref[...], preferred_element_type=f32).reshape(bt, c, BRANCH_W)
    klane = lax.broadcasted_iota(jnp.int32, (bt, c, kw), 2)
    for h in range(nh):
        own = ((klane & (LANES - 1)) >> _log2(hw)) == h
        q_scr[:, h] = jnp.where(own, q, 0.0)
        k_scr[:, h] = jnp.where(own, k, 0.0)
        v_scr[:, h] = v[:, :, h * RET_DV:(h + 1) * RET_DV]
    qf = q_scr[...].reshape(n, kw)
    kf = k_scr[...].reshape(n, kw)
    vf = v_scr[...].reshape(n, RET_DV)

    row = _iota2(n, 1, 0)
    head = (row >> _log2(c)) & (nh - 1)
    pos = (row & (c - 1)).astype(f32)
    lg_row = jnp.zeros((n, 1), f32)
    for h in range(nh):
        lg_row = jnp.where(head == h, lg_ref[:, h:h + 1], lg_row)
    q_in = qf * jnp.exp((pos + 1.0) * lg_row)
    k_out = kf * jnp.exp((float(c) - 1.0 - pos) * lg_row)
    e_all = jnp.exp(float(c) * lg_row)

    ii, jj = _iota2(r, r, 0), _iota2(r, r, 1)
    incl = ((ii >> _log2(c)) == (jj >> _log2(c))) & (ii >= jj)
    dist = (ii - jj).astype(f32)
    grp = lambda a, g: a[g * r:(g + 1) * r]
    dec = {}
    for g in range(ng):
        key = (g * r) % (nh * c)
        if key not in dec:
            dec[key] = jnp.exp(jnp.where(incl, dist * grp(lg_row, g), -jnp.inf))
    o_loc = [_dot(_dot_nt(grp(qf, g), grp(kf, g)) * dec[(g * r) % (nh * c)], grp(vf, g)) for g in range(ng)]
    o_loc = o_loc[0] if ng == 1 else jnp.concatenate(o_loc, axis=0)

    blocks = [(b, h) for b in range(bt) for h in range(nh)]
    rows = lambda a, i: a[i * c:(i + 1) * c]
    s_old = [s_scr[b, h] for b, h in blocks]
    o_st = [_dot(rows(q_in, i), s_old[i]) for i in range(len(blocks))]
    for i, (b, h) in enumerate(blocks):
        s_scr[b, h] = s_old[i] * rows(e_all, i)[0:1] + _dot_tn(rows(k_out, i), rows(vf, i))
    for b in range(bt):
        y = jnp.concatenate([_rms(rows(o_loc, b * nh + h) + o_st[b * nh + h]) for h in range(nh)], axis=-1)
        y_ref[b] = (y * _silu(gt_scr[b])).astype(bf16)

    @pl.when(j == pl.num_programs(1) - 1)
    def _():
        for h in range(RET_HEADS):
            so_ref[:, h, 0:hw, :] = s_scr[:, h, h * hw:(h + 1) * hw, :]
            so_ref[:, h, hw:2 * hw, :] = s_scr[:, h, LANES + h * hw:LANES + (h + 1) * hw, :]


def _ret(h, g2, w, cos, sin, lg, s0, c, bt):
    b, l, _ = h.shape
    kw = RET_HEADS * RET_DK
    consts = [g2, w["wq"], w["wk"], w["wv"], w["wg"]]
    scratch = [
        pltpu.VMEM((bt, RET_HEADS, kw, RET_DV), f32),
        pltpu.VMEM((bt, RET_HEADS, c, kw), f32),
        pltpu.VMEM((bt, RET_HEADS, c, kw), f32),
        pltpu.VMEM((bt, RET_HEADS, c, RET_DV), f32),
        pltpu.VMEM((bt, c, BRANCH_W), f32),
    ]
    seq = lambda i, j: (i, j, 0)
    out_shape = [jax.ShapeDtypeStruct((b, l, BRANCH_W), bf16), jax.ShapeDtypeStruct(s0.shape, f32)]
    st_spec = pl.BlockSpec((bt,) + s0.shape[1:], lambda i, j: (i, 0, 0, 0))
    return pl.pallas_call(
        _ret_body,
        grid=(b // bt, l // c),
        in_specs=[pl.BlockSpec((bt, c, D_MODEL), seq)] + [_const_spec(a) for a in consts]
                 + [pl.BlockSpec((c, LANES), lambda i, j: (j, 0)), pl.BlockSpec((c, LANES), lambda i, j: (j, 0)),
                    _const_spec(lg), st_spec],
        out_specs=[pl.BlockSpec((bt, c, BRANCH_W), seq), st_spec],
        out_shape=out_shape,
        scratch_shapes=scratch,
        compiler_params=pltpu.CompilerParams(dimension_semantics=("arbitrary", "arbitrary"),
                                             vmem_limit_bytes=VMEM_LIMIT),
        name="ret",
    )(h, *consts, cos, sin, lg, s0)


def _hgrn_body(h_ref, g_ref, wq_ref, wf_ref, wi_ref, wg_ref, lb_ref, nrm_ref,
               s0_ref, y_ref, so_ref, st_scr, q_scr, k_scr, v_scr, cum_scr, gt_scr):
    bt, c, _ = h_ref.shape
    j = pl.program_id(1)
    nh = HG_HEADS
    gs, ng, r = _stack_geometry(bt, c, nh)
    n = bt * nh * c
    sub = HG_SUB

    @pl.when(j == 0)
    def _():
        def init(b, carry):
            for h in range(nh):
                st_scr[b, h] = s0_ref[b, h].T
            return carry
        lax.fori_loop(0, bt, init, 0)

    u = _normed_input(h_ref, g_ref)
    q = _silu(jnp.dot(u, wq_ref[...], preferred_element_type=f32))
    lb = lb_ref[...]
    fg = lb + (1.0 - lb) * jax.nn.sigmoid(jnp.dot(u, wf_ref[...], preferred_element_type=f32))
    v = jnp.dot(u, wi_ref[...], preferred_element_type=f32)
    gt_scr[...] = jnp.dot(u, wg_ref[...], preferred_element_type=f32).reshape(bt, c, BRANCH_W)
    cum = _dot_sel_left(_blockdiag_ltri(bt * c, c), jnp.log(fg))
    k = 1.0 - fg
    for h in range(nh):
        sl = slice(h * HG_DK, (h + 1) * HG_DK)
        q_scr[:, h] = q[:, sl].reshape(bt, c, HG_DK)
        k_scr[:, h] = k[:, sl].reshape(bt, c, HG_DK)
        v_scr[:, h] = v[:, sl].reshape(bt, c, HG_DV)
        cum_scr[:, h] = cum[:, sl].reshape(bt, c, HG_DK)
    qf = q_scr[...].reshape(n, HG_DK)
    kf = k_scr[...].reshape(n, HG_DK)
    vf = v_scr[...].reshape(n, HG_DV)
    cf = cum_scr[...].reshape(n, HG_DK)

    ii, jj = _iota2(r, r, 0), _iota2(r, r, 1)
    scores = [jnp.zeros((r, r), f32) for _ in range(ng)]

    q3, k3, c3 = (a.reshape(n // sub, sub, HG_DK) for a in (qf, kf, cf))
    sub_i = lax.broadcasted_iota(jnp.int32, q3.shape, 1)
    ones = jnp.ones((HG_DK, r), bf16)
    same_sub = (ii >> _log2(sub)) == (jj >> _log2(sub))
    for t in range(sub):
        pt = jnp.where(sub_i >= t, q3 * k3[:, t:t + 1, :] * jnp.exp(c3 - c3[:, t:t + 1, :]), 0.0)
        row_sum = jnp.dot(pt.reshape(n, HG_DK).astype(bf16), ones, preferred_element_type=f32)
        place = same_sub & ((jj & (sub - 1)) == t)
        scores = [jnp.where(place, row_sum[g * r:(g + 1) * r], scores[g]) for g in range(ng)]

    m = sub
    while m < c:
        q4, k4, c4 = (a.reshape(n // (2 * m), 2, m, HG_DK) for a in (qf, kf, cf))
        bnd = c4[:, 0:1, m - 1:m, :]
        odd = lax.broadcasted_iota(jnp.int32, q4.shape, 1) == 1
        e = jnp.exp(jnp.where(odd, c4 - bnd, bnd - c4))
        q_t = jnp.where(odd, q4 * e, 0.0).reshape(n, HG_DK)
        k_t = jnp.where(odd, 0.0, k4 * e).reshape(n, HG_DK)
        same_pair = (ii >> _log2(2 * m)) == (jj >> _log2(2 * m))
        scores = [scores[g] + jnp.where(same_pair, _dot_nt(q_t[g * r:(g + 1) * r], k_t[g * r:(g + 1) * r]), 0.0)
                  for g in range(ng)]
        m *= 2

    cb3 = cf.reshape(n // c, c, HG_DK)
    clast3 = cb3[:, c - 1:c, :]
    q_in = qf * jnp.exp(cf)
    k_out = (kf.reshape(n // c, c, HG_DK) * jnp.exp(clast3 - cb3)).reshape(n, HG_DK)
    e_all = jnp.exp(clast3)
    blocks = [(b, h) for b in range(bt) for h in range(nh)]
    rows = lambda a, i: a[i * c:(i + 1) * c]
    st_old = [st_scr[b, h] for b, h in blocks]
    o_state = [_dot_nt(rows(q_in, i), st_old[i]) for i in range(len(blocks))]
    for i, (b, h) in enumerate(blocks):
        st_scr[b, h] = st_old[i] * e_all[i] + _dot_tn(rows(vf, i), rows(k_out, i))
    for g in range(ng):
        o = _dot(scores[g], vf[g * r:(g + 1) * r])
        for s in range(gs):
            b = g * gs + s
            y = jnp.concatenate([_rms(rows(o, s * nh + h) + o_state[b * nh + h]) * nrm_ref[...] for h in range(nh)],
                                axis=-1)
            y_ref[b] = (y * _silu(gt_scr[b])).astype(bf16)

    @pl.when(j == pl.num_programs(1) - 1)
    def _():
        def fin(b, carry):
            for h in range(nh):
                so_ref[b, h] = st_scr[b, h].T
            return carry
        lax.fori_loop(0, bt, fin, 0)


def _hgrn(h, g2, w, s0, c, bt):
    consts = [g2, w["wq"], w["wf"], w["wi"], w["wg"], w["lb"], w["nrm"]]
    scratch = ([pltpu.VMEM((bt, HG_HEADS, HG_DV, HG_DK), f32)] + [pltpu.VMEM((bt, HG_HEADS, c, HG_DK), f32)] * 4
               + [pltpu.VMEM((bt, c, BRANCH_W), f32)])
    return _mixer_call(_hgrn_body, "hgrn", h, consts, [s0], c, bt, [s0.shape], scratch)


def _unit_lower_inverses(a_list, r, c):
    ii, jj = _iota2(r, r, 0), _iota2(r, r, 1)
    eye = jnp.where(ii == jj, 1.0, 0.0)
    t_list = [eye for _ in a_list]
    m = 1
    while m < c:
        lm = _log2(m)
        join = ((ii >> (lm + 1)) == (jj >> (lm + 1))) & (((ii >> lm) & 1) == 1) & (((jj >> lm) & 1) == 0)
        f_list = [jnp.where(join, a, 0.0) for a in a_list]
        if m == 1:
            t_list = [eye - f for f in f_list]
        else:
            ft_list = [_dot(f, t) for f, t in zip(f_list, t_list)]
            t_list = [t - _dot(t, ft) for t, ft in zip(t_list, ft_list)]
        m *= 2
    return t_list


def _gdn_body(h_ref, g_ref, wqkv_ref, wz_ref, wab_ref, cw_ref, dtb_ref, alog_ref, nrm_ref,
              s0_ref, c0_ref, y_ref, so_ref, co_ref, s_scr, xp_scr, z_scr, q_scr, k_scr, v_scr, cum_scr, beta_scr):
    bt, c, _ = h_ref.shape
    j = pl.program_id(1)
    nh = GDN_HEADS
    hk = nh * GDN_DK
    gs, ng, r = _stack_geometry(bt, c, nh)

    @pl.when(j == 0)
    def _():
        s_scr[...] = s0_ref[...]
        xp_scr[:, 0:CONV_PAD, :] = c0_ref[...]

    u = _normed_input(h_ref, g_ref)
    xp_scr[:, CONV_PAD:CONV_PAD + c, :] = jnp.dot(u, wqkv_ref[...], preferred_element_type=f32).reshape(
        bt, c, GDN_CONV_DIM)
    z_scr[...] = jnp.dot(u, wz_ref[...], preferred_element_type=f32).reshape(bt, c, BRANCH_W)
    ab = jnp.dot(u, wab_ref[...], preferred_element_type=f32)
    log_g = -jnp.exp(alog_ref[...]) * jax.nn.softplus(ab + dtb_ref[...])
    cum = _dot_sel_left(_blockdiag_ltri(bt * c, c), log_g)
    beta = jax.nn.sigmoid(ab)
    qkv = _silu(_causal_conv(xp_scr, cw_ref, c))
    tail = xp_scr[:, c:c + CONV_PAD, :]
    xp_scr[:, 0:CONV_PAD, :] = tail
    co_ref[...] = tail
    lane_bcast = lambda col: jnp.broadcast_to(col, (bt * c, LANES)).reshape(bt, c, LANES)
    for h in range(nh):
        q = qkv[:, :, h * GDN_DK:(h + 1) * GDN_DK]
        k = qkv[:, :, hk + h * GDN_DK:hk + (h + 1) * GDN_DK]
        q_scr[:, h] = q * lax.rsqrt(jnp.sum(q * q, axis=-1, keepdims=True) + EPS) * (GDN_DK ** -0.5)
        k_scr[:, h] = k * lax.rsqrt(jnp.sum(k * k, axis=-1, keepdims=True) + EPS)
        v_scr[:, h] = qkv[:, :, 2 * hk + h * GDN_DV:2 * hk + (h + 1) * GDN_DV]
        cum_scr[:, h] = lane_bcast(cum[:, h:h + 1])
        beta_scr[:, h] = lane_bcast(beta[:, nh + h:nh + h + 1])

    grp = lambda ref, g: ref[g * gs:(g + 1) * gs].reshape(r, LANES)
    groups = range(ng)
    qs = [grp(q_scr, g) for g in groups]
    ks = [grp(k_scr, g) for g in groups]
    vs = [grp(v_scr, g) for g in groups]
    cb = [grp(cum_scr, g) for g in groups]
    bb = [grp(beta_scr, g) for g in groups]
    ii, jj = _iota2(r, r, 0), _iota2(r, r, 1)
    same = (ii >> _log2(c)) == (jj >> _log2(c))
    incl = same & (ii >= jj)
    strict = same & (ii > jj)
    dec = [jnp.exp(jnp.where(incl, _widen(cb[g], r) - cb[g].T[0:1, :], -jnp.inf)) for g in groups]
    a_mat = [jnp.where(strict, _widen(bb[g], r) * dec[g] * _dot_nt(ks[g], ks[g]), 0.0) for g in groups]
    t_inv = _unit_lower_inverses(a_mat, r, c)
    e_in = [jnp.exp(cb[g]) for g in groups]
    sol = [_dot(t_inv[g], jnp.concatenate([vs[g] * bb[g], ks[g] * (e_in[g] * bb[g])], axis=-1)) for g in groups]
    scd = [_dot_nt(qs[g], ks[g]) * dec[g] for g in groups]
    q_in = [qs[g] * e_in[g] for g in groups]

    rows = lambda a, i: a[i * c:(i + 1) * c]
    for g in groups:
        blocks = [(g * gs + s, h) for s in range(gs) for h in range(nh)]
        nb = len(blocks)
        s_old = [s_scr[b, h] for b, h in blocks]
        proj = [_dot(jnp.concatenate([rows(sol[g], i)[:, GDN_DV:], rows(q_in[g], i)], axis=0), s_old[i])
                for i in range(nb)]
        w_new = [rows(sol[g], i)[:, :GDN_DV] - proj[i][:c] for i in range(nb)]
        for i, (b, h) in enumerate(blocks):
            cum_i = rows(cb[g], i)
            clast = cum_i[c - 1:c]
            s_scr[b, h] = s_old[i] * jnp.exp(clast) + _dot_tn(rows(ks[g], i) * jnp.exp(clast - cum_i), w_new[i])
        o = _dot(scd[g], jnp.concatenate(w_new, axis=0)) + jnp.concatenate([p[c:] for p in proj], axis=0)
        for s in range(gs):
            b = g * gs + s
            y = jnp.concatenate([_rms(rows(o, s * nh + h)) * nrm_ref[...] for h in range(nh)], axis=-1)
            y_ref[b] = (y * _silu(z_scr[b])).astype(bf16)

    @pl.when(j == pl.num_programs(1) - 1)
    def _():
        so_ref[...] = s_scr[...]


def _gdn(h, g2, w, s0, conv0, c, bt):
    b, l, _ = h.shape
    consts = [g2, w["wqkv"], w["wz"], w["wab"], w["cw"], w["dtb"], w["alog"], w["nrm"]]
    scratch = [
        pltpu.VMEM((bt, GDN_HEADS, GDN_DK, GDN_DV), f32),
        pltpu.VMEM((bt, CONV_PAD + c, GDN_CONV_DIM), f32),
        pltpu.VMEM((bt, c, BRANCH_W), f32),
    ] + [pltpu.VMEM((bt, GDN_HEADS, c, LANES), f32)] * 5
    return _mixer_call(_gdn_body, "gdn", h, consts, [s0, conv0], c, bt,
                       [s0.shape, (b, CONV_PAD, GDN_CONV_DIM)], scratch)


def _pad_lanes(v, offset=0):
    row = jnp.zeros((1, LANES), f32)
    return row.at[0, offset:offset + v.shape[0]].set(v.astype(f32))


def _layer_weights(l, norm_g, ffn_w_in, ffn_w_out, w_in, w_branch, w_out, mb_conv_w, mb_conv_b, mb_a_log,
                   mb_dt_bias, mb_d, mb_norm, lb_all, hg_norm, gdn_conv_w, gdn_a_log, gdn_dt_bias, gdn_norm):
    wi = w_in[l]
    col = lambda n: wi[:, IN_OFFSETS[n]:IN_OFFSETS[n + 1]]
    colb = lambda n: col(n).astype(bf16)
    pad_cols = lambda a: jnp.pad(a, ((0, 0), (0, LANES - a.shape[1]))).astype(bf16)
    hw = RET_DK // 2
    perm = np.arange(RET_HEADS * RET_DK).reshape(RET_HEADS, 2, hw).transpose(1, 0, 2).reshape(-1)
    row = lambda v: v.astype(f32).reshape(1, -1)
    return {
        "g_ffn1": norm_g[l, 0:2], "g_mix": norm_g[l, 2:4], "g_ffn2": norm_g[l, 4:6], "g_u": norm_g[l, 2:3],
        "layer": l, "ffn_in": ffn_w_in, "ffn_out": ffn_w_out,
        "mamba": {"wz": colb(0), "wx": colb(1), "wdt": pad_cols(col(2)), "cw": mb_conv_w[l], "cb": row(mb_conv_b[l]),
                  "dtb": _pad_lanes(mb_dt_bias[l]), "alog": _pad_lanes(mb_a_log[l]),
                  "d": row(jnp.repeat(mb_d[l], MB_HEADDIM)), "nrm": row(mb_norm[l])},
        "hgrn": {"wq": colb(3), "wf": colb(4), "wi": colb(5), "wg": colb(6), "lb": row(lb_all[l]),
                 "nrm": row(hg_norm[l])},
        "ret": {"wq": col(7)[:, perm].astype(bf16), "wk": col(8)[:, perm].astype(bf16), "wv": colb(9), "wg": colb(10)},
        "gdn": {"wqkv": colb(11), "wz": colb(12), "wab": pad_cols(jnp.concatenate([col(13), col(14)], axis=1)),
                "cw": gdn_conv_w[l], "dtb": _pad_lanes(gdn_dt_bias[l]), "alog": _pad_lanes(gdn_a_log[l]),
                "nrm": row(gdn_norm[l])},
        "w_gate": colb(15), "w_branch": w_branch[l].astype(bf16), "w_out": w_out[l].astype(bf16),
    }


def _rope_tables(pos):
    half = RET_DK // 2
    inv_freq = 1.0 / (ROPE_BASE ** jnp.linspace(0.0, 1.0, half, dtype=f32))
    ang = pos.astype(f32)[:, None] * inv_freq[None, :]
    reps = LANES // half
    return jnp.tile(jnp.cos(ang), (1, reps)), jnp.tile(jnp.sin(ang), (1, reps))


def _pad_conv(buf):
    return jnp.pad(buf, ((0, 0), (CONV_PAD - (CONV_W - 1), 0), (0, 0)))


def _layer(h, w, states, rope, lg, chunks, bt):
    b, l, d = h.shape
    s_ssm, s_ssm_conv, s_hg, s_ret, s_gdn, s_gdn_conv = states
    h = _ffn(h.reshape(b * l, d), w["g_ffn1"], w["ffn_in"], w["ffn_out"], w["layer"], 0).reshape(b, l, d)
    y_mb, n_ssm, n_ssm_conv = _mamba(h, w["g_u"], w["mamba"], s_ssm, _pad_conv(s_ssm_conv),
                                     chunks["mamba"], bt)
    y_hg, n_hg = _hgrn(h, w["g_u"], w["hgrn"], s_hg, chunks["hgrn"], bt)
    y_rt, n_ret = _ret(h, w["g_u"], w["ret"], rope[0], rope[1], lg, s_ret, chunks["ret"], bt)
    y_gd, n_gdn, n_gdn_conv = _gdn(h, w["g_u"], w["gdn"], s_gdn, _pad_conv(s_gdn_conv), chunks["gdn"], bt)
    ys = [y.reshape(b * l, BRANCH_W) for y in (y_mb, y_hg, y_rt, y_gd)]
    hf = _merge(h.reshape(b * l, d), w["g_mix"], ys, w["w_gate"], w["w_branch"], w["w_out"])
    hf = _ffn(hf, w["g_ffn2"], w["ffn_in"], w["ffn_out"], w["layer"], 1)
    new_states = (n_ssm, n_ssm_conv[:, CONV_PAD - (CONV_W - 1):], n_hg, n_ret, n_gdn,
                  n_gdn_conv[:, CONV_PAD - (CONV_W - 1):])
    return hf.reshape(b, l, d), new_states


def _chunks(l):
    pick = lambda pref: pref if l % pref == 0 else math.gcd(l, pref)
    return {"mamba": pick(128), "ret": pick(128), "hgrn": pick(64), "gdn": pick(64)}


def kernel(x_prompt, x_sample, state_ssm, state_ssm_conv, state_hgrn, state_ret, state_gdn, state_gdn_conv,
           norm_g, ffn_w_in, ffn_w_out, w_in, w_branch, w_out, mb_conv_w, mb_conv_b, mb_a_log, mb_dt_bias, mb_d,
           mb_norm, hg_lb_logits, hg_norm, gdn_conv_w, gdn_a_log, gdn_dt_bias, gdn_norm):
    depth = norm_g.shape[0]
    lb_all = jnp.cumsum(jax.nn.softmax(hg_lb_logits.astype(f32), axis=0), axis=0)
    lb_all = lb_all - lb_all[0:1]
    lg = _pad_lanes(jnp.log(1.0 - jnp.exp2(-5.0 - jnp.arange(RET_HEADS, dtype=f32))))

    bp, lp, _ = x_prompt.shape
    bs, ls, _ = x_sample.shape
    rope_p = _rope_tables(jnp.arange(lp))
    rope_s = _rope_tables(PAST_LEN + jnp.arange(ls))
    zero_states = (
        jnp.zeros((bp, MB_HEADS, MB_STATE, MB_HEADDIM), f32),
        jnp.zeros((bp, CONV_W - 1, MB_CONV_DIM), f32),
        jnp.zeros((bp, HG_HEADS, HG_DK, HG_DV), f32),
        jnp.zeros((bp, RET_HEADS, RET_DK, RET_DV), f32),
        jnp.zeros((bp, GDN_HEADS, GDN_DK, GDN_DV), f32),
        jnp.zeros((bp, CONV_W - 1, GDN_CONV_DIM), f32),
    )
    bt_p = math.gcd(bp, 8)
    bt_s = math.gcd(bs, 16)

    ffn_w_in = ffn_w_in.astype(bf16)
    ffn_w_out = ffn_w_out.astype(bf16)
    hp, hs = x_prompt, x_sample
    st_p, st_s = [], []
    for l in range(depth):
        w = _layer_weights(l, norm_g, ffn_w_in, ffn_w_out, w_in, w_branch, w_out, mb_conv_w, mb_conv_b,
                           mb_a_log, mb_dt_bias, mb_d, mb_norm, lb_all, hg_norm, gdn_conv_w, gdn_a_log,
                           gdn_dt_bias, gdn_norm)
        hp, sp = _layer(hp, w, zero_states, rope_p, lg, _chunks(lp), bt_p)
        past = (state_ssm[l], state_ssm_conv[l], state_hgrn[l], state_ret[l], state_gdn[l], state_gdn_conv[l])
        hs, ss = _layer(hs, w, past, rope_s, lg, _chunks(ls), bt_s)
        st_p.append(sp)
        st_s.append(ss)

    stack = lambda sts, i: jnp.stack([s[i] for s in sts], axis=0)
    return (hp, hs,
            stack(st_p, 0), stack(st_p, 1), stack(st_p, 2), stack(st_p, 3), stack(st_p, 4), stack(st_p, 5),
            stack(st_s, 0), stack(st_s, 1), stack(st_s, 2), stack(st_s, 3), stack(st_s, 4), stack(st_s, 5))
```

```python
import functools
import math

import numpy as np
import jax
import jax.numpy as jnp
from jax import lax
from jax.experimental import pallas as pl
from jax.experimental.pallas import tpu as pltpu

f32 = jnp.float32
bf16 = jnp.bfloat16

D_MODEL = 1024
D_FF = 2816
BRANCH_W = 512
N_BRANCH = 4
CONV_W = 4
EPS = 1e-6
ROPE_BASE = 10000.0
PAST_LEN = 16384

MB_HEADS, MB_HEADDIM, MB_STATE, MB_GROUPS = 8, 64, 64, 2
MB_CONV_DIM = BRANCH_W + 2 * MB_GROUPS * MB_STATE
HG_HEADS, HG_DK, HG_DV = 4, 128, 128
RET_HEADS, RET_DK, RET_DV = 4, 64, 128
GDN_HEADS, GDN_DK, GDN_DV = 4, 128, 128
GDN_CONV_DIM = 2 * GDN_HEADS * GDN_DK + GDN_HEADS * GDN_DV

IN_SPLITS = (
    BRANCH_W, MB_CONV_DIM, MB_HEADS,
    HG_HEADS * HG_DK, HG_HEADS * HG_DK, BRANCH_W, BRANCH_W,
    RET_HEADS * RET_DK, RET_HEADS * RET_DK, BRANCH_W, BRANCH_W,
    GDN_CONV_DIM, BRANCH_W, GDN_HEADS, GDN_HEADS,
    N_BRANCH * D_MODEL,
)
IN_OFFSETS = tuple(int(v) for v in np.cumsum((0,) + IN_SPLITS))

LANES = 128
SUBLANES = 8
MXU_DIM = 256
CONV_PAD = SUBLANES
VMEM_LIMIT = 56 * 1024 * 1024
FFN_TM = 512
MERGE_TM = 512
HG_SUB = SUBLANES


def _dot(a, b):
    return jnp.dot(a.astype(bf16), b.astype(bf16), preferred_element_type=f32)


def _dot_nt(a, b):
    return lax.dot_general(a.astype(bf16), b.astype(bf16), (((1,), (1,)), ((), ())), preferred_element_type=f32)


def _dot_tn(a, b):
    return lax.dot_general(a.astype(bf16), b.astype(bf16), (((0,), (0,)), ((), ())), preferred_element_type=f32)


def _split3(x):
    hi = x.astype(bf16)
    r = x - hi.astype(f32)
    mid = r.astype(bf16)
    lo = (r - mid.astype(f32)).astype(bf16)
    return hi, mid, lo


def _dot_sel_left(m01, x):
    hi, mid, lo = _split3(x)
    return (jnp.dot(m01, hi, preferred_element_type=f32) + jnp.dot(m01, mid, preferred_element_type=f32)
            + jnp.dot(m01, lo, preferred_element_type=f32))


def _rms(x):
    return x * lax.rsqrt(jnp.mean(x * x, axis=-1, keepdims=True) + EPS)


def _silu(x):
    return x * jax.nn.sigmoid(x)


def _iota2(n, m, axis):
    return lax.broadcasted_iota(jnp.int32, (n, m), axis)


def _log2(n):
    assert n > 0 and n & (n - 1) == 0, n
    return n.bit_length() - 1


def _ltri(c):
    return jnp.where(_iota2(c, c, 0) >= _iota2(c, c, 1), 1.0, 0.0).astype(bf16)


def _blockdiag_ltri(n, c):
    ii, jj = _iota2(n, n, 0), _iota2(n, n, 1)
    lc = _log2(c)
    return jnp.where(((ii >> lc) == (jj >> lc)) & (ii >= jj), 1.0, 0.0).astype(bf16)


def _row_bcast(col, c):
    dg = jnp.where(_iota2(c, c, 0) == _iota2(c, c, 1), jnp.broadcast_to(col, (c, c)), 0.0)
    return _dot_sel_left(jnp.ones((c, c), bf16), dg)


def _decay_matrix(col, c):
    diff = col - _row_bcast(col, c)
    return jnp.exp(jnp.where(_iota2(c, c, 0) >= _iota2(c, c, 1), diff, -jnp.inf))


def _causal_conv(xp_ref, w_ref, c):
    y = xp_ref[:, pl.ds(CONV_PAD - 3, c), :] * w_ref[0:1, :]
    for j in range(1, CONV_W):
        y = y + xp_ref[:, pl.ds(CONV_PAD - 3 + j, c), :] * w_ref[j:j + 1, :]
    return y


def _stack_geometry(bt, c, heads):
    gs = min(bt, max(1, MXU_DIM // (heads * c)))
    assert bt % gs == 0 and (gs * heads * c) % LANES == 0
    return gs, bt // gs, gs * heads * c


def _widen(x, r):
    return x if r == LANES else jnp.concatenate([x] * (r // LANES), axis=-1)


def _ffn_body(x_ref, g_ref, wg_ref, wu_ref, wo_ref, o_ref):
    x = x_ref[...]
    xn = (_rms(x) * g_ref[0:1, :]).astype(bf16)
    gate = jnp.dot(xn, wg_ref[...], preferred_element_type=f32)
    up = jnp.dot(xn, wu_ref[...], preferred_element_type=f32)
    act = (_silu(gate) * up).astype(bf16)
    y = jnp.dot(act, wo_ref[...], preferred_element_type=f32)
    o_ref[...] = x + 0.5 * (_rms(y) * g_ref[1:2, :])


def _ffn(x, g2, w_in, w_out, l, k):
    t = x.shape[0]
    tm = min(FFN_TM, t)
    resident = pl.Buffered(1)
    return pl.pallas_call(
        _ffn_body,
        grid=(t // tm,),
        in_specs=[
            pl.BlockSpec((tm, D_MODEL), lambda i: (i, 0)),
            pl.BlockSpec((2, D_MODEL), lambda i: (0, 0)),
            pl.BlockSpec((None, None, D_MODEL, D_FF), lambda i: (l, k, 0, 0), pipeline_mode=resident),
            pl.BlockSpec((None, None, D_MODEL, D_FF), lambda i: (l, k, 0, 1), pipeline_mode=resident),
            pl.BlockSpec((None, None, D_FF, D_MODEL), lambda i: (l, k, 0, 0), pipeline_mode=resident),
        ],
        out_specs=pl.BlockSpec((tm, D_MODEL), lambda i: (i, 0)),
        out_shape=jax.ShapeDtypeStruct((t, D_MODEL), f32),
        compiler_params=pltpu.CompilerParams(dimension_semantics=("arbitrary",), vmem_limit_bytes=VMEM_LIMIT),
        name="ffn",
    )(x, g2, w_in, w_in, w_out)


def _merge_body(h_ref, g_ref, y0_ref, y1_ref, y2_ref, y3_ref, wg_ref, wb_ref, wo_ref, o_ref):
    h = h_ref[...]
    u = (_rms(h) * g_ref[0:1, :]).astype(bf16)
    merged = jnp.zeros(h.shape, f32)
    for n, y_ref in enumerate((y0_ref, y1_ref, y2_ref, y3_ref)):
        gate = jax.nn.sigmoid(jnp.dot(u, wg_ref[:, n * D_MODEL:(n + 1) * D_MODEL], preferred_element_type=f32))
        merged = merged + gate * jnp.dot(y_ref[...], wb_ref[n], preferred_element_type=f32)
    mixed = jnp.dot(merged.astype(bf16), wo_ref[...], preferred_element_type=f32)
    o_ref[...] = h + _rms(mixed) * g_ref[1:2, :]


def _merge(h, g2, ys, w_gate, w_branch, w_out):
    t = h.shape[0]
    tm = min(MERGE_TM, t)
    row = lambda i: (i, 0)
    return pl.pallas_call(
        _merge_body,
        grid=(t // tm,),
        in_specs=[
            pl.BlockSpec((tm, D_MODEL), row),
            pl.BlockSpec((2, D_MODEL), lambda i: (0, 0)),
            pl.BlockSpec((tm, BRANCH_W), row), pl.BlockSpec((tm, BRANCH_W), row),
            pl.BlockSpec((tm, BRANCH_W), row), pl.BlockSpec((tm, BRANCH_W), row),
            pl.BlockSpec((D_MODEL, N_BRANCH * D_MODEL), lambda i: (0, 0)),
            pl.BlockSpec((N_BRANCH, BRANCH_W, D_MODEL), lambda i: (0, 0, 0)),
            pl.BlockSpec((D_MODEL, D_MODEL), lambda i: (0, 0)),
        ],
        out_specs=pl.BlockSpec((tm, D_MODEL), row),
        out_shape=jax.ShapeDtypeStruct((t, D_MODEL), f32),
        compiler_params=pltpu.CompilerParams(dimension_semantics=("arbitrary",), vmem_limit_bytes=VMEM_LIMIT),
        name="merge",
    )(h, g2, *ys, w_gate, w_branch, w_out)


def _normed_input(h_ref, g_ref):
    bt, c, d = h_ref.shape
    return (_rms(h_ref[...].reshape(bt * c, d)) * g_ref[...]).astype(bf16)


def _const_spec(a):
    nd = a.ndim
    return pl.BlockSpec(a.shape, lambda i, j, _nd=nd: (0,) * _nd)


def _batch_spec(bt, a):
    nd = a.ndim
    return pl.BlockSpec((bt,) + a.shape[1:], lambda i, j, _nd=nd: (i,) + (0,) * (_nd - 1))


def _mixer_call(body, name, h, consts, batch_ins, c, bt, batch_outs, scratch, slot, chunk_ins=()):
    b, l, d = h.shape
    layer, depth, prev = slot
    seq = lambda i, j: (i, j, 0)
    st = batch_outs[0]
    out_shape = ([jax.ShapeDtypeStruct((b, l, BRANCH_W), bf16), jax.ShapeDtypeStruct((depth,) + tuple(st), f32)]
                 + [jax.ShapeDtypeStruct(s, f32) for s in batch_outs[1:]])
    out_specs = ([pl.BlockSpec((bt, c, BRANCH_W), seq),
                  pl.BlockSpec((None, bt) + tuple(st[1:]), lambda i, j, _nd=len(st): (layer, i) + (0,) * (_nd - 1))]
                 + [pl.BlockSpec((bt,) + s[1:], lambda i, j, _nd=len(s): (i,) + (0,) * (_nd - 1))
                    for s in batch_outs[1:]])
    operands = [h, *consts, *chunk_ins, *batch_ins]
    in_specs = ([pl.BlockSpec((bt, c, d), seq)] + [_const_spec(a) for a in consts]
                + [pl.BlockSpec((c, a.shape[1]), lambda i, j: (j, 0)) for a in chunk_ins]
                + [_batch_spec(bt, a) for a in batch_ins])
    n_in = len(operands)
    aliases = {}
    fn = body
    if prev is not None:
        operands.append(prev)
        in_specs.append(pl.BlockSpec(memory_space=pl.ANY))
        aliases = {n_in: 1}
        fn = lambda *refs: body(*refs[:n_in], *refs[n_in + 1:])
    return pl.pallas_call(
        fn,
        grid=(b // bt, l // c),
        in_specs=in_specs,
        out_specs=out_specs,
        out_shape=out_shape,
        scratch_shapes=scratch,
        input_output_aliases=aliases,
        compiler_params=pltpu.CompilerParams(dimension_semantics=("arbitrary", "arbitrary"),
                                             vmem_limit_bytes=VMEM_LIMIT),
        name=name,
    )(*operands)


def _mamba_body(h_ref, g_ref, wz_ref, wx_ref, wdt_ref, cw_ref, cb_ref, dtb_ref, alog_ref, d_ref, nrm_ref,
                s0_ref, c0_ref, y_ref, so_ref, co_ref, s_scr, xp_scr, z_scr, xs_scr, c_scr, b_scr, v_scr, cum_scr):
    bt, c, _ = h_ref.shape
    j = pl.program_id(1)
    nh = MB_HEADS
    npair = nh // 2
    hpg = nh // MB_GROUPS
    n = bt * nh * c
    r = min(MXU_DIM, n)
    assert n % r == 0 and r % (2 * c) == 0
    ng = n // r

    @pl.when(j == 0)
    def _():
        s_scr[...] = jnp.zeros_like(s_scr)
        for p in range(npair):
            g = (2 * p) // hpg
            s_scr[:, p, g * MB_STATE:(g + 1) * MB_STATE, :] = jnp.concatenate(
                [s0_ref[:, 2 * p], s0_ref[:, 2 * p + 1]], axis=-1)
        xp_scr[:, 0:CONV_PAD, :] = c0_ref[...]

    u = _normed_input(h_ref, g_ref)
    z_scr[...] = jnp.dot(u, wz_ref[...], preferred_element_type=f32).reshape(bt, c, BRANCH_W)
    xp_scr[:, CONV_PAD:CONV_PAD + c, :] = jnp.dot(u, wx_ref[...], preferred_element_type=f32).reshape(
        bt, c, MB_CONV_DIM)
    dt = jax.nn.softplus(jnp.dot(u, wdt_ref[...], preferred_element_type=f32) + dtb_ref[...])
    cum = _dot_sel_left(_blockdiag_ltri(bt * c, c), -jnp.exp(alog_ref[...]) * dt)
    xa = _silu(_causal_conv(xp_scr, cw_ref, c) + cb_ref[...])
    tail = xp_scr[:, c:c + CONV_PAD, :]
    xp_scr[:, 0:CONV_PAD, :] = tail
    co_ref[...] = tail
    xs = xa[:, :, :BRANCH_W]
    bm = xa[:, :, BRANCH_W:BRANCH_W + LANES]
    cm = xa[:, :, BRANCH_W + LANES:BRANCH_W + 2 * LANES]
    xs_scr[...] = xs
    lane3 = lax.broadcasted_iota(jnp.int32, (bt, c, LANES), 2)
    lane_bcast = lambda col: jnp.broadcast_to(col, (bt * c, LANES)).reshape(bt, c, LANES)
    for h in range(nh):
        own_half = (lane3 >> _log2(MB_HEADDIM)) == (h % 2)
        v_scr[:, h] = jnp.where(own_half, xs[:, :, (h // 2) * LANES:(h // 2 + 1) * LANES] * lane_bcast(dt[:, h:h + 1]),
                                0.0)
        c_scr[:, h] = jnp.where((lane3 >> _log2(MB_STATE)) == (h // hpg), cm, 0.0)
        b_scr[:, h] = bm
        cum_scr[:, h] = lane_bcast(cum[:, h:h + 1])
    cf = c_scr[...].reshape(n, LANES)
    bf = b_scr[...].reshape(n, LANES)
    vf = v_scr[...].reshape(n, LANES)
    cbf = cum_scr[...].reshape(n, LANES)

    ii, jj = _iota2(r, r, 0), _iota2(r, r, 1)
    incl = ((ii >> _log2(c)) == (jj >> _log2(c))) & (ii >= jj)
    grp = lambda a, g: a[g * r:(g + 1) * r]
    dec = [jnp.exp(jnp.where(incl, _widen(grp(cbf, g), r) - grp(cbf, g).T[0:1, :], -jnp.inf)) for g in range(ng)]
    o_loc = [_dot(_dot_nt(grp(cf, g), grp(bf, g)) * dec[g], grp(vf, g)) for g in range(ng)]
    o_loc = o_loc[0] if ng == 1 else jnp.concatenate(o_loc, axis=0)

    cb3 = cbf.reshape(n // c, c, LANES)
    k_out = (vf.reshape(n // c, c, LANES) * jnp.exp(cb3[:, c - 1:c, :] - cb3)).reshape(n, LANES)
    e_in = jnp.exp(cbf)
    lo_half = _iota2(1, LANES, 1) < MB_HEADDIM
    row_lo = _iota2(2 * c, LANES, 0) < c
    own = row_lo == (_iota2(2 * c, LANES, 1) < MB_HEADDIM)
    pairs = [(b, p) for b in range(bt) for p in range(npair)]
    prow = lambda a, i: a[i * 2 * c:(i + 1) * 2 * c]
    s_old = [s_scr[b, p] for b, p in pairs]
    o_st = [jnp.where(own, _dot(prow(cf, i), s_old[i]) * prow(e_in, i), 0.0) for i in range(len(pairs))]
    for i, (b, p) in enumerate(pairs):
        cum_i = prow(cbf, i)
        e_all = jnp.exp(jnp.where(lo_half, cum_i[c - 1:c], cum_i[2 * c - 1:2 * c]))
        s_scr[b, p] = s_old[i] * e_all + _dot_tn(prow(bf, i), prow(k_out, i))
    for b in range(bt):
        ys = []
        for p in range(npair):
            o2 = prow(o_loc, b * npair + p) + o_st[b * npair + p]
            sl = slice(p * LANES, (p + 1) * LANES)
            ys.append(o2[:c] + o2[c:] + d_ref[:, sl] * xs_scr[b, :, sl])
        y = jnp.concatenate(ys, axis=-1) * _silu(z_scr[b])
        y_ref[b] = (_rms(y) * nrm_ref[...]).astype(bf16)

    @pl.when(j == pl.num_programs(1) - 1)
    def _():
        for p in range(npair):
            g = (2 * p) // hpg
            so_ref[:, 2 * p] = s_scr[:, p, g * MB_STATE:(g + 1) * MB_STATE, 0:MB_HEADDIM]
            so_ref[:, 2 * p + 1] = s_scr[:, p, g * MB_STATE:(g + 1) * MB_STATE, MB_HEADDIM:2 * MB_HEADDIM]


def _mamba(h, g2, w, s0, conv0, c, bt, slot):
    b, l, _ = h.shape
    npair = MB_HEADS // 2
    consts = [g2, w["wz"], w["wx"], w["wdt"], w["cw"], w["cb"], w["dtb"], w["alog"], w["d"], w["nrm"]]
    scratch = [
        pltpu.VMEM((bt, npair, LANES, LANES), f32),
        pltpu.VMEM((bt, CONV_PAD + c, MB_CONV_DIM), f32),
        pltpu.VMEM((bt, c, BRANCH_W), f32),
        pltpu.VMEM((bt, c, BRANCH_W), f32),
    ] + [pltpu.VMEM((bt, MB_HEADS, c, LANES), f32)] * 4
    return _mixer_call(_mamba_body, "mamba", h, consts, [s0, conv0], c, bt,
                       [s0.shape, (b, CONV_PAD, MB_CONV_DIM)], scratch, slot)


def _ret_body(h_ref, g_ref, wq_ref, wk_ref, wv_ref, wg_ref, lg_ref, cos_ref, sin_ref,
              s0_ref, y_ref, so_ref, s_scr, q_scr, k_scr, v_scr, gt_scr):
    bt, c, _ = h_ref.shape
    j = pl.program_id(1)
    kw = RET_HEADS * RET_DK
    hw = RET_DK // 2

    @pl.when(j == 0)
    def _():
        s_scr[...] = jnp.zeros_like(s_scr)
        for h in range(RET_HEADS):
            s_scr[:, h, h * hw:(h + 1) * hw, :] = s0_ref[:, h, 0:hw, :]
            s_scr[:, h, LANES + h * hw:LANES + (h + 1) * hw, :] = s0_ref[:, h, hw:2 * hw, :]

    u = _normed_input(h_ref, g_ref)
    cos = cos_ref[...]
    sin = sin_ref[...]

    def rot(x):
        x = x.reshape(bt, c, kw)
        x1, x2 = x[:, :, :LANES], x[:, :, LANES:]
        return jnp.concatenate([x1 * cos - x2 * sin, x1 * sin + x2 * cos], axis=-1)

    nh = RET_HEADS
    n = bt * nh * c
    r = min(MXU_DIM, n)
    assert n % r == 0 and r % c == 0
    ng = n // r
    q = rot(jnp.dot(u, wq_ref[...], preferred_element_type=f32))
    k = rot(jnp.dot(u, wk_ref[...], preferred_element_type=f32)) * (RET_DK ** -0.5)
    v = jnp.dot(u, wv_ref[...], preferred_element_type=f32).reshape(bt, c, BRANCH_W)
    gt_scr[...] = jnp.dot(u, wg_ref[...], preferred_element_type=f32).reshape(bt, c, BRANCH_W)
    klane = lax.broadcasted_iota(jnp.int32, (bt, c, kw), 2)
    for h in range(nh):
        own = ((klane & (LANES - 1)) >> _log2(hw)) == h
        q_scr[:, h] = jnp.where(own, q, 0.0)
        k_scr[:, h] = jnp.where(own, k, 0.0)
        v_scr[:, h] = v[:, :, h * RET_DV:(h + 1) * RET_DV]
    qf = q_scr[...].reshape(n, kw)
    kf = k_scr[...].reshape(n, kw)
    vf = v_scr[...].reshape(n, RET_DV)

    row = _iota2(n, 1, 0)
    head = (row >> _log2(c)) & (nh - 1)
    pos = (row & (c - 1)).astype(f32)
    lg_row = jnp.zeros((n, 1), f32)
    for h in range(nh):
        lg_row = jnp.where(head == h, lg_ref[:, h:h + 1], lg_row)
    q_in = qf * jnp.exp((pos + 1.0) * lg_row)
    k_out = kf * jnp.exp((float(c) - 1.0 - pos) * lg_row)
    e_all = jnp.exp(float(c) * lg_row)

    ii, jj = _iota2(r, r, 0), _iota2(r, r, 1)
    incl = ((ii >> _log2(c)) == (jj >> _log2(c))) & (ii >= jj)
    dist = (ii - jj).astype(f32)
    grp = lambda a, g: a[g * r:(g + 1) * r]
    dec = {}
    for g in range(ng):
        key = (g * r) % (nh * c)
        if key not in dec:
            dec[key] = jnp.exp(jnp.where(incl, dist * grp(lg_row, g), -jnp.inf))
    o_loc = [_dot(_dot_nt(grp(qf, g), grp(kf, g)) * dec[(g * r) % (nh * c)], grp(vf, g)) for g in range(ng)]
    o_loc = o_loc[0] if ng == 1 else jnp.concatenate(o_loc, axis=0)

    blocks = [(b, h) for b in range(bt) for h in range(nh)]
    rows = lambda a, i: a[i * c:(i + 1) * c]
    s_old = [s_scr[b, h] for b, h in blocks]
    o_st = [_dot(rows(q_in, i), s_old[i]) for i in range(len(blocks))]
    for i, (b, h) in enumerate(blocks):
        s_scr[b, h] = s_old[i] * rows(e_all, i)[0:1] + _dot_tn(rows(k_out, i), rows(vf, i))
    for b in range(bt):
        y = jnp.concatenate([_rms(rows(o_loc, b * nh + h) + o_st[b * nh + h]) for h in range(nh)], axis=-1)
        y_ref[b] = (y * _silu(gt_scr[b])).astype(bf16)

    @pl.when(j == pl.num_programs(1) - 1)
    def _():
        for h in range(RET_HEADS):
            so_ref[:, h, 0:hw, :] = s_scr[:, h, h * hw:(h + 1) * hw, :]
            so_ref[:, h, hw:2 * hw, :] = s_scr[:, h, LANES + h * hw:LANES + (h + 1) * hw, :]


def _ret(h, g2, w, cos, sin, lg, s0, c, bt, slot):
    kw = RET_HEADS * RET_DK
    consts = [g2, w["wq"], w["wk"], w["wv"], w["wg"], lg]
    scratch = [
        pltpu.VMEM((bt, RET_HEADS, kw, RET_DV), f32),
        pltpu.VMEM((bt, RET_HEADS, c, kw), f32),
        pltpu.VMEM((bt, RET_HEADS, c, kw), f32),
        pltpu.VMEM((bt, RET_HEADS, c, RET_DV), f32),
        pltpu.VMEM((bt, c, BRANCH_W), f32),
    ]
    return _mixer_call(_ret_body, "ret", h, consts, [s0], c, bt, [s0.shape], scratch, slot, chunk_ins=(cos, sin))


def _hgrn_body(h_ref, g_ref, wq_ref, wf_ref, wi_ref, wg_ref, lb_ref, nrm_ref,
               s0_ref, y_ref, so_ref, st_scr, q_scr, k_scr, v_scr, cum_scr, gt_scr):
    bt, c, _ = h_ref.shape
    j = pl.program_id(1)
    nh = HG_HEADS
    gs, ng, r = _stack_geometry(bt, c, nh)
    n = bt * nh * c
    sub = HG_SUB

    @pl.when(j == 0)
    def _():
        def init(b, carry):
            for h in range(nh):
                st_scr[b, h] = s0_ref[b, h].T
            return carry
        lax.fori_loop(0, bt, init, 0)

    u = _normed_input(h_ref, g_ref)
    q = _silu(jnp.dot(u, wq_ref[...], preferred_element_type=f32))
    lb = lb_ref[...]
    fg = lb + (1.0 - lb) * jax.nn.sigmoid(jnp.dot(u, wf_ref[...], preferred_element_type=f32))
    v = jnp.dot(u, wi_ref[...], preferred_element_type=f32)
    gt_scr[...] = jnp.dot(u, wg_ref[...], preferred_element_type=f32).reshape(bt, c, BRANCH_W)
    cum = _dot_sel_left(_blockdiag_ltri(bt * c, c), jnp.log(fg))
    k = 1.0 - fg
    for h in range(nh):
        sl = slice(h * HG_DK, (h + 1) * HG_DK)
        q_scr[:, h] = q[:, sl].reshape(bt, c, HG_DK)
        k_scr[:, h] = k[:, sl].reshape(bt, c, HG_DK)
        v_scr[:, h] = v[:, sl].reshape(bt, c, HG_DV)
        cum_scr[:, h] = cum[:, sl].reshape(bt, c, HG_DK)
    qf = q_scr[...].reshape(n, HG_DK)
    kf = k_scr[...].reshape(n, HG_DK)
    vf = v_scr[...].reshape(n, HG_DV)
    cf = cum_scr[...].reshape(n, HG_DK)

    ii, jj = _iota2(r, r, 0), _iota2(r, r, 1)
    scores = [jnp.zeros((r, r), f32) for _ in range(ng)]

    q3, k3, c3 = (a.reshape(n // sub, sub, HG_DK) for a in (qf, kf, cf))
    sub_i = lax.broadcasted_iota(jnp.int32, q3.shape, 1)
    ones = jnp.ones((HG_DK, r), bf16)
    same_sub = (ii >> _log2(sub)) == (jj >> _log2(sub))
    for t in range(sub):
        pt = jnp.where(sub_i >= t, q3 * k3[:, t:t + 1, :] * jnp.exp(c3 - c3[:, t:t + 1, :]), 0.0)
        row_sum = jnp.dot(pt.reshape(n, HG_DK).astype(bf16), ones, preferred_element_type=f32)
        place = same_sub & ((jj & (sub - 1)) == t)
        scores = [jnp.where(place, row_sum[g * r:(g + 1) * r], scores[g]) for g in range(ng)]

    m = sub
    while m < c:
        q4, k4, c4 = (a.reshape(n // (2 * m), 2, m, HG_DK) for a in (qf, kf, cf))
        bnd = c4[:, 0:1, m - 1:m, :]
        odd = lax.broadcasted_iota(jnp.int32, q4.shape, 1) == 1
        e = jnp.exp(jnp.where(odd, c4 - bnd, bnd - c4))
        q_t = jnp.where(odd, q4 * e, 0.0).reshape(n, HG_DK)
        k_t = jnp.where(odd, 0.0, k4 * e).reshape(n, HG_DK)
        same_pair = (ii >> _log2(2 * m)) == (jj >> _log2(2 * m))
        scores = [scores[g] + jnp.where(same_pair, _dot_nt(q_t[g * r:(g + 1) * r], k_t[g * r:(g + 1) * r]), 0.0)
                  for g in range(ng)]
        m *= 2

    cb3 = cf.reshape(n // c, c, HG_DK)
    clast3 = cb3[:, c - 1:c, :]
    q_in = qf * jnp.exp(cf)
    k_out = (kf.reshape(n // c, c, HG_DK) * jnp.exp(clast3 - cb3)).reshape(n, HG_DK)
    e_all = jnp.exp(clast3)
    blocks = [(b, h) for b in range(bt) for h in range(nh)]
    rows = lambda a, i: a[i * c:(i + 1) * c]
    st_old = [st_scr[b, h] for b, h in blocks]
    o_state = [_dot_nt(rows(q_in, i), st_old[i]) for i in range(len(blocks))]
    for i, (b, h) in enumerate(blocks):
        st_scr[b, h] = st_old[i] * e_all[i] + _dot_tn(rows(vf, i), rows(k_out, i))
    for g in range(ng):
        o = _dot(scores[g], vf[g * r:(g + 1) * r])
        for s in range(gs):
            b = g * gs + s
            y = jnp.concatenate([_rms(rows(o, s * nh + h) + o_state[b * nh + h]) * nrm_ref[...] for h in range(nh)],
                                axis=-1)
            y_ref[b] = (y * _silu(gt_scr[b])).astype(bf16)

    @pl.when(j == pl.num_programs(1) - 1)
    def _():
        def fin(b, carry):
            for h in range(nh):
                so_ref[b, h] = st_scr[b, h].T
            return carry
        lax.fori_loop(0, bt, fin, 0)


def _hgrn(h, g2, w, s0, c, bt, slot):
    consts = [g2, w["wq"], w["wf"], w["wi"], w["wg"], w["lb"], w["nrm"]]
    scratch = ([pltpu.VMEM((bt, HG_HEADS, HG_DV, HG_DK), f32)] + [pltpu.VMEM((bt, HG_HEADS, c, HG_DK), f32)] * 4
               + [pltpu.VMEM((bt, c, BRANCH_W), f32)])
    return _mixer_call(_hgrn_body, "hgrn", h, consts, [s0], c, bt, [s0.shape], scratch, slot)


def _unit_lower_inverses(a_list, r, c):
    ii, jj = _iota2(r, r, 0), _iota2(r, r, 1)
    eye = jnp.where(ii == jj, 1.0, 0.0)
    t_list = [eye for _ in a_list]
    m = 1
    while m < c:
        lm = _log2(m)
        join = ((ii >> (lm + 1)) == (jj >> (lm + 1))) & (((ii >> lm) & 1) == 1) & (((jj >> lm) & 1) == 0)
        f_list = [jnp.where(join, a, 0.0) for a in a_list]
        if m == 1:
            t_list = [eye - f for f in f_list]
        else:
            ft_list = [_dot(f, t) for f, t in zip(f_list, t_list)]
            t_list = [t - _dot(t, ft) for t, ft in zip(t_list, ft_list)]
        m *= 2
    return t_list


def _gdn_body(h_ref, g_ref, wqkv_ref, wz_ref, wab_ref, cw_ref, dtb_ref, alog_ref, nrm_ref,
              s0_ref, c0_ref, y_ref, so_ref, co_ref, s_scr, xp_scr, z_scr, q_scr, k_scr, v_scr, cum_scr, beta_scr):
    bt, c, _ = h_ref.shape
    j = pl.program_id(1)
    nh = GDN_HEADS
    hk = nh * GDN_DK
    gs, ng, r = _stack_geometry(bt, c, nh)

    @pl.when(j == 0)
    def _():
        s_scr[...] = s0_ref[...]
        xp_scr[:, 0:CONV_PAD, :] = c0_ref[...]

    u = _normed_input(h_ref, g_ref)
    xp_scr[:, CONV_PAD:CONV_PAD + c, :] = jnp.dot(u, wqkv_ref[...], preferred_element_type=f32).reshape(
        bt, c, GDN_CONV_DIM)
    z_scr[...] = jnp.dot(u, wz_ref[...], preferred_element_type=f32).reshape(bt, c, BRANCH_W)
    ab = jnp.dot(u, wab_ref[...], preferred_element_type=f32)
    log_g = -jnp.exp(alog_ref[...]) * jax.nn.softplus(ab + dtb_ref[...])
    cum = _dot_sel_left(_blockdiag_ltri(bt * c, c), log_g)
    beta = jax.nn.sigmoid(ab)
    qkv = _silu(_causal_conv(xp_scr, cw_ref, c))
    tail = xp_scr[:, c:c + CONV_PAD, :]
    xp_scr[:, 0:CONV_PAD, :] = tail
    co_ref[...] = tail
    lane_bcast = lambda col: jnp.broadcast_to(col, (bt * c, LANES)).reshape(bt, c, LANES)
    for h in range(nh):
        q = qkv[:, :, h * GDN_DK:(h + 1) * GDN_DK]
        k = qkv[:, :, hk + h * GDN_DK:hk + (h + 1) * GDN_DK]
        q_scr[:, h] = q * lax.rsqrt(jnp.sum(q * q, axis=-1, keepdims=True) + EPS) * (GDN_DK ** -0.5)
        k_scr[:, h] = k * lax.rsqrt(jnp.sum(k * k, axis=-1, keepdims=True) + EPS)
        v_scr[:, h] = qkv[:, :, 2 * hk + h * GDN_DV:2 * hk + (h + 1) * GDN_DV]
        cum_scr[:, h] = lane_bcast(cum[:, h:h + 1])
        beta_scr[:, h] = lane_bcast(beta[:, nh + h:nh + h + 1])

    grp = lambda ref, g: ref[g * gs:(g + 1) * gs].reshape(r, LANES)
    groups = range(ng)
    qs = [grp(q_scr, g) for g in groups]
    ks = [grp(k_scr, g) for g in groups]
    vs = [grp(v_scr, g) for g in groups]
    cb = [grp(cum_scr, g) for g in groups]
    bb = [grp(beta_scr, g) for g in groups]
    ii, jj = _iota2(r, r, 0), _iota2(r, r, 1)
    same = (ii >> _log2(c)) == (jj >> _log2(c))
    incl = same & (ii >= jj)
    strict = same & (ii > jj)
    dec = [jnp.exp(jnp.where(incl, _widen(cb[g], r) - cb[g].T[0:1, :], -jnp.inf)) for g in groups]
    a_mat = [jnp.where(strict, _widen(bb[g], r) * dec[g] * _dot_nt(ks[g], ks[g]), 0.0) for g in groups]
    t_inv = _unit_lower_inverses(a_mat, r, c)
    e_in = [jnp.exp(cb[g]) for g in groups]
    sol = [_dot(t_inv[g], jnp.concatenate([vs[g] * bb[g], ks[g] * (e_in[g] * bb[g])], axis=-1)) for g in groups]
    scd = [_dot_nt(qs[g], ks[g]) * dec[g] for g in groups]
    q_in = [qs[g] * e_in[g] for g in groups]

    rows = lambda a, i: a[i * c:(i + 1) * c]
    for g in groups:
        blocks = [(g * gs + s, h) for s in range(gs) for h in range(nh)]
        nb = len(blocks)
        s_old = [s_scr[b, h] for b, h in blocks]
        proj = [_dot(jnp.concatenate([rows(sol[g], i)[:, GDN_DV:], rows(q_in[g], i)], axis=0), s_old[i])
                for i in range(nb)]
        w_new = [rows(sol[g], i)[:, :GDN_DV] - proj[i][:c] for i in range(nb)]
        for i, (b, h) in enumerate(blocks):
            cum_i = rows(cb[g], i)
            clast = cum_i[c - 1:c]
            s_scr[b, h] = s_old[i] * jnp.exp(clast) + _dot_tn(rows(ks[g], i) * jnp.exp(clast - cum_i), w_new[i])
        o = _dot(scd[g], jnp.concatenate(w_new, axis=0)) + jnp.concatenate([p[c:] for p in proj], axis=0)
        for s in range(gs):
            b = g * gs + s
            y = jnp.concatenate([_rms(rows(o, s * nh + h)) * nrm_ref[...] for h in range(nh)], axis=-1)
            y_ref[b] = (y * _silu(z_scr[b])).astype(bf16)

    @pl.when(j == pl.num_programs(1) - 1)
    def _():
        so_ref[...] = s_scr[...]


def _gdn(h, g2, w, s0, conv0, c, bt, slot):
    b, l, _ = h.shape
    consts = [g2, w["wqkv"], w["wz"], w["wab"], w["cw"], w["dtb"], w["alog"], w["nrm"]]
    scratch = [
        pltpu.VMEM((bt, GDN_HEADS, GDN_DK, GDN_DV), f32),
        pltpu.VMEM((bt, CONV_PAD + c, GDN_CONV_DIM), f32),
        pltpu.VMEM((bt, c, BRANCH_W), f32),
    ] + [pltpu.VMEM((bt, GDN_HEADS, c, LANES), f32)] * 5
    return _mixer_call(_gdn_body, "gdn", h, consts, [s0, conv0], c, bt,
                       [s0.shape, (b, CONV_PAD, GDN_CONV_DIM)], scratch, slot)


def _pad_lanes(v, offset=0):
    row = jnp.zeros((1, LANES), f32)
    return row.at[0, offset:offset + v.shape[0]].set(v.astype(f32))


def _layer_weights(l, norm_g, ffn_w_in, ffn_w_out, w_in, w_branch, w_out, mb_conv_w, mb_conv_b, mb_a_log,
                   mb_dt_bias, mb_d, mb_norm, lb_all, hg_norm, gdn_conv_w, gdn_a_log, gdn_dt_bias, gdn_norm):
    wi = w_in[l]
    col = lambda n: wi[:, IN_OFFSETS[n]:IN_OFFSETS[n + 1]]
    colb = lambda n: col(n).astype(bf16)
    pad_cols = lambda a: jnp.pad(a, ((0, 0), (0, LANES - a.shape[1]))).astype(bf16)
    hw = RET_DK // 2
    perm = np.arange(RET_HEADS * RET_DK).reshape(RET_HEADS, 2, hw).transpose(1, 0, 2).reshape(-1)
    row = lambda v: v.astype(f32).reshape(1, -1)
    return {
        "g_ffn1": norm_g[l, 0:2], "g_mix": norm_g[l, 2:4], "g_ffn2": norm_g[l, 4:6], "g_u": norm_g[l, 2:3],
        "layer": l, "ffn_in": ffn_w_in, "ffn_out": ffn_w_out,
        "mamba": {"wz": colb(0), "wx": colb(1), "wdt": pad_cols(col(2)), "cw": mb_conv_w[l], "cb": row(mb_conv_b[l]),
                  "dtb": _pad_lanes(mb_dt_bias[l]), "alog": _pad_lanes(mb_a_log[l]),
                  "d": row(jnp.repeat(mb_d[l], MB_HEADDIM)), "nrm": row(mb_norm[l])},
        "hgrn": {"wq": colb(3), "wf": colb(4), "wi": colb(5), "wg": colb(6), "lb": row(lb_all[l]),
                 "nrm": row(hg_norm[l])},
        "ret": {"wq": col(7)[:, perm].astype(bf16), "wk": col(8)[:, perm].astype(bf16), "wv": colb(9), "wg": colb(10)},
        "gdn": {"wqkv": colb(11), "wz": colb(12), "wab": pad_cols(jnp.concatenate([col(13), col(14)], axis=1)),
                "cw": gdn_conv_w[l], "dtb": _pad_lanes(gdn_dt_bias[l]), "alog": _pad_lanes(gdn_a_log[l]),
                "nrm": row(gdn_norm[l])},
        "w_gate": colb(15), "w_branch": w_branch[l].astype(bf16), "w_out": w_out[l].astype(bf16),
    }


def _rope_tables(pos):
    half = RET_DK // 2
    inv_freq = 1.0 / (ROPE_BASE ** jnp.linspace(0.0, 1.0, half, dtype=f32))
    ang = pos.astype(f32)[:, None] * inv_freq[None, :]
    reps = LANES // half
    return jnp.tile(jnp.cos(ang), (1, reps)), jnp.tile(jnp.sin(ang), (1, reps))


def _pad_conv(buf):
    return jnp.pad(buf, ((0, 0), (CONV_PAD - (CONV_W - 1), 0), (0, 0)))


def _layer(h, w, states, rope, lg, chunks, bt, depth, stacked):
    b, l, d = h.shape
    s_ssm, s_ssm_conv, s_hg, s_ret, s_gdn, s_gdn_conv = states
    slot = lambda n: (w["layer"], depth, None if stacked is None else stacked[n])
    h = _ffn(h.reshape(b * l, d), w["g_ffn1"], w["ffn_in"], w["ffn_out"], w["layer"], 0).reshape(b, l, d)
    y_mb, n_ssm, n_ssm_conv = _mamba(h, w["g_u"], w["mamba"], s_ssm, _pad_conv(s_ssm_conv),
                                     chunks["mamba"], bt, slot(0))
    y_hg, n_hg = _hgrn(h, w["g_u"], w["hgrn"], s_hg, chunks["hgrn"], bt, slot(1))
    y_rt, n_ret = _ret(h, w["g_u"], w["ret"], rope[0], rope[1], lg, s_ret, chunks["ret"], bt, slot(2))
    y_gd, n_gdn, n_gdn_conv = _gdn(h, w["g_u"], w["gdn"], s_gdn, _pad_conv(s_gdn_conv), chunks["gdn"], bt, slot(3))
    ys = [y.reshape(b * l, BRANCH_W) for y in (y_mb, y_hg, y_rt, y_gd)]
    hf = _merge(h.reshape(b * l, d), w["g_mix"], ys, w["w_gate"], w["w_branch"], w["w_out"])
    hf = _ffn(hf, w["g_ffn2"], w["ffn_in"], w["ffn_out"], w["layer"], 1)
    convs = (n_ssm_conv[:, CONV_PAD - (CONV_W - 1):], n_gdn_conv[:, CONV_PAD - (CONV_W - 1):])
    return hf.reshape(b, l, d), (n_ssm, n_hg, n_ret, n_gdn), convs


def _chunks(l):
    pick = lambda pref: pref if l % pref == 0 else math.gcd(l, pref)
    return {"mamba": pick(128), "ret": pick(128), "hgrn": pick(64), "gdn": pick(64)}


def kernel(x_prompt, x_sample, state_ssm, state_ssm_conv, state_hgrn, state_ret, state_gdn, state_gdn_conv,
           norm_g, ffn_w_in, ffn_w_out, w_in, w_branch, w_out, mb_conv_w, mb_conv_b, mb_a_log, mb_dt_bias, mb_d,
           mb_norm, hg_lb_logits, hg_norm, gdn_conv_w, gdn_a_log, gdn_dt_bias, gdn_norm):
    depth = norm_g.shape[0]
    lb_all = jnp.cumsum(jax.nn.softmax(hg_lb_logits.astype(f32), axis=0), axis=0)
    lb_all = lb_all - lb_all[0:1]
    lg = _pad_lanes(jnp.log(1.0 - jnp.exp2(-5.0 - jnp.arange(RET_HEADS, dtype=f32))))

    bp, lp, _ = x_prompt.shape
    bs, ls, _ = x_sample.shape
    rope_p = _rope_tables(jnp.arange(lp))
    rope_s = _rope_tables(PAST_LEN + jnp.arange(ls))
    zero_states = (
        jnp.zeros((bp, MB_HEADS, MB_STATE, MB_HEADDIM), f32),
        jnp.zeros((bp, CONV_W - 1, MB_CONV_DIM), f32),
        jnp.zeros((bp, HG_HEADS, HG_DK, HG_DV), f32),
        jnp.zeros((bp, RET_HEADS, RET_DK, RET_DV), f32),
        jnp.zeros((bp, GDN_HEADS, GDN_DK, GDN_DV), f32),
        jnp.zeros((bp, CONV_W - 1, GDN_CONV_DIM), f32),
    )
    bt_p = math.gcd(bp, 8)
    bt_s = math.gcd(bs, 16)

    ffn_w_in = ffn_w_in.astype(bf16)
    ffn_w_out = ffn_w_out.astype(bf16)
    hp, hs = x_prompt, x_sample
    st_p = st_s = None
    cv_p, cv_s = [], []
    for l in range(depth):
        w = _layer_weights(l, norm_g, ffn_w_in, ffn_w_out, w_in, w_branch, w_out, mb_conv_w, mb_conv_b,
                           mb_a_log, mb_dt_bias, mb_d, mb_norm, lb_all, hg_norm, gdn_conv_w, gdn_a_log,
                           gdn_dt_bias, gdn_norm)
        hp, st_p, cp = _layer(hp, w, zero_states, rope_p, lg, _chunks(lp), bt_p, depth, st_p)
        past = (state_ssm[l], state_ssm_conv[l], state_hgrn[l], state_ret[l], state_gdn[l], state_gdn_conv[l])
        hs, st_s, cs = _layer(hs, w, past, rope_s, lg, _chunks(ls), bt_s, depth, st_s)
        cv_p.append(cp)
        cv_s.append(cs)

    stack = lambda cvs, i: jnp.stack([cv[i] for cv in cvs], axis=0)
    return (hp, hs,
            st_p[0], stack(cv_p, 0), st_p[1], st_p[2], st_p[3], stack(cv_p, 1),
            st_s[0], stack(cv_s, 0), st_s[1], st_s[2], st_s[3], stack(cv_s, 1))
```

```python
import functools
import math

import numpy as np
import jax
import jax.numpy as jnp
from jax import lax
from jax.experimental import pallas as pl
from jax.experimental.pallas import tpu as pltpu

f32 = jnp.float32
bf16 = jnp.bfloat16

D_MODEL = 1024
D_FF = 2816
BRANCH_W = 512
N_BRANCH = 4
CONV_W = 4
EPS = 1e-6
ROPE_BASE = 10000.0
PAST_LEN = 16384

MB_HEADS, MB_HEADDIM, MB_STATE, MB_GROUPS = 8, 64, 64, 2
MB_CONV_DIM = BRANCH_W + 2 * MB_GROUPS * MB_STATE
HG_HEADS, HG_DK, HG_DV = 4, 128, 128
RET_HEADS, RET_DK, RET_DV = 4, 64, 128
GDN_HEADS, GDN_DK, GDN_DV = 4, 128, 128
GDN_CONV_DIM = 2 * GDN_HEADS * GDN_DK + GDN_HEADS * GDN_DV

IN_SPLITS = (
    BRANCH_W, MB_CONV_DIM, MB_HEADS,
    HG_HEADS * HG_DK, HG_HEADS * HG_DK, BRANCH_W, BRANCH_W,
    RET_HEADS * RET_DK, RET_HEADS * RET_DK, BRANCH_W, BRANCH_W,
    GDN_CONV_DIM, BRANCH_W, GDN_HEADS, GDN_HEADS,
    N_BRANCH * D_MODEL,
)
IN_OFFSETS = tuple(int(v) for v in np.cumsum((0,) + IN_SPLITS))

LANES = 128
SUBLANES = 8
MXU_DIM = 256
STACK_ROWS = 128
CONV_PAD = SUBLANES
VMEM_LIMIT = 56 * 1024 * 1024
FFN_TM = 512
MERGE_TM = 512
HG_SUB = SUBLANES


def _dot(a, b):
    return jnp.dot(a.astype(bf16), b.astype(bf16), preferred_element_type=f32)


def _dot_nt(a, b):
    return lax.dot_general(a.astype(bf16), b.astype(bf16), (((1,), (1,)), ((), ())), preferred_element_type=f32)


def _dot_tn(a, b):
    return lax.dot_general(a.astype(bf16), b.astype(bf16), (((0,), (0,)), ((), ())), preferred_element_type=f32)


def _split3(x):
    hi = x.astype(bf16)
    r = x - hi.astype(f32)
    mid = r.astype(bf16)
    lo = (r - mid.astype(f32)).astype(bf16)
    return hi, mid, lo


def _dot_sel_left(m01, x):
    hi, mid, lo = _split3(x)
    return (jnp.dot(m01, hi, preferred_element_type=f32) + jnp.dot(m01, mid, preferred_element_type=f32)
            + jnp.dot(m01, lo, preferred_element_type=f32))


def _rms(x):
    return x * lax.rsqrt(jnp.mean(x * x, axis=-1, keepdims=True) + EPS)


def _silu(x):
    return x * jax.nn.sigmoid(x)


def _iota2(n, m, axis):
    return lax.broadcasted_iota(jnp.int32, (n, m), axis)


def _log2(n):
    assert n > 0 and n & (n - 1) == 0, n
    return n.bit_length() - 1


def _ltri(c):
    return jnp.where(_iota2(c, c, 0) >= _iota2(c, c, 1), 1.0, 0.0).astype(bf16)


def _blockdiag_ltri(n, c):
    ii, jj = _iota2(n, n, 0), _iota2(n, n, 1)
    lc = _log2(c)
    return jnp.where(((ii >> lc) == (jj >> lc)) & (ii >= jj), 1.0, 0.0).astype(bf16)


def _row_bcast(col, c):
    dg = jnp.where(_iota2(c, c, 0) == _iota2(c, c, 1), jnp.broadcast_to(col, (c, c)), 0.0)
    return _dot_sel_left(jnp.ones((c, c), bf16), dg)


def _decay_matrix(col, c):
    diff = col - _row_bcast(col, c)
    return jnp.exp(jnp.where(_iota2(c, c, 0) >= _iota2(c, c, 1), diff, -jnp.inf))


def _causal_conv(xp_ref, w_ref, c):
    y = xp_ref[:, pl.ds(CONV_PAD - 3, c), :] * w_ref[0:1, :]
    for j in range(1, CONV_W):
        y = y + xp_ref[:, pl.ds(CONV_PAD - 3 + j, c), :] * w_ref[j:j + 1, :]
    return y


def _stack_geometry(bt, c, heads):
    n = bt * heads * c
    r = min(STACK_ROWS, n)
    assert n % r == 0 and r % c == 0 and r % LANES == 0
    return n, r, n // r


def _group_rows(ref, g, r):
    _, heads, c, w = ref.shape
    bpg = r // c
    if bpg <= heads:
        assert heads % bpg == 0
        b, h0 = divmod(g * bpg, heads)
        return ref[b, h0:h0 + bpg].reshape(r, w)
    assert bpg % heads == 0
    nseq = bpg // heads
    return ref[g * nseq:(g + 1) * nseq].reshape(r, w)


def _widen(x, r):
    return x if r == LANES else jnp.concatenate([x] * (r // LANES), axis=-1)


def _ffn_body(x_ref, g_ref, wg_ref, wu_ref, wo_ref, o_ref):
    x = x_ref[...]
    xn = (_rms(x) * g_ref[0:1, :]).astype(bf16)
    gate = jnp.dot(xn, wg_ref[...], preferred_element_type=f32)
    up = jnp.dot(xn, wu_ref[...], preferred_element_type=f32)
    act = (_silu(gate) * up).astype(bf16)
    y = jnp.dot(act, wo_ref[...], preferred_element_type=f32)
    o_ref[...] = x + 0.5 * (_rms(y) * g_ref[1:2, :])


def _ffn(x, g2, w_in, w_out, l, k):
    t = x.shape[0]
    tm = min(FFN_TM, t)
    resident = pl.Buffered(1)
    return pl.pallas_call(
        _ffn_body,
        grid=(t // tm,),
        in_specs=[
            pl.BlockSpec((tm, D_MODEL), lambda i: (i, 0)),
            pl.BlockSpec((2, D_MODEL), lambda i: (0, 0)),
            pl.BlockSpec((None, None, D_MODEL, D_FF), lambda i: (l, k, 0, 0), pipeline_mode=resident),
            pl.BlockSpec((None, None, D_MODEL, D_FF), lambda i: (l, k, 0, 1), pipeline_mode=resident),
            pl.BlockSpec((None, None, D_FF, D_MODEL), lambda i: (l, k, 0, 0), pipeline_mode=resident),
        ],
        out_specs=pl.BlockSpec((tm, D_MODEL), lambda i: (i, 0)),
        out_shape=jax.ShapeDtypeStruct((t, D_MODEL), f32),
        compiler_params=pltpu.CompilerParams(dimension_semantics=("arbitrary",), vmem_limit_bytes=VMEM_LIMIT),
        name="ffn",
    )(x, g2, w_in, w_in, w_out)


def _merge_body(h_ref, g_ref, y0_ref, y1_ref, y2_ref, y3_ref, wg_ref, wb_ref, wo_ref, o_ref):
    h = h_ref[...]
    u = (_rms(h) * g_ref[0:1, :]).astype(bf16)
    merged = jnp.zeros(h.shape, f32)
    for n, y_ref in enumerate((y0_ref, y1_ref, y2_ref, y3_ref)):
        gate = jax.nn.sigmoid(jnp.dot(u, wg_ref[:, n * D_MODEL:(n + 1) * D_MODEL], preferred_element_type=f32))
        merged = merged + gate * jnp.dot(y_ref[...], wb_ref[n], preferred_element_type=f32)
    mixed = jnp.dot(merged.astype(bf16), wo_ref[...], preferred_element_type=f32)
    o_ref[...] = h + _rms(mixed) * g_ref[1:2, :]


def _merge(h, g2, ys, w_gate, w_branch, w_out):
    t = h.shape[0]
    tm = min(MERGE_TM, t)
    row = lambda i: (i, 0)
    return pl.pallas_call(
        _merge_body,
        grid=(t // tm,),
        in_specs=[
            pl.BlockSpec((tm, D_MODEL), row),
            pl.BlockSpec((2, D_MODEL), lambda i: (0, 0)),
            pl.BlockSpec((tm, BRANCH_W), row), pl.BlockSpec((tm, BRANCH_W), row),
            pl.BlockSpec((tm, BRANCH_W), row), pl.BlockSpec((tm, BRANCH_W), row),
            pl.BlockSpec((D_MODEL, N_BRANCH * D_MODEL), lambda i: (0, 0)),
            pl.BlockSpec((N_BRANCH, BRANCH_W, D_MODEL), lambda i: (0, 0, 0)),
            pl.BlockSpec((D_MODEL, D_MODEL), lambda i: (0, 0)),
        ],
        out_specs=pl.BlockSpec((tm, D_MODEL), row),
        out_shape=jax.ShapeDtypeStruct((t, D_MODEL), f32),
        compiler_params=pltpu.CompilerParams(dimension_semantics=("arbitrary",), vmem_limit_bytes=VMEM_LIMIT),
        name="merge",
    )(h, g2, *ys, w_gate, w_branch, w_out)


def _normed_input(h_ref, g_ref):
    bt, c, d = h_ref.shape
    return (_rms(h_ref[...].reshape(bt * c, d)) * g_ref[...]).astype(bf16)


def _const_spec(a):
    nd = a.ndim
    return pl.BlockSpec(a.shape, lambda i, j, _nd=nd: (0,) * _nd)


def _batch_spec(bt, a):
    nd = a.ndim
    return pl.BlockSpec((bt,) + a.shape[1:], lambda i, j, _nd=nd: (i,) + (0,) * (_nd - 1))


def _mixer_call(body, name, h, consts, batch_ins, c, bt, batch_outs, scratch, slot, chunk_ins=()):
    b, l, d = h.shape
    layer, depth, prev = slot
    seq = lambda i, j: (i, j, 0)
    st_in, st_entry = batch_ins[0]
    st = tuple(st_in.shape[1:])
    st_block = (None, bt) + st[1:]
    out_shape = ([jax.ShapeDtypeStruct((b, l, BRANCH_W), bf16), jax.ShapeDtypeStruct((depth,) + st, f32)]
                 + [jax.ShapeDtypeStruct(s, f32) for s in batch_outs])
    out_specs = ([pl.BlockSpec((bt, c, BRANCH_W), seq),
                  pl.BlockSpec(st_block, lambda i, j: (layer, i) + (0,) * (len(st) - 1))]
                 + [pl.BlockSpec((bt,) + s[1:], lambda i, j, _nd=len(s): (i,) + (0,) * (_nd - 1))
                    for s in batch_outs])
    operands = [h, *consts, *chunk_ins, st_in, *batch_ins[1:]]
    in_specs = ([pl.BlockSpec((bt, c, d), seq)] + [_const_spec(a) for a in consts]
                + [pl.BlockSpec((c, a.shape[1]), lambda i, j: (j, 0)) for a in chunk_ins]
                + [pl.BlockSpec(st_block, lambda i, j: (st_entry, i) + (0,) * (len(st) - 1))]
                + [_batch_spec(bt, a) for a in batch_ins[1:]])
    n_in = len(operands)
    aliases = {}
    fn = body
    if prev is not None:
        operands.append(prev)
        in_specs.append(pl.BlockSpec(memory_space=pl.ANY))
        aliases = {n_in: 1}
        fn = lambda *refs: body(*refs[:n_in], *refs[n_in + 1:])
    return pl.pallas_call(
        fn,
        grid=(b // bt, l // c),
        in_specs=in_specs,
        out_specs=out_specs,
        out_shape=out_shape,
        scratch_shapes=scratch,
        input_output_aliases=aliases,
        compiler_params=pltpu.CompilerParams(dimension_semantics=("arbitrary", "arbitrary"),
                                             vmem_limit_bytes=VMEM_LIMIT),
        name=name,
    )(*operands)


def _mamba_body(h_ref, g_ref, wz_ref, wx_ref, wdt_ref, cw_ref, cb_ref, dtb_ref, alog_ref, d_ref, nrm_ref,
                s0_ref, c0_ref, y_ref, so_ref, co_ref, s_scr, xp_scr, z_scr, xs_scr, c_scr, b_scr, v_scr, cum_scr):
    bt, c, _ = h_ref.shape
    j = pl.program_id(1)
    nh = MB_HEADS
    npair = nh // 2
    hpg = nh // MB_GROUPS
    n = bt * nh * c
    r = min(MXU_DIM, n)
    assert n % r == 0 and r % c == 0
    ng = n // r

    @pl.when(j == 0)
    def _():
        s_scr[...] = jnp.zeros_like(s_scr)
        for p in range(npair):
            g = (2 * p) // hpg
            s_scr[:, p, g * MB_STATE:(g + 1) * MB_STATE, :] = jnp.concatenate(
                [s0_ref[:, 2 * p], s0_ref[:, 2 * p + 1]], axis=-1)
        xp_scr[:, 0:CONV_PAD, :] = c0_ref[...]

    u = _normed_input(h_ref, g_ref)
    z_scr[...] = jnp.dot(u, wz_ref[...], preferred_element_type=f32).reshape(bt, c, BRANCH_W)
    xp_scr[:, CONV_PAD:CONV_PAD + c, :] = jnp.dot(u, wx_ref[...], preferred_element_type=f32).reshape(
        bt, c, MB_CONV_DIM)
    dt = jax.nn.softplus(jnp.dot(u, wdt_ref[...], preferred_element_type=f32) + dtb_ref[...])
    cum = _dot_sel_left(_blockdiag_ltri(bt * c, c), -jnp.exp(alog_ref[...]) * dt)
    xa = _silu(_causal_conv(xp_scr, cw_ref, c) + cb_ref[...])
    tail = xp_scr[:, c:c + CONV_PAD, :]
    xp_scr[:, 0:CONV_PAD, :] = tail
    co_ref[...] = tail
    xs = xa[:, :, :BRANCH_W]
    bm = xa[:, :, BRANCH_W:BRANCH_W + LANES]
    cm = xa[:, :, BRANCH_W + LANES:BRANCH_W + 2 * LANES]
    xs_scr[...] = xs
    lane3 = lax.broadcasted_iota(jnp.int32, (bt, c, LANES), 2)
    lane_bcast = lambda col: jnp.broadcast_to(col, (bt * c, LANES)).reshape(bt, c, LANES)
    for h in range(nh):
        own_half = (lane3 >> _log2(MB_HEADDIM)) == (h % 2)
        v_scr[:, h] = jnp.where(own_half, xs[:, :, (h // 2) * LANES:(h // 2 + 1) * LANES] * lane_bcast(dt[:, h:h + 1]),
                                0.0)
        c_scr[:, h] = jnp.where((lane3 >> _log2(MB_STATE)) == (h // hpg), cm, 0.0)
        b_scr[:, h] = bm
        cum_scr[:, h] = lane_bcast(cum[:, h:h + 1])
    cf = c_scr[...].reshape(n, LANES)
    bf = b_scr[...].reshape(n, LANES)
    vf = v_scr[...].reshape(n, LANES)
    cbf = cum_scr[...].reshape(n, LANES)

    ii, jj = _iota2(r, r, 0), _iota2(r, r, 1)
    incl = ((ii >> _log2(c)) == (jj >> _log2(c))) & (ii >= jj)
    grp = lambda a, g: a[g * r:(g + 1) * r]
    dec = [jnp.exp(jnp.where(incl, _widen(grp(cbf, g), r) - grp(cbf, g).T[0:1, :], -jnp.inf)) for g in range(ng)]
    o_loc = [_dot(_dot_nt(grp(cf, g), grp(bf, g)) * dec[g], grp(vf, g)) for g in range(ng)]
    o_loc = o_loc[0] if ng == 1 else jnp.concatenate(o_loc, axis=0)

    cb3 = cbf.reshape(n // c, c, LANES)
    k_out = (vf.reshape(n // c, c, LANES) * jnp.exp(cb3[:, c - 1:c, :] - cb3)).reshape(n, LANES)
    e_in = jnp.exp(cbf)
    lo_half = _iota2(1, LANES, 1) < MB_HEADDIM
    row_lo = _iota2(2 * c, LANES, 0) < c
    own = row_lo == (_iota2(2 * c, LANES, 1) < MB_HEADDIM)
    pairs = [(b, p) for b in range(bt) for p in range(npair)]
    prow = lambda a, i: a[i * 2 * c:(i + 1) * 2 * c]
    s_old = [s_scr[b, p] for b, p in pairs]
    o_st = [jnp.where(own, _dot(prow(cf, i), s_old[i]) * prow(e_in, i), 0.0) for i in range(len(pairs))]
    for i, (b, p) in enumerate(pairs):
        cum_i = prow(cbf, i)
        e_all = jnp.exp(jnp.where(lo_half, cum_i[c - 1:c], cum_i[2 * c - 1:2 * c]))
        s_scr[b, p] = s_old[i] * e_all + _dot_tn(prow(bf, i), prow(k_out, i))
    for b in range(bt):
        ys = []
        for p in range(npair):
            o2 = prow(o_loc, b * npair + p) + o_st[b * npair + p]
            sl = slice(p * LANES, (p + 1) * LANES)
            ys.append(o2[:c] + o2[c:] + d_ref[:, sl] * xs_scr[b, :, sl])
        y = jnp.concatenate(ys, axis=-1) * _silu(z_scr[b])
        y_ref[b] = (_rms(y) * nrm_ref[...]).astype(bf16)

    @pl.when(j == pl.num_programs(1) - 1)
    def _():
        for p in range(npair):
            g = (2 * p) // hpg
            so_ref[:, 2 * p] = s_scr[:, p, g * MB_STATE:(g + 1) * MB_STATE, 0:MB_HEADDIM]
            so_ref[:, 2 * p + 1] = s_scr[:, p, g * MB_STATE:(g + 1) * MB_STATE, MB_HEADDIM:2 * MB_HEADDIM]


def _mamba(h, g2, w, s0, conv0, c, bt, slot):
    b, l, _ = h.shape
    npair = MB_HEADS // 2
    consts = [g2, w["wz"], w["wx"], w["wdt"], w["cw"], w["cb"], w["dtb"], w["alog"], w["d"], w["nrm"]]
    scratch = [
        pltpu.VMEM((bt, npair, LANES, LANES), f32),
        pltpu.VMEM((bt, CONV_PAD + c, MB_CONV_DIM), f32),
        pltpu.VMEM((bt, c, BRANCH_W), f32),
        pltpu.VMEM((bt, c, BRANCH_W), f32),
    ] + [pltpu.VMEM((bt, MB_HEADS, c, LANES), f32)] * 4
    return _mixer_call(_mamba_body, "mamba", h, consts, [s0, conv0], c, bt,
                       [(b, CONV_PAD, MB_CONV_DIM)], scratch, slot)


def _ret_body(h_ref, g_ref, wq_ref, wk_ref, wv_ref, wg_ref, lg_ref, cos_ref, sin_ref,
              s0_ref, y_ref, so_ref, s_scr, q_scr, k_scr, v_scr, gt_scr):
    bt, c, _ = h_ref.shape
    j = pl.program_id(1)
    kw = RET_HEADS * RET_DK
    hw = RET_DK // 2

    @pl.when(j == 0)
    def _():
        s_scr[...] = jnp.zeros_like(s_scr)
        for h in range(RET_HEADS):
            s_scr[:, h, h * hw:(h + 1) * hw, :] = s0_ref[:, h, 0:hw, :]
            s_scr[:, h, LANES + h * hw:LANES + (h + 1) * hw, :] = s0_ref[:, h, hw:2 * hw, :]

    u = _normed_input(h_ref, g_ref)
    cos = cos_ref[...]
    sin = sin_ref[...]

    def rot(x):
        x = x.reshape(bt, c, kw)
        x1, x2 = x[:, :, :LANES], x[:, :, LANES:]
        return jnp.concatenate([x1 * cos - x2 * sin, x1 * sin + x2 * cos], axis=-1)

    nh = RET_HEADS
    n = bt * nh * c
    r = min(MXU_DIM, n)
    assert n % r == 0 and r % c == 0
    ng = n // r
    q = rot(jnp.dot(u, wq_ref[...], preferred_element_type=f32))
    k = rot(jnp.dot(u, wk_ref[...], preferred_element_type=f32)) * (RET_DK ** -0.5)
    v = jnp.dot(u, wv_ref[...], preferred_element_type=f32).reshape(bt, c, BRANCH_W)
    gt_scr[...] = jnp.dot(u, wg_ref[...], preferred_element_type=f32).reshape(bt, c, BRANCH_W)
    klane = lax.broadcasted_iota(jnp.int32, (bt, c, kw), 2)
    for h in range(nh):
        own = ((klane & (LANES - 1)) >> _log2(hw)) == h
        q_scr[:, h] = jnp.where(own, q, 0.0)
        k_scr[:, h] = jnp.where(own, k, 0.0)
        v_scr[:, h] = v[:, :, h * RET_DV:(h + 1) * RET_DV]
    qf = q_scr[...].reshape(n, kw)
    kf = k_scr[...].reshape(n, kw)
    vf = v_scr[...].reshape(n, RET_DV)

    row = _iota2(n, 1, 0)
    head = (row >> _log2(c)) & (nh - 1)
    pos = (row & (c - 1)).astype(f32)
    lg_row = jnp.zeros((n, 1), f32)
    for h in range(nh):
        lg_row = jnp.where(head == h, lg_ref[:, h:h + 1], lg_row)
    q_in = qf * jnp.exp((pos + 1.0) * lg_row)
    k_out = kf * jnp.exp((float(c) - 1.0 - pos) * lg_row)
    e_all = jnp.exp(float(c) * lg_row)

    ii, jj = _iota2(r, r, 0), _iota2(r, r, 1)
    incl = ((ii >> _log2(c)) == (jj >> _log2(c))) & (ii >= jj)
    dist = (ii - jj).astype(f32)
    grp = lambda a, g: a[g * r:(g + 1) * r]
    dec = {}
    for g in range(ng):
        key = (g * r) % (nh * c)
        if key not in dec:
            dec[key] = jnp.exp(jnp.where(incl, dist * grp(lg_row, g), -jnp.inf))
    o_loc = [_dot(_dot_nt(grp(qf, g), grp(kf, g)) * dec[(g * r) % (nh * c)], grp(vf, g)) for g in range(ng)]
    o_loc = o_loc[0] if ng == 1 else jnp.concatenate(o_loc, axis=0)

    blocks = [(b, h) for b in range(bt) for h in range(nh)]
    rows = lambda a, i: a[i * c:(i + 1) * c]
    s_old = [s_scr[b, h] for b, h in blocks]
    o_st = [_dot(rows(q_in, i), s_old[i]) for i in range(len(blocks))]
    for i, (b, h) in enumerate(blocks):
        s_scr[b, h] = s_old[i] * rows(e_all, i)[0:1] + _dot_tn(rows(k_out, i), rows(vf, i))
    for b in range(bt):
        y = jnp.concatenate([_rms(rows(o_loc, b * nh + h) + o_st[b * nh + h]) for h in range(nh)], axis=-1)
        y_ref[b] = (y * _silu(gt_scr[b])).astype(bf16)

    @pl.when(j == pl.num_programs(1) - 1)
    def _():
        for h in range(RET_HEADS):
            so_ref[:, h, 0:hw, :] = s_scr[:, h, h * hw:(h + 1) * hw, :]
            so_ref[:, h, hw:2 * hw, :] = s_scr[:, h, LANES + h * hw:LANES + (h + 1) * hw, :]


def _ret(h, g2, w, cos, sin, lg, s0, c, bt, slot):
    kw = RET_HEADS * RET_DK
    consts = [g2, w["wq"], w["wk"], w["wv"], w["wg"], lg]
    scratch = [
        pltpu.VMEM((bt, RET_HEADS, kw, RET_DV), f32),
        pltpu.VMEM((bt, RET_HEADS, c, kw), f32),
        pltpu.VMEM((bt, RET_HEADS, c, kw), f32),
        pltpu.VMEM((bt, RET_HEADS, c, RET_DV), f32),
        pltpu.VMEM((bt, c, BRANCH_W), f32),
    ]
    return _mixer_call(_ret_body, "ret", h, consts, [s0], c, bt, [], scratch, slot, chunk_ins=(cos, sin))


def _hgrn_body(h_ref, g_ref, wq_ref, wf_ref, wi_ref, wg_ref, lb_ref, nrm_ref,
               s0_ref, y_ref, so_ref, st_scr, q_scr, k_scr, v_scr, cum_scr, gt_scr):
    bt, c, _ = h_ref.shape
    j = pl.program_id(1)
    nh = HG_HEADS
    n, r, ng = _stack_geometry(bt, c, nh)
    sub = HG_SUB

    @pl.when(j == 0)
    def _():
        def init(b, carry):
            for h in range(nh):
                st_scr[b, h] = s0_ref[b, h].T
            return carry
        lax.fori_loop(0, bt, init, 0)

    u = _normed_input(h_ref, g_ref)
    q = _silu(jnp.dot(u, wq_ref[...], preferred_element_type=f32))
    lb = lb_ref[...]
    fg = lb + (1.0 - lb) * jax.nn.sigmoid(jnp.dot(u, wf_ref[...], preferred_element_type=f32))
    v = jnp.dot(u, wi_ref[...], preferred_element_type=f32)
    gt_scr[...] = jnp.dot(u, wg_ref[...], preferred_element_type=f32).reshape(bt, c, BRANCH_W)
    cum = _dot_sel_left(_blockdiag_ltri(bt * c, c), jnp.log(fg))
    k = 1.0 - fg
    for h in range(nh):
        sl = slice(h * HG_DK, (h + 1) * HG_DK)
        q_scr[:, h] = q[:, sl].reshape(bt, c, HG_DK)
        k_scr[:, h] = k[:, sl].reshape(bt, c, HG_DK)
        v_scr[:, h] = v[:, sl].reshape(bt, c, HG_DV)
        cum_scr[:, h] = cum[:, sl].reshape(bt, c, HG_DK)
    qf = q_scr[...].reshape(n, HG_DK)
    kf = k_scr[...].reshape(n, HG_DK)
    vf = v_scr[...].reshape(n, HG_DV)
    cf = cum_scr[...].reshape(n, HG_DK)

    ii, jj = _iota2(r, r, 0), _iota2(r, r, 1)
    scores = [jnp.zeros((r, r), f32) for _ in range(ng)]

    q3, k3, c3 = (a.reshape(n // sub, sub, HG_DK) for a in (qf, kf, cf))
    sub_i = lax.broadcasted_iota(jnp.int32, q3.shape, 1)
    ones = jnp.ones((HG_DK, r), bf16)
    same_sub = (ii >> _log2(sub)) == (jj >> _log2(sub))
    for t in range(sub):
        pt = jnp.where(sub_i >= t, q3 * k3[:, t:t + 1, :] * jnp.exp(c3 - c3[:, t:t + 1, :]), 0.0)
        row_sum = jnp.dot(pt.reshape(n, HG_DK).astype(bf16), ones, preferred_element_type=f32)
        place = same_sub & ((jj & (sub - 1)) == t)
        scores = [jnp.where(place, row_sum[g * r:(g + 1) * r], scores[g]) for g in range(ng)]

    m = sub
    while m < c:
        q4, k4, c4 = (a.reshape(n // (2 * m), 2, m, HG_DK) for a in (qf, kf, cf))
        bnd = c4[:, 0:1, m - 1:m, :]
        odd = lax.broadcasted_iota(jnp.int32, q4.shape, 1) == 1
        e = jnp.exp(jnp.where(odd, c4 - bnd, bnd - c4))
        q_t = jnp.where(odd, q4 * e, 0.0).reshape(n, HG_DK)
        k_t = jnp.where(odd, 0.0, k4 * e).reshape(n, HG_DK)
        same_pair = (ii >> _log2(2 * m)) == (jj >> _log2(2 * m))
        scores = [scores[g] + jnp.where(same_pair, _dot_nt(q_t[g * r:(g + 1) * r], k_t[g * r:(g + 1) * r]), 0.0)
                  for g in range(ng)]
        m *= 2

    cb3 = cf.reshape(n // c, c, HG_DK)
    clast3 = cb3[:, c - 1:c, :]
    q_in = qf * jnp.exp(cf)
    k_out = (kf.reshape(n // c, c, HG_DK) * jnp.exp(clast3 - cb3)).reshape(n, HG_DK)
    e_all = jnp.exp(clast3)
    blocks = [(b, h) for b in range(bt) for h in range(nh)]
    rows = lambda a, i: a[i * c:(i + 1) * c]
    st_old = [st_scr[b, h] for b, h in blocks]
    o_state = [_dot_nt(rows(q_in, i), st_old[i]) for i in range(len(blocks))]
    for i, (b, h) in enumerate(blocks):
        st_scr[b, h] = st_old[i] * e_all[i] + _dot_tn(rows(vf, i), rows(k_out, i))
    o = [_dot(scores[g], vf[g * r:(g + 1) * r]) for g in range(ng)]
    o = o[0] if ng == 1 else jnp.concatenate(o, axis=0)
    for b in range(bt):
        y = jnp.concatenate([_rms(rows(o, b * nh + h) + o_state[b * nh + h]) * nrm_ref[...] for h in range(nh)],
                            axis=-1)
        y_ref[b] = (y * _silu(gt_scr[b])).astype(bf16)

    @pl.when(j == pl.num_programs(1) - 1)
    def _():
        def fin(b, carry):
            for h in range(nh):
                so_ref[b, h] = st_scr[b, h].T
            return carry
        lax.fori_loop(0, bt, fin, 0)


def _hgrn(h, g2, w, s0, c, bt, slot):
    consts = [g2, w["wq"], w["wf"], w["wi"], w["wg"], w["lb"], w["nrm"]]
    scratch = ([pltpu.VMEM((bt, HG_HEADS, HG_DV, HG_DK), f32)] + [pltpu.VMEM((bt, HG_HEADS, c, HG_DK), f32)] * 4
               + [pltpu.VMEM((bt, c, BRANCH_W), f32)])
    return _mixer_call(_hgrn_body, "hgrn", h, consts, [s0], c, bt, [], scratch, slot)


def _unit_lower_inverses(a_list, r, c):
    ii, jj = _iota2(r, r, 0), _iota2(r, r, 1)
    eye = jnp.where(ii == jj, 1.0, 0.0)
    t_list = [eye for _ in a_list]
    m = 1
    while m < c:
        lm = _log2(m)
        join = ((ii >> (lm + 1)) == (jj >> (lm + 1))) & (((ii >> lm) & 1) == 1) & (((jj >> lm) & 1) == 0)
        f_list = [jnp.where(join, a, 0.0) for a in a_list]
        if m == 1:
            t_list = [eye - f for f in f_list]
        else:
            ft_list = [_dot(f, t) for f, t in zip(f_list, t_list)]
            t_list = [t - _dot(t, ft) for t, ft in zip(t_list, ft_list)]
        m *= 2
    return t_list


def _gdn_body(h_ref, g_ref, wqkv_ref, wz_ref, wab_ref, cw_ref, dtb_ref, alog_ref, nrm_ref,
              s0_ref, c0_ref, y_ref, so_ref, co_ref, s_scr, xp_scr, z_scr, q_scr, k_scr, v_scr, cum_scr, beta_scr):
    bt, c, _ = h_ref.shape
    j = pl.program_id(1)
    nh = GDN_HEADS
    hk = nh * GDN_DK
    n, r, ng = _stack_geometry(bt, c, nh)

    @pl.when(j == 0)
    def _():
        s_scr[...] = s0_ref[...]
        xp_scr[:, 0:CONV_PAD, :] = c0_ref[...]

    u = _normed_input(h_ref, g_ref)
    xp_scr[:, CONV_PAD:CONV_PAD + c, :] = jnp.dot(u, wqkv_ref[...], preferred_element_type=f32).reshape(
        bt, c, GDN_CONV_DIM)
    z_scr[...] = jnp.dot(u, wz_ref[...], preferred_element_type=f32).reshape(bt, c, BRANCH_W)
    ab = jnp.dot(u, wab_ref[...], preferred_element_type=f32)
    log_g = -jnp.exp(alog_ref[...]) * jax.nn.softplus(ab + dtb_ref[...])
    cum = _dot_sel_left(_blockdiag_ltri(bt * c, c), log_g)
    beta = jax.nn.sigmoid(ab)
    qkv = _silu(_causal_conv(xp_scr, cw_ref, c))
    tail = xp_scr[:, c:c + CONV_PAD, :]
    xp_scr[:, 0:CONV_PAD, :] = tail
    co_ref[...] = tail
    lane_bcast = lambda col: jnp.broadcast_to(col, (bt * c, LANES)).reshape(bt, c, LANES)
    for h in range(nh):
        q = qkv[:, :, h * GDN_DK:(h + 1) * GDN_DK]
        k = qkv[:, :, hk + h * GDN_DK:hk + (h + 1) * GDN_DK]
        q_scr[:, h] = q * lax.rsqrt(jnp.sum(q * q, axis=-1, keepdims=True) + EPS) * (GDN_DK ** -0.5)
        k_scr[:, h] = k * lax.rsqrt(jnp.sum(k * k, axis=-1, keepdims=True) + EPS)
        v_scr[:, h] = qkv[:, :, 2 * hk + h * GDN_DV:2 * hk + (h + 1) * GDN_DV]
        cum_scr[:, h] = lane_bcast(cum[:, h:h + 1])
        beta_scr[:, h] = lane_bcast(beta[:, nh + h:nh + h + 1])

    grp = lambda ref, g: _group_rows(ref, g, r)
    groups = range(ng)
    qs = [grp(q_scr, g) for g in groups]
    ks = [grp(k_scr, g) for g in groups]
    vs = [grp(v_scr, g) for g in groups]
    cb = [grp(cum_scr, g) for g in groups]
    bb = [grp(beta_scr, g) for g in groups]
    ii, jj = _iota2(r, r, 0), _iota2(r, r, 1)
    same = (ii >> _log2(c)) == (jj >> _log2(c))
    incl = same & (ii >= jj)
    strict = same & (ii > jj)
    dec = [jnp.exp(jnp.where(incl, _widen(cb[g], r) - cb[g].T[0:1, :], -jnp.inf)) for g in groups]
    a_mat = [jnp.where(strict, _widen(bb[g], r) * dec[g] * _dot_nt(ks[g], ks[g]), 0.0) for g in groups]
    t_inv = _unit_lower_inverses(a_mat, r, c)
    e_in = [jnp.exp(cb[g]) for g in groups]
    sol = [_dot(t_inv[g], jnp.concatenate([vs[g] * bb[g], ks[g] * (e_in[g] * bb[g])], axis=-1)) for g in groups]
    scd = [_dot_nt(qs[g], ks[g]) * dec[g] for g in groups]
    q_in = [qs[g] * e_in[g] for g in groups]

    rows = lambda a, i: a[i * c:(i + 1) * c]
    nb = r // c
    o_all = []
    for g in groups:
        blocks = [divmod(g * nb + i, nh) for i in range(nb)]
        s_old = [s_scr[b, h] for b, h in blocks]
        proj = [_dot(jnp.concatenate([rows(sol[g], i)[:, GDN_DV:], rows(q_in[g], i)], axis=0), s_old[i])
                for i in range(nb)]
        w_new = [rows(sol[g], i)[:, :GDN_DV] - proj[i][:c] for i in range(nb)]
        for i, (b, h) in enumerate(blocks):
            cum_i = rows(cb[g], i)
            clast = cum_i[c - 1:c]
            s_scr[b, h] = s_old[i] * jnp.exp(clast) + _dot_tn(rows(ks[g], i) * jnp.exp(clast - cum_i), w_new[i])
        o_all.append(_dot(scd[g], jnp.concatenate(w_new, axis=0)) + jnp.concatenate([p[c:] for p in proj], axis=0))
    o = o_all[0] if ng == 1 else jnp.concatenate(o_all, axis=0)
    for b in range(bt):
        y = jnp.concatenate([_rms(rows(o, b * nh + h)) * nrm_ref[...] for h in range(nh)], axis=-1)
        y_ref[b] = (y * _silu(z_scr[b])).astype(bf16)

    @pl.when(j == pl.num_programs(1) - 1)
    def _():
        so_ref[...] = s_scr[...]


def _gdn(h, g2, w, s0, conv0, c, bt, slot):
    b, l, _ = h.shape
    consts = [g2, w["wqkv"], w["wz"], w["wab"], w["cw"], w["dtb"], w["alog"], w["nrm"]]
    scratch = [
        pltpu.VMEM((bt, GDN_HEADS, GDN_DK, GDN_DV), f32),
        pltpu.VMEM((bt, CONV_PAD + c, GDN_CONV_DIM), f32),
        pltpu.VMEM((bt, c, BRANCH_W), f32),
    ] + [pltpu.VMEM((bt, GDN_HEADS, c, LANES), f32)] * 5
    return _mixer_call(_gdn_body, "gdn", h, consts, [s0, conv0], c, bt,
                       [(b, CONV_PAD, GDN_CONV_DIM)], scratch, slot)


def _pad_lanes(v, offset=0):
    row = jnp.zeros((1, LANES), f32)
    return row.at[0, offset:offset + v.shape[0]].set(v.astype(f32))


def _layer_weights(l, norm_g, ffn_w_in, ffn_w_out, w_in, w_branch, w_out, mb_conv_w, mb_conv_b, mb_a_log,
                   mb_dt_bias, mb_d, mb_norm, lb_all, hg_norm, gdn_conv_w, gdn_a_log, gdn_dt_bias, gdn_norm):
    wi = w_in[l]
    col = lambda n: wi[:, IN_OFFSETS[n]:IN_OFFSETS[n + 1]]
    colb = lambda n: col(n).astype(bf16)
    pad_cols = lambda a: jnp.pad(a, ((0, 0), (0, LANES - a.shape[1]))).astype(bf16)
    hw = RET_DK // 2
    perm = np.arange(RET_HEADS * RET_DK).reshape(RET_HEADS, 2, hw).transpose(1, 0, 2).reshape(-1)
    row = lambda v: v.astype(f32).reshape(1, -1)
    return {
        "g_ffn1": norm_g[l, 0:2], "g_mix": norm_g[l, 2:4], "g_ffn2": norm_g[l, 4:6], "g_u": norm_g[l, 2:3],
        "layer": l, "ffn_in": ffn_w_in, "ffn_out": ffn_w_out,
        "mamba": {"wz": colb(0), "wx": colb(1), "wdt": pad_cols(col(2)), "cw": mb_conv_w[l], "cb": row(mb_conv_b[l]),
                  "dtb": _pad_lanes(mb_dt_bias[l]), "alog": _pad_lanes(mb_a_log[l]),
                  "d": row(jnp.repeat(mb_d[l], MB_HEADDIM)), "nrm": row(mb_norm[l])},
        "hgrn": {"wq": colb(3), "wf": colb(4), "wi": colb(5), "wg": colb(6), "lb": row(lb_all[l]),
                 "nrm": row(hg_norm[l])},
        "ret": {"wq": col(7)[:, perm].astype(bf16), "wk": col(8)[:, perm].astype(bf16), "wv": colb(9), "wg": colb(10)},
        "gdn": {"wqkv": colb(11), "wz": colb(12), "wab": pad_cols(jnp.concatenate([col(13), col(14)], axis=1)),
                "cw": gdn_conv_w[l], "dtb": _pad_lanes(gdn_dt_bias[l]), "alog": _pad_lanes(gdn_a_log[l]),
                "nrm": row(gdn_norm[l])},
        "w_gate": colb(15), "w_branch": w_branch[l].astype(bf16), "w_out": w_out[l].astype(bf16),
    }


def _rope_tables(pos):
    half = RET_DK // 2
    inv_freq = 1.0 / (ROPE_BASE ** jnp.linspace(0.0, 1.0, half, dtype=f32))
    ang = pos.astype(f32)[:, None] * inv_freq[None, :]
    reps = LANES // half
    return jnp.tile(jnp.cos(ang), (1, reps)), jnp.tile(jnp.sin(ang), (1, reps))


def _pad_conv(buf):
    return jnp.pad(buf, ((0, 0), (CONV_PAD - (CONV_W - 1), 0), (0, 0)))


def _layer(h, w, states, rope, lg, chunks, bt, depth, stacked):
    b, l, d = h.shape
    s_ssm, s_ssm_conv, s_hg, s_ret, s_gdn, s_gdn_conv = states
    slot = lambda n: (w["layer"], depth, None if stacked is None else stacked[n])
    h = _ffn(h.reshape(b * l, d), w["g_ffn1"], w["ffn_in"], w["ffn_out"], w["layer"], 0).reshape(b, l, d)
    y_mb, n_ssm, n_ssm_conv = _mamba(h, w["g_u"], w["mamba"], s_ssm, _pad_conv(s_ssm_conv),
                                     chunks["mamba"], bt, slot(0))
    y_hg, n_hg = _hgrn(h, w["g_u"], w["hgrn"], s_hg, chunks["hgrn"], bt, slot(1))
    y_rt, n_ret = _ret(h, w["g_u"], w["ret"], rope[0], rope[1], lg, s_ret, chunks["ret"], bt, slot(2))
    y_gd, n_gdn, n_gdn_conv = _gdn(h, w["g_u"], w["gdn"], s_gdn, _pad_conv(s_gdn_conv), chunks["gdn"], bt, slot(3))
    ys = [y.reshape(b * l, BRANCH_W) for y in (y_mb, y_hg, y_rt, y_gd)]
    hf = _merge(h.reshape(b * l, d), w["g_mix"], ys, w["w_gate"], w["w_branch"], w["w_out"])
    hf = _ffn(hf, w["g_ffn2"], w["ffn_in"], w["ffn_out"], w["layer"], 1)
    convs = (n_ssm_conv[:, CONV_PAD - (CONV_W - 1):], n_gdn_conv[:, CONV_PAD - (CONV_W - 1):])
    return hf.reshape(b, l, d), (n_ssm, n_hg, n_ret, n_gdn), convs


def _chunks(l):
    pick = lambda pref: pref if l % pref == 0 else math.gcd(l, pref)
    return {"mamba": pick(128), "ret": pick(128), "hgrn": pick(64), "gdn": pick(64)}


def kernel(x_prompt, x_sample, state_ssm, state_ssm_conv, state_hgrn, state_ret, state_gdn, state_gdn_conv,
           norm_g, ffn_w_in, ffn_w_out, w_in, w_branch, w_out, mb_conv_w, mb_conv_b, mb_a_log, mb_dt_bias, mb_d,
           mb_norm, hg_lb_logits, hg_norm, gdn_conv_w, gdn_a_log, gdn_dt_bias, gdn_norm):
    depth = norm_g.shape[0]
    lb_all = jnp.cumsum(jax.nn.softmax(hg_lb_logits.astype(f32), axis=0), axis=0)
    lb_all = lb_all - lb_all[0:1]
    lg = _pad_lanes(jnp.log(1.0 - jnp.exp2(-5.0 - jnp.arange(RET_HEADS, dtype=f32))))

    bp, lp, _ = x_prompt.shape
    bs, ls, _ = x_sample.shape
    rope_p = _rope_tables(jnp.arange(lp))
    rope_s = _rope_tables(PAST_LEN + jnp.arange(ls))
    zero_states = (
        (jnp.zeros((1, bp, MB_HEADS, MB_STATE, MB_HEADDIM), f32), 0),
        jnp.zeros((bp, CONV_W - 1, MB_CONV_DIM), f32),
        (jnp.zeros((1, bp, HG_HEADS, HG_DK, HG_DV), f32), 0),
        (jnp.zeros((1, bp, RET_HEADS, RET_DK, RET_DV), f32), 0),
        (jnp.zeros((1, bp, GDN_HEADS, GDN_DK, GDN_DV), f32), 0),
        jnp.zeros((bp, CONV_W - 1, GDN_CONV_DIM), f32),
    )
    bt_p = math.gcd(bp, 8)
    bt_s = math.gcd(bs, 16)

    ffn_w_in = ffn_w_in.astype(bf16)
    ffn_w_out = ffn_w_out.astype(bf16)
    hp, hs = x_prompt, x_sample
    st_p = st_s = None
    cv_p, cv_s = [], []
    for l in range(depth):
        w = _layer_weights(l, norm_g, ffn_w_in, ffn_w_out, w_in, w_branch, w_out, mb_conv_w, mb_conv_b,
                           mb_a_log, mb_dt_bias, mb_d, mb_norm, lb_all, hg_norm, gdn_conv_w, gdn_a_log,
                           gdn_dt_bias, gdn_norm)
        hp, st_p, cp = _layer(hp, w, zero_states, rope_p, lg, _chunks(lp), bt_p, depth, st_p)
        past = ((state_ssm, l), state_ssm_conv[l], (state_hgrn, l), (state_ret, l), (state_gdn, l), state_gdn_conv[l])
        hs, st_s, cs = _layer(hs, w, past, rope_s, lg, _chunks(ls), bt_s, depth, st_s)
        cv_p.append(cp)
        cv_s.append(cs)

    stack = lambda cvs, i: jnp.stack([cv[i] for cv in cvs], axis=0)
    return (hp, hs,
            st_p[0], stack(cv_p, 0), st_p[1], st_p[2], st_p[3], stack(cv_p, 1),
            st_s[0], stack(cv_s, 0), st_s[1], st_s[2], st_s[3], stack(cv_s, 1))
```

```python
import functools
import math

import numpy as np
import jax
import jax.numpy as jnp
from jax import lax
from jax.experimental import pallas as pl
from jax.experimental.pallas import tpu as pltpu

f32 = jnp.float32
bf16 = jnp.bfloat16

D_MODEL = 1024
D_FF = 2816
BRANCH_W = 512
N_BRANCH = 4
CONV_W = 4
EPS = 1e-6
ROPE_BASE = 10000.0
PAST_LEN = 16384

MB_HEADS, MB_HEADDIM, MB_STATE, MB_GROUPS = 8, 64, 64, 2
MB_CONV_DIM = BRANCH_W + 2 * MB_GROUPS * MB_STATE
HG_HEADS, HG_DK, HG_DV = 4, 128, 128
RET_HEADS, RET_DK, RET_DV = 4, 64, 128
GDN_HEADS, GDN_DK, GDN_DV = 4, 128, 128
GDN_CONV_DIM = 2 * GDN_HEADS * GDN_DK + GDN_HEADS * GDN_DV

IN_SPLITS = (
    BRANCH_W, MB_CONV_DIM, MB_HEADS,
    HG_HEADS * HG_DK, HG_HEADS * HG_DK, BRANCH_W, BRANCH_W,
    RET_HEADS * RET_DK, RET_HEADS * RET_DK, BRANCH_W, BRANCH_W,
    GDN_CONV_DIM, BRANCH_W, GDN_HEADS, GDN_HEADS,
    N_BRANCH * D_MODEL,
)
IN_OFFSETS = tuple(int(v) for v in np.cumsum((0,) + IN_SPLITS))

LANES = 128
SUBLANES = 8
MXU_DIM = 256
STACK_ROWS = 128
CONV_PAD = SUBLANES
VMEM_LIMIT = 56 * 1024 * 1024
FFN_TM = 512
MERGE_TM = 512
HG_SUB = SUBLANES


def _dot(a, b):
    return jnp.dot(a.astype(bf16), b.astype(bf16), preferred_element_type=f32)


def _dot_nt(a, b):
    return lax.dot_general(a.astype(bf16), b.astype(bf16), (((1,), (1,)), ((), ())), preferred_element_type=f32)


def _dot_tn(a, b):
    return lax.dot_general(a.astype(bf16), b.astype(bf16), (((0,), (0,)), ((), ())), preferred_element_type=f32)


def _split3(x):
    hi = x.astype(bf16)
    r = x - hi.astype(f32)
    mid = r.astype(bf16)
    lo = (r - mid.astype(f32)).astype(bf16)
    return hi, mid, lo


def _dot_sel_left(m01, x):
    hi, mid, lo = _split3(x)
    return (jnp.dot(m01, hi, preferred_element_type=f32) + jnp.dot(m01, mid, preferred_element_type=f32)
            + jnp.dot(m01, lo, preferred_element_type=f32))


def _rms(x):
    return x * lax.rsqrt(jnp.mean(x * x, axis=-1, keepdims=True) + EPS)


def _silu(x):
    return x * jax.nn.sigmoid(x)


def _iota2(n, m, axis):
    return lax.broadcasted_iota(jnp.int32, (n, m), axis)


def _log2(n):
    assert n > 0 and n & (n - 1) == 0, n
    return n.bit_length() - 1


def _ltri(c):
    return jnp.where(_iota2(c, c, 0) >= _iota2(c, c, 1), 1.0, 0.0).astype(bf16)


def _blockdiag_ltri(n, c):
    ii, jj = _iota2(n, n, 0), _iota2(n, n, 1)
    lc = _log2(c)
    return jnp.where(((ii >> lc) == (jj >> lc)) & (ii >= jj), 1.0, 0.0).astype(bf16)


def _row_bcast(col, c):
    dg = jnp.where(_iota2(c, c, 0) == _iota2(c, c, 1), jnp.broadcast_to(col, (c, c)), 0.0)
    return _dot_sel_left(jnp.ones((c, c), bf16), dg)


def _decay_matrix(col, c):
    diff = col - _row_bcast(col, c)
    return jnp.exp(jnp.where(_iota2(c, c, 0) >= _iota2(c, c, 1), diff, -jnp.inf))


def _causal_conv(xp_ref, w_ref, c):
    y = xp_ref[:, pl.ds(CONV_PAD - 3, c), :] * w_ref[0:1, :]
    for j in range(1, CONV_W):
        y = y + xp_ref[:, pl.ds(CONV_PAD - 3 + j, c), :] * w_ref[j:j + 1, :]
    return y


def _stack_geometry(bt, c, heads):
    n = bt * heads * c
    r = min(STACK_ROWS, n)
    assert n % r == 0 and r % c == 0 and r % LANES == 0
    return n, r, n // r


def _group_rows(ref, g, r):
    _, heads, c, w = ref.shape
    bpg = r // c
    if bpg <= heads:
        assert heads % bpg == 0
        b, h0 = divmod(g * bpg, heads)
        return ref[b, h0:h0 + bpg].reshape(r, w)
    assert bpg % heads == 0
    nseq = bpg // heads
    return ref[g * nseq:(g + 1) * nseq].reshape(r, w)


def _widen(x, r):
    return x if r == LANES else jnp.concatenate([x] * (r // LANES), axis=-1)


def _ffn_body(x_ref, g_ref, wg_ref, wu_ref, wo_ref, o_ref):
    x = x_ref[...]
    xn = (_rms(x) * g_ref[0:1, :]).astype(bf16)
    gate = jnp.dot(xn, wg_ref[...], preferred_element_type=f32)
    up = jnp.dot(xn, wu_ref[...], preferred_element_type=f32)
    act = (_silu(gate) * up).astype(bf16)
    y = jnp.dot(act, wo_ref[...], preferred_element_type=f32)
    o_ref[...] = x + 0.5 * (_rms(y) * g_ref[1:2, :])


def _ffn(x, g2, w_in, w_out, l, k):
    t = x.shape[0]
    tm = min(FFN_TM, t)
    resident = pl.Buffered(1)
    return pl.pallas_call(
        _ffn_body,
        grid=(t // tm,),
        in_specs=[
            pl.BlockSpec((tm, D_MODEL), lambda i: (i, 0)),
            pl.BlockSpec((2, D_MODEL), lambda i: (0, 0)),
            pl.BlockSpec((None, None, D_MODEL, D_FF), lambda i: (l, k, 0, 0), pipeline_mode=resident),
            pl.BlockSpec((None, None, D_MODEL, D_FF), lambda i: (l, k, 0, 1), pipeline_mode=resident),
            pl.BlockSpec((None, None, D_FF, D_MODEL), lambda i: (l, k, 0, 0), pipeline_mode=resident),
        ],
        out_specs=pl.BlockSpec((tm, D_MODEL), lambda i: (i, 0)),
        out_shape=jax.ShapeDtypeStruct((t, D_MODEL), f32),
        compiler_params=pltpu.CompilerParams(dimension_semantics=("arbitrary",), vmem_limit_bytes=VMEM_LIMIT),
        name="ffn",
    )(x, g2, w_in, w_in, w_out)


def _merge_body(h_ref, g_ref, y0_ref, y1_ref, y2_ref, y3_ref, wg_ref, wb_ref, wo_ref, o_ref):
    h = h_ref[...]
    u = (_rms(h) * g_ref[0:1, :]).astype(bf16)
    merged = jnp.zeros(h.shape, f32)
    for n, y_ref in enumerate((y0_ref, y1_ref, y2_ref, y3_ref)):
        gate = jax.nn.sigmoid(jnp.dot(u, wg_ref[:, n * D_MODEL:(n + 1) * D_MODEL], preferred_element_type=f32))
        merged = merged + gate * jnp.dot(y_ref[...], wb_ref[n], preferred_element_type=f32)
    mixed = jnp.dot(merged.astype(bf16), wo_ref[...], preferred_element_type=f32)
    o_ref[...] = h + _rms(mixed) * g_ref[1:2, :]


def _merge(h, g2, ys, w_gate, w_branch, w_out):
    t = h.shape[0]
    tm = min(MERGE_TM, t)
    row = lambda i: (i, 0)
    resident = pl.Buffered(1)
    return pl.pallas_call(
        _merge_body,
        grid=(t // tm,),
        in_specs=[pl.BlockSpec((tm, D_MODEL), row), pl.BlockSpec((2, D_MODEL), lambda i: (0, 0))]
                 + [pl.BlockSpec((tm, BRANCH_W), row)] * N_BRANCH
                 + [pl.BlockSpec((D_MODEL, N_BRANCH * D_MODEL), lambda i: (0, 0), pipeline_mode=resident),
                    pl.BlockSpec((N_BRANCH, BRANCH_W, D_MODEL), lambda i: (0, 0, 0), pipeline_mode=resident),
                    pl.BlockSpec((D_MODEL, D_MODEL), lambda i: (0, 0), pipeline_mode=resident)],
        out_specs=pl.BlockSpec((tm, D_MODEL), row),
        out_shape=jax.ShapeDtypeStruct((t, D_MODEL), f32),
        compiler_params=pltpu.CompilerParams(dimension_semantics=("arbitrary",), vmem_limit_bytes=VMEM_LIMIT),
        name="merge",
    )(h, g2, *ys, w_gate, w_branch, w_out)


def _normed_input(h_ref, g_ref):
    bt, c, d = h_ref.shape
    return (_rms(h_ref[...].reshape(bt * c, d)) * g_ref[...]).astype(bf16)


def _const_spec(a):
    nd = a.ndim
    return pl.BlockSpec(a.shape, lambda i, j, _nd=nd: (0,) * _nd)


def _batch_spec(bt, a):
    nd = a.ndim
    return pl.BlockSpec((bt,) + a.shape[1:], lambda i, j, _nd=nd: (i,) + (0,) * (_nd - 1))


def _mixer_call(body, name, h, consts, batch_ins, c, bt, batch_outs, scratch, slot, chunk_ins=()):
    b, l, d = h.shape
    layer, depth, prev = slot
    seq = lambda i, j: (i, j, 0)
    st_in, st_entry = batch_ins[0]
    st = tuple(st_in.shape[1:])
    st_block = (None, bt) + st[1:]
    out_shape = ([jax.ShapeDtypeStruct((b, l, BRANCH_W), bf16), jax.ShapeDtypeStruct((depth,) + st, f32)]
                 + [jax.ShapeDtypeStruct(s, f32) for s in batch_outs])
    out_specs = ([pl.BlockSpec((bt, c, BRANCH_W), seq),
                  pl.BlockSpec(st_block, lambda i, j: (layer, i) + (0,) * (len(st) - 1))]
                 + [pl.BlockSpec((bt,) + s[1:], lambda i, j, _nd=len(s): (i,) + (0,) * (_nd - 1))
                    for s in batch_outs])
    operands = [h, *consts, *chunk_ins, st_in, *batch_ins[1:]]
    in_specs = ([pl.BlockSpec((bt, c, d), seq)] + [_const_spec(a) for a in consts]
                + [pl.BlockSpec((c, a.shape[1]), lambda i, j: (j, 0)) for a in chunk_ins]
                + [pl.BlockSpec(st_block, lambda i, j: (st_entry, i) + (0,) * (len(st) - 1))]
                + [_batch_spec(bt, a) for a in batch_ins[1:]])
    n_in = len(operands)
    aliases = {}
    fn = body
    if prev is not None:
        operands.append(prev)
        in_specs.append(pl.BlockSpec(memory_space=pl.ANY))
        aliases = {n_in: 1}
        fn = lambda *refs: body(*refs[:n_in], *refs[n_in + 1:])
    return pl.pallas_call(
        fn,
        grid=(b // bt, l // c),
        in_specs=in_specs,
        out_specs=out_specs,
        out_shape=out_shape,
        scratch_shapes=scratch,
        input_output_aliases=aliases,
        compiler_params=pltpu.CompilerParams(dimension_semantics=("arbitrary", "arbitrary"),
                                             vmem_limit_bytes=VMEM_LIMIT),
        name=name,
    )(*operands)


def _mamba_body(h_ref, g_ref, wz_ref, wx_ref, wdt_ref, cw_ref, cb_ref, dtb_ref, alog_ref, d_ref, nrm_ref,
                s0_ref, c0_ref, y_ref, so_ref, co_ref, s_scr, xp_scr, z_scr, xs_scr, v_scr, cum_scr):
    bt, c, _ = h_ref.shape
    j = pl.program_id(1)
    nh = MB_HEADS
    npair = nh // 2
    hpg = nh // MB_GROUPS
    n = bt * nh * c
    r = min(MXU_DIM, n)
    assert n % r == 0 and r % c == 0
    ng = n // r

    @pl.when(j == 0)
    def _():
        s_scr[...] = jnp.zeros_like(s_scr)
        for p in range(npair):
            g = (2 * p) // hpg
            s_scr[:, p, g * MB_STATE:(g + 1) * MB_STATE, :] = jnp.concatenate(
                [s0_ref[:, 2 * p], s0_ref[:, 2 * p + 1]], axis=-1)
        xp_scr[:, 0:CONV_PAD, :] = c0_ref[...]

    u = _normed_input(h_ref, g_ref)
    z_scr[...] = jnp.dot(u, wz_ref[...], preferred_element_type=f32).reshape(bt, c, BRANCH_W)
    xp_scr[:, CONV_PAD:CONV_PAD + c, :] = jnp.dot(u, wx_ref[...], preferred_element_type=f32).reshape(
        bt, c, MB_CONV_DIM)
    dt = jax.nn.softplus(jnp.dot(u, wdt_ref[...], preferred_element_type=f32) + dtb_ref[...])
    cum = _dot_sel_left(_blockdiag_ltri(bt * c, c), -jnp.exp(alog_ref[...]) * dt)
    xa = _silu(_causal_conv(xp_scr, cw_ref, c) + cb_ref[...])
    tail = xp_scr[:, c:c + CONV_PAD, :]
    xp_scr[:, 0:CONV_PAD, :] = tail
    co_ref[...] = tail
    xs = xa[:, :, :BRANCH_W]
    bm = xa[:, :, BRANCH_W:BRANCH_W + LANES]
    cm = xa[:, :, BRANCH_W + LANES:BRANCH_W + 2 * LANES]
    xs_scr[...] = xs
    lane3 = lax.broadcasted_iota(jnp.int32, (bt, c, LANES), 2)
    lane_bcast = lambda col: jnp.broadcast_to(col, (bt * c, LANES)).reshape(bt, c, LANES)
    for h in range(nh):
        own_half = (lane3 >> _log2(MB_HEADDIM)) == (h % 2)
        v_scr[:, h] = jnp.where(own_half, xs[:, :, (h // 2) * LANES:(h // 2 + 1) * LANES] * lane_bcast(dt[:, h:h + 1]),
                                0.0)
        cum_scr[:, h] = lane_bcast(cum[:, h:h + 1])
    vf = v_scr[...].reshape(n, LANES)
    cbf = cum_scr[...].reshape(n, LANES)
    cm_g = [jnp.where((lane3 >> _log2(MB_STATE)) == g, cm, 0.0) for g in range(MB_GROUPS)]
    blocks_of = lambda first, count: [divmod(first + i, nh) for i in range(count)]
    c_rows = lambda blocks: jnp.concatenate([cm_g[h // hpg][b] for b, h in blocks], axis=0)
    b_rows = lambda blocks: jnp.concatenate([bm[b] for b, _ in blocks], axis=0)

    ii, jj = _iota2(r, r, 0), _iota2(r, r, 1)
    incl = ((ii >> _log2(c)) == (jj >> _log2(c))) & (ii >= jj)
    grp = lambda a, g: a[g * r:(g + 1) * r]
    gblocks = [blocks_of(g * (r // c), r // c) for g in range(ng)]
    dec = [jnp.exp(jnp.where(incl, _widen(grp(cbf, g), r) - grp(cbf, g).T[0:1, :], -jnp.inf)) for g in range(ng)]
    o_loc = [_dot(_dot_nt(c_rows(gblocks[g]), b_rows(gblocks[g])) * dec[g], grp(vf, g)) for g in range(ng)]
    o_loc = o_loc[0] if ng == 1 else jnp.concatenate(o_loc, axis=0)

    cb3 = cbf.reshape(n // c, c, LANES)
    k_out = (vf.reshape(n // c, c, LANES) * jnp.exp(cb3[:, c - 1:c, :] - cb3)).reshape(n, LANES)
    e_in = jnp.exp(cbf)
    lo_half = _iota2(1, LANES, 1) < MB_HEADDIM
    row_lo = _iota2(2 * c, LANES, 0) < c
    own = row_lo == (_iota2(2 * c, LANES, 1) < MB_HEADDIM)
    pairs = [(b, p) for b in range(bt) for p in range(npair)]
    prow = lambda a, i: a[i * 2 * c:(i + 1) * 2 * c]
    s_old = [s_scr[b, p] for b, p in pairs]
    pblocks = [blocks_of(i * 2, 2) for i in range(len(pairs))]
    o_st = [jnp.where(own, _dot(c_rows(pblocks[i]), s_old[i]) * prow(e_in, i), 0.0) for i in range(len(pairs))]
    for i, (b, p) in enumerate(pairs):
        cum_i = prow(cbf, i)
        e_all = jnp.exp(jnp.where(lo_half, cum_i[c - 1:c], cum_i[2 * c - 1:2 * c]))
        s_scr[b, p] = s_old[i] * e_all + _dot_tn(b_rows(pblocks[i]), prow(k_out, i))
    for b in range(bt):
        ys = []
        for p in range(npair):
            o2 = prow(o_loc, b * npair + p) + o_st[b * npair + p]
            sl = slice(p * LANES, (p + 1) * LANES)
            ys.append(o2[:c] + o2[c:] + d_ref[:, sl] * xs_scr[b, :, sl])
        y = jnp.concatenate(ys, axis=-1) * _silu(z_scr[b])
        y_ref[b] = (_rms(y) * nrm_ref[...]).astype(bf16)

    @pl.when(j == pl.num_programs(1) - 1)
    def _():
        for p in range(npair):
            g = (2 * p) // hpg
            so_ref[:, 2 * p] = s_scr[:, p, g * MB_STATE:(g + 1) * MB_STATE, 0:MB_HEADDIM]
            so_ref[:, 2 * p + 1] = s_scr[:, p, g * MB_STATE:(g + 1) * MB_STATE, MB_HEADDIM:2 * MB_HEADDIM]


def _mamba(h, g2, w, s0, conv0, c, bt, slot):
    b, l, _ = h.shape
    npair = MB_HEADS // 2
    consts = [g2, w["wz"], w["wx"], w["wdt"], w["cw"], w["cb"], w["dtb"], w["alog"], w["d"], w["nrm"]]
    scratch = [
        pltpu.VMEM((bt, npair, LANES, LANES), f32),
        pltpu.VMEM((bt, CONV_PAD + c, MB_CONV_DIM), f32),
        pltpu.VMEM((bt, c, BRANCH_W), f32),
        pltpu.VMEM((bt, c, BRANCH_W), f32),
    ] + [pltpu.VMEM((bt, MB_HEADS, c, LANES), f32)] * 2
    return _mixer_call(_mamba_body, "mamba", h, consts, [s0, conv0], c, bt,
                       [(b, CONV_PAD, MB_CONV_DIM)], scratch, slot)


def _ret_body(h_ref, g_ref, wq_ref, wk_ref, wv_ref, wg_ref, lg_ref, cos_ref, sin_ref,
              s0_ref, y_ref, so_ref, s_scr, q_scr, k_scr, v_scr, gt_scr):
    bt, c, _ = h_ref.shape
    j = pl.program_id(1)
    kw = RET_HEADS * RET_DK
    hw = RET_DK // 2

    @pl.when(j == 0)
    def _():
        s_scr[...] = jnp.zeros_like(s_scr)
        for h in range(RET_HEADS):
            s_scr[:, h, h * hw:(h + 1) * hw, :] = s0_ref[:, h, 0:hw, :]
            s_scr[:, h, LANES + h * hw:LANES + (h + 1) * hw, :] = s0_ref[:, h, hw:2 * hw, :]

    u = _normed_input(h_ref, g_ref)
    cos = cos_ref[...]
    sin = sin_ref[...]

    def rot(x):
        x = x.reshape(bt, c, kw)
        x1, x2 = x[:, :, :LANES], x[:, :, LANES:]
        return jnp.concatenate([x1 * cos - x2 * sin, x1 * sin + x2 * cos], axis=-1)

    nh = RET_HEADS
    n = bt * nh * c
    r = min(MXU_DIM, n)
    assert n % r == 0 and r % c == 0
    ng = n // r
    q = rot(jnp.dot(u, wq_ref[...], preferred_element_type=f32))
    k = rot(jnp.dot(u, wk_ref[...], preferred_element_type=f32)) * (RET_DK ** -0.5)
    v = jnp.dot(u, wv_ref[...], preferred_element_type=f32).reshape(bt, c, BRANCH_W)
    gt_scr[...] = jnp.dot(u, wg_ref[...], preferred_element_type=f32).reshape(bt, c, BRANCH_W)
    klane = lax.broadcasted_iota(jnp.int32, (bt, c, kw), 2)
    for h in range(nh):
        own = ((klane & (LANES - 1)) >> _log2(hw)) == h
        q_scr[:, h] = jnp.where(own, q, 0.0)
        k_scr[:, h] = jnp.where(own, k, 0.0)
        v_scr[:, h] = v[:, :, h * RET_DV:(h + 1) * RET_DV]
    qf = q_scr[...].reshape(n, kw)
    kf = k_scr[...].reshape(n, kw)
    vf = v_scr[...].reshape(n, RET_DV)

    row = _iota2(n, 1, 0)
    head = (row >> _log2(c)) & (nh - 1)
    pos = (row & (c - 1)).astype(f32)
    lg_row = jnp.zeros((n, 1), f32)
    for h in range(nh):
        lg_row = jnp.where(head == h, lg_ref[:, h:h + 1], lg_row)
    q_in = qf * jnp.exp((pos + 1.0) * lg_row)
    k_out = kf * jnp.exp((float(c) - 1.0 - pos) * lg_row)
    e_all = jnp.exp(float(c) * lg_row)

    ii, jj = _iota2(r, r, 0), _iota2(r, r, 1)
    incl = ((ii >> _log2(c)) == (jj >> _log2(c))) & (ii >= jj)
    dist = (ii - jj).astype(f32)
    grp = lambda a, g: a[g * r:(g + 1) * r]
    dec = {}
    for g in range(ng):
        key = (g * r) % (nh * c)
        if key not in dec:
            dec[key] = jnp.exp(jnp.where(incl, dist * grp(lg_row, g), -jnp.inf))
    o_loc = [_dot(_dot_nt(grp(qf, g), grp(kf, g)) * dec[(g * r) % (nh * c)], grp(vf, g)) for g in range(ng)]
    o_loc = o_loc[0] if ng == 1 else jnp.concatenate(o_loc, axis=0)

    blocks = [(b, h) for b in range(bt) for h in range(nh)]
    rows = lambda a, i: a[i * c:(i + 1) * c]
    s_old = [s_scr[b, h] for b, h in blocks]
    o_st = [_dot(rows(q_in, i), s_old[i]) for i in range(len(blocks))]
    for i, (b, h) in enumerate(blocks):
        s_scr[b, h] = s_old[i] * rows(e_all, i)[0:1] + _dot_tn(rows(k_out, i), rows(vf, i))
    for b in range(bt):
        y = jnp.concatenate([_rms(rows(o_loc, b * nh + h) + o_st[b * nh + h]) for h in range(nh)], axis=-1)
        y_ref[b] = (y * _silu(gt_scr[b])).astype(bf16)

    @pl.when(j == pl.num_programs(1) - 1)
    def _():
        for h in range(RET_HEADS):
            so_ref[:, h, 0:hw, :] = s_scr[:, h, h * hw:(h + 1) * hw, :]
            so_ref[:, h, hw:2 * hw, :] = s_scr[:, h, LANES + h * hw:LANES + (h + 1) * hw, :]


def _ret(h, g2, w, cos, sin, lg, s0, c, bt, slot):
    kw = RET_HEADS * RET_DK
    consts = [g2, w["wq"], w["wk"], w["wv"], w["wg"], lg]
    scratch = [
        pltpu.VMEM((bt, RET_HEADS, kw, RET_DV), f32),
        pltpu.VMEM((bt, RET_HEADS, c, kw), f32),
        pltpu.VMEM((bt, RET_HEADS, c, kw), f32),
        pltpu.VMEM((bt, RET_HEADS, c, RET_DV), f32),
        pltpu.VMEM((bt, c, BRANCH_W), f32),
    ]
    return _mixer_call(_ret_body, "ret", h, consts, [s0], c, bt, [], scratch, slot, chunk_ins=(cos, sin))


def _hgrn_body(h_ref, g_ref, wq_ref, wf_ref, wi_ref, wg_ref, lb_ref, nrm_ref,
               s0_ref, y_ref, so_ref, st_scr, q_scr, k_scr, v_scr, cum_scr, gt_scr):
    bt, c, _ = h_ref.shape
    j = pl.program_id(1)
    nh = HG_HEADS
    n, r, ng = _stack_geometry(bt, c, nh)
    sub = HG_SUB

    @pl.when(j == 0)
    def _():
        st_scr[...] = s0_ref[...]

    u = _normed_input(h_ref, g_ref)
    q = _silu(jnp.dot(u, wq_ref[...], preferred_element_type=f32))
    lb = lb_ref[...]
    fg = lb + (1.0 - lb) * jax.nn.sigmoid(jnp.dot(u, wf_ref[...], preferred_element_type=f32))
    v = jnp.dot(u, wi_ref[...], preferred_element_type=f32)
    gt_scr[...] = jnp.dot(u, wg_ref[...], preferred_element_type=f32).reshape(bt, c, BRANCH_W)
    cum = _dot_sel_left(_blockdiag_ltri(bt * c, c), jnp.log(fg))
    k = 1.0 - fg
    for h in range(nh):
        sl = slice(h * HG_DK, (h + 1) * HG_DK)
        q_scr[:, h] = q[:, sl].reshape(bt, c, HG_DK)
        k_scr[:, h] = k[:, sl].reshape(bt, c, HG_DK)
        v_scr[:, h] = v[:, sl].reshape(bt, c, HG_DV)
        cum_scr[:, h] = cum[:, sl].reshape(bt, c, HG_DK)
    qf = q_scr[...].reshape(n, HG_DK)
    kf = k_scr[...].reshape(n, HG_DK)
    vf = v_scr[...].reshape(n, HG_DV)
    cf = cum_scr[...].reshape(n, HG_DK)

    ii, jj = _iota2(r, r, 0), _iota2(r, r, 1)
    scores = [jnp.zeros((r, r), f32) for _ in range(ng)]

    q3, k3, c3 = (a.reshape(n // sub, sub, HG_DK) for a in (qf, kf, cf))
    sub_i = lax.broadcasted_iota(jnp.int32, q3.shape, 1)
    ones = jnp.ones((HG_DK, r), bf16)
    same_sub = (ii >> _log2(sub)) == (jj >> _log2(sub))
    for t in range(sub):
        pt = jnp.where(sub_i >= t, q3 * k3[:, t:t + 1, :] * jnp.exp(c3 - c3[:, t:t + 1, :]), 0.0)
        row_sum = jnp.dot(pt.reshape(n, HG_DK).astype(bf16), ones, preferred_element_type=f32)
        place = same_sub & ((jj & (sub - 1)) == t)
        scores = [jnp.where(place, row_sum[g * r:(g + 1) * r], scores[g]) for g in range(ng)]

    m = sub
    while m < c:
        q4, k4, c4 = (a.reshape(n // (2 * m), 2, m, HG_DK) for a in (qf, kf, cf))
        bnd = c4[:, 0:1, m - 1:m, :]
        odd = lax.broadcasted_iota(jnp.int32, q4.shape, 1) == 1
        e = jnp.exp(jnp.where(odd, c4 - bnd, bnd - c4))
        q_t = jnp.where(odd, q4 * e, 0.0).reshape(n, HG_DK)
        k_t = jnp.where(odd, 0.0, k4 * e).reshape(n, HG_DK)
        same_pair = (ii >> _log2(2 * m)) == (jj >> _log2(2 * m))
        scores = [scores[g] + jnp.where(same_pair, _dot_nt(q_t[g * r:(g + 1) * r], k_t[g * r:(g + 1) * r]), 0.0)
                  for g in range(ng)]
        m *= 2

    cb3 = cf.reshape(n // c, c, HG_DK)
    clast3 = cb3[:, c - 1:c, :]
    q_in = qf * jnp.exp(cf)
    k_out = (kf.reshape(n // c, c, HG_DK) * jnp.exp(clast3 - cb3)).reshape(n, HG_DK)
    nblk = n // c
    e_rows = jnp.exp(cb3[:, c - 1, :])
    if nblk < HG_DK:
        e_rows = jnp.concatenate([e_rows, jnp.zeros((HG_DK - nblk, HG_DK), f32)], axis=0)
    e_cols = e_rows.T
    blocks = [(b, h) for b in range(bt) for h in range(nh)]
    rows = lambda a, i: a[i * c:(i + 1) * c]
    st_old = [st_scr[b, h] for b, h in blocks]
    o_state = [_dot(rows(q_in, i), st_old[i]) for i in range(len(blocks))]
    for i, (b, h) in enumerate(blocks):
        st_scr[b, h] = st_old[i] * e_cols[:, i:i + 1] + _dot_tn(rows(k_out, i), rows(vf, i))
    o = [_dot(scores[g], vf[g * r:(g + 1) * r]) for g in range(ng)]
    o = o[0] if ng == 1 else jnp.concatenate(o, axis=0)
    for b in range(bt):
        y = jnp.concatenate([_rms(rows(o, b * nh + h) + o_state[b * nh + h]) * nrm_ref[...] for h in range(nh)],
                            axis=-1)
        y_ref[b] = (y * _silu(gt_scr[b])).astype(bf16)

    @pl.when(j == pl.num_programs(1) - 1)
    def _():
        so_ref[...] = st_scr[...]


def _hgrn(h, g2, w, s0, c, bt, slot):
    consts = [g2, w["wq"], w["wf"], w["wi"], w["wg"], w["lb"], w["nrm"]]
    scratch = ([pltpu.VMEM((bt, HG_HEADS, HG_DK, HG_DV), f32)] + [pltpu.VMEM((bt, HG_HEADS, c, HG_DK), f32)] * 4
               + [pltpu.VMEM((bt, c, BRANCH_W), f32)])
    return _mixer_call(_hgrn_body, "hgrn", h, consts, [s0], c, bt, [], scratch, slot)


def _unit_lower_inverses(a_list, r, c):
    ii, jj = _iota2(r, r, 0), _iota2(r, r, 1)
    eye = jnp.where(ii == jj, 1.0, 0.0)
    t_list = [eye for _ in a_list]
    m = 1
    while m < c:
        lm = _log2(m)
        join = ((ii >> (lm + 1)) == (jj >> (lm + 1))) & (((ii >> lm) & 1) == 1) & (((jj >> lm) & 1) == 0)
        f_list = [jnp.where(join, a, 0.0) for a in a_list]
        if m == 1:
            t_list = [eye - f for f in f_list]
        else:
            ft_list = [_dot(f, t) for f, t in zip(f_list, t_list)]
            t_list = [t - _dot(t, ft) for t, ft in zip(t_list, ft_list)]
        m *= 2
    return t_list


def _gdn_body(h_ref, g_ref, wqkv_ref, wz_ref, wab_ref, cw_ref, dtb_ref, alog_ref, nrm_ref,
              s0_ref, c0_ref, y_ref, so_ref, co_ref, s_scr, xp_scr, z_scr, q_scr, k_scr, v_scr, cum_scr, beta_scr):
    bt, c, _ = h_ref.shape
    j = pl.program_id(1)
    nh = GDN_HEADS
    hk = nh * GDN_DK
    n, r, ng = _stack_geometry(bt, c, nh)

    @pl.when(j == 0)
    def _():
        s_scr[...] = s0_ref[...]
        xp_scr[:, 0:CONV_PAD, :] = c0_ref[...]

    u = _normed_input(h_ref, g_ref)
    xp_scr[:, CONV_PAD:CONV_PAD + c, :] = jnp.dot(u, wqkv_ref[...], preferred_element_type=f32).reshape(
        bt, c, GDN_CONV_DIM)
    z_scr[...] = jnp.dot(u, wz_ref[...], preferred_element_type=f32).reshape(bt, c, BRANCH_W)
    ab = jnp.dot(u, wab_ref[...], preferred_element_type=f32)
    log_g = -jnp.exp(alog_ref[...]) * jax.nn.softplus(ab + dtb_ref[...])
    cum = _dot_sel_left(_blockdiag_ltri(bt * c, c), log_g)
    beta = jax.nn.sigmoid(ab)
    qkv = _silu(_causal_conv(xp_scr, cw_ref, c))
    tail = xp_scr[:, c:c + CONV_PAD, :]
    xp_scr[:, 0:CONV_PAD, :] = tail
    co_ref[...] = tail
    lane_bcast = lambda col: jnp.broadcast_to(col, (bt * c, LANES)).reshape(bt, c, LANES)
    for h in range(nh):
        q = qkv[:, :, h * GDN_DK:(h + 1) * GDN_DK]
        k = qkv[:, :, hk + h * GDN_DK:hk + (h + 1) * GDN_DK]
        q_scr[:, h] = q * lax.rsqrt(jnp.sum(q * q, axis=-1, keepdims=True) + EPS) * (GDN_DK ** -0.5)
        k_scr[:, h] = k * lax.rsqrt(jnp.sum(k * k, axis=-1, keepdims=True) + EPS)
        v_scr[:, h] = qkv[:, :, 2 * hk + h * GDN_DV:2 * hk + (h + 1) * GDN_DV]
        cum_scr[:, h] = lane_bcast(cum[:, h:h + 1])
        beta_scr[:, h] = lane_bcast(beta[:, nh + h:nh + h + 1])

    grp = lambda ref, g: _group_rows(ref, g, r)
    groups = range(ng)
    qs = [grp(q_scr, g) for g in groups]
    ks = [grp(k_scr, g) for g in groups]
    vs = [grp(v_scr, g) for g in groups]
    cb = [grp(cum_scr, g) for g in groups]
    bb = [grp(beta_scr, g) for g in groups]
    ii, jj = _iota2(r, r, 0), _iota2(r, r, 1)
    same = (ii >> _log2(c)) == (jj >> _log2(c))
    incl = same & (ii >= jj)
    strict = same & (ii > jj)
    dec = [jnp.exp(jnp.where(incl, _widen(cb[g], r) - cb[g].T[0:1, :], -jnp.inf)) for g in groups]
    a_mat = [jnp.where(strict, _widen(bb[g], r) * dec[g] * _dot_nt(ks[g], ks[g]), 0.0) for g in groups]
    t_inv = _unit_lower_inverses(a_mat, r, c)
    e_in = [jnp.exp(cb[g]) for g in groups]
    sol = [_dot(t_inv[g], jnp.concatenate([vs[g] * bb[g], ks[g] * (e_in[g] * bb[g])], axis=-1)) for g in groups]
    scd = [_dot_nt(qs[g], ks[g]) * dec[g] for g in groups]
    q_in = [qs[g] * e_in[g] for g in groups]

    rows = lambda a, i: a[i * c:(i + 1) * c]
    nb = r // c
    o_all = []
    for g in groups:
        blocks = [divmod(g * nb + i, nh) for i in range(nb)]
        s_old = [s_scr[b, h] for b, h in blocks]
        proj = [_dot(jnp.concatenate([rows(sol[g], i)[:, GDN_DV:], rows(q_in[g], i)], axis=0), s_old[i])
                for i in range(nb)]
        w_new = [rows(sol[g], i)[:, :GDN_DV] - proj[i][:c] for i in range(nb)]
        for i, (b, h) in enumerate(blocks):
            cum_i = rows(cb[g], i)
            clast = cum_i[c - 1:c]
            s_scr[b, h] = s_old[i] * jnp.exp(clast) + _dot_tn(rows(ks[g], i) * jnp.exp(clast - cum_i), w_new[i])
        o_all.append(_dot(scd[g], jnp.concatenate(w_new, axis=0)) + jnp.concatenate([p[c:] for p in proj], axis=0))
    o = o_all[0] if ng == 1 else jnp.concatenate(o_all, axis=0)
    for b in range(bt):
        y = jnp.concatenate([_rms(rows(o, b * nh + h)) * nrm_ref[...] for h in range(nh)], axis=-1)
        y_ref[b] = (y * _silu(z_scr[b])).astype(bf16)

    @pl.when(j == pl.num_programs(1) - 1)
    def _():
        so_ref[...] = s_scr[...]


def _gdn(h, g2, w, s0, conv0, c, bt, slot):
    b, l, _ = h.shape
    consts = [g2, w["wqkv"], w["wz"], w["wab"], w["cw"], w["dtb"], w["alog"], w["nrm"]]
    scratch = [
        pltpu.VMEM((bt, GDN_HEADS, GDN_DK, GDN_DV), f32),
        pltpu.VMEM((bt, CONV_PAD + c, GDN_CONV_DIM), f32),
        pltpu.VMEM((bt, c, BRANCH_W), f32),
    ] + [pltpu.VMEM((bt, GDN_HEADS, c, LANES), f32)] * 5
    return _mixer_call(_gdn_body, "gdn", h, consts, [s0, conv0], c, bt,
                       [(b, CONV_PAD, GDN_CONV_DIM)], scratch, slot)


def _pad_lanes(v, offset=0):
    row = jnp.zeros((1, LANES), f32)
    return row.at[0, offset:offset + v.shape[0]].set(v.astype(f32))


def _layer_weights(l, norm_g, ffn_w_in, ffn_w_out, w_in, w_branch, w_out, mb_conv_w, mb_conv_b, mb_a_log,
                   mb_dt_bias, mb_d, mb_norm, lb_all, hg_norm, gdn_conv_w, gdn_a_log, gdn_dt_bias, gdn_norm):
    wi = w_in[l]
    col = lambda n: wi[:, IN_OFFSETS[n]:IN_OFFSETS[n + 1]]
    colb = lambda n: col(n).astype(bf16)
    pad_cols = lambda a: jnp.pad(a, ((0, 0), (0, LANES - a.shape[1]))).astype(bf16)
    hw = RET_DK // 2
    perm = np.arange(RET_HEADS * RET_DK).reshape(RET_HEADS, 2, hw).transpose(1, 0, 2).reshape(-1)
    row = lambda v: v.astype(f32).reshape(1, -1)
    return {
        "g_ffn1": norm_g[l, 0:2], "g_mix": norm_g[l, 2:4], "g_ffn2": norm_g[l, 4:6], "g_u": norm_g[l, 2:3],
        "layer": l, "ffn_in": ffn_w_in, "ffn_out": ffn_w_out,
        "mamba": {"wz": colb(0), "wx": colb(1), "wdt": pad_cols(col(2)), "cw": mb_conv_w[l],
                  "cb": row(mb_conv_b[l]),
                  "dtb": _pad_lanes(mb_dt_bias[l]), "alog": _pad_lanes(mb_a_log[l]),
                  "d": row(jnp.repeat(mb_d[l], MB_HEADDIM)), "nrm": row(mb_norm[l])},
        "hgrn": {"wq": colb(3), "wf": colb(4), "wi": colb(5), "wg": colb(6), "lb": row(lb_all[l]),
                 "nrm": row(hg_norm[l])},
        "ret": {"wq": col(7)[:, perm].astype(bf16), "wk": col(8)[:, perm].astype(bf16),
                "wv": colb(9), "wg": colb(10)},
        "gdn": {"wqkv": colb(11), "wz": colb(12),
                "wab": pad_cols(jnp.concatenate([col(13), col(14)], axis=1)),
                "cw": gdn_conv_w[l], "dtb": _pad_lanes(gdn_dt_bias[l]), "alog": _pad_lanes(gdn_a_log[l]),
                "nrm": row(gdn_norm[l])},
        "w_gate": colb(15), "w_branch": w_branch[l].astype(bf16), "w_out": w_out[l].astype(bf16),
    }


def _rope_tables(pos):
    half = RET_DK // 2
    inv_freq = 1.0 / (ROPE_BASE ** jnp.linspace(0.0, 1.0, half, dtype=f32))
    ang = pos.astype(f32)[:, None] * inv_freq[None, :]
    reps = LANES // half
    return jnp.tile(jnp.cos(ang), (1, reps)), jnp.tile(jnp.sin(ang), (1, reps))


def _pad_conv(buf):
    return jnp.pad(buf, ((0, 0), (CONV_PAD - (CONV_W - 1), 0), (0, 0)))


def _layer(h, w, states, rope, lg, chunks, bt, depth, stacked):
    b, l, d = h.shape
    s_ssm, s_ssm_conv, s_hg, s_ret, s_gdn, s_gdn_conv = states
    slot = lambda n: (w["layer"], depth, None if stacked is None else stacked[n])
    h = _ffn(h.reshape(b * l, d), w["g_ffn1"], w["ffn_in"], w["ffn_out"], w["layer"], 0).reshape(b, l, d)
    y_mb, n_ssm, n_ssm_conv = _mamba(h, w["g_u"], w["mamba"], s_ssm, _pad_conv(s_ssm_conv),
                                     chunks["mamba"], bt, slot(0))
    y_hg, n_hg = _hgrn(h, w["g_u"], w["hgrn"], s_hg, chunks["hgrn"], bt, slot(1))
    y_rt, n_ret = _ret(h, w["g_u"], w["ret"], rope[0], rope[1], lg, s_ret, chunks["ret"], bt, slot(2))
    y_gd, n_gdn, n_gdn_conv = _gdn(h, w["g_u"], w["gdn"], s_gdn, _pad_conv(s_gdn_conv), chunks["gdn"], bt, slot(3))
    ys = [y.reshape(b * l, BRANCH_W) for y in (y_mb, y_hg, y_rt, y_gd)]
    hf = _merge(h.reshape(b * l, d), w["g_mix"], ys, w["w_gate"], w["w_branch"], w["w_out"])
    hf = _ffn(hf, w["g_ffn2"], w["ffn_in"], w["ffn_out"], w["layer"], 1)
    convs = (n_ssm_conv[:, CONV_PAD - (CONV_W - 1):], n_gdn_conv[:, CONV_PAD - (CONV_W - 1):])
    return hf.reshape(b, l, d), (n_ssm, n_hg, n_ret, n_gdn), convs


def _chunks(l):
    pick = lambda pref: pref if l % pref == 0 else math.gcd(l, pref)
    return {"mamba": pick(128), "ret": pick(128), "hgrn": pick(64), "gdn": pick(64)}


def kernel(x_prompt, x_sample, state_ssm, state_ssm_conv, state_hgrn, state_ret, state_gdn, state_gdn_conv,
           norm_g, ffn_w_in, ffn_w_out, w_in, w_branch, w_out, mb_conv_w, mb_conv_b, mb_a_log, mb_dt_bias, mb_d,
           mb_norm, hg_lb_logits, hg_norm, gdn_conv_w, gdn_a_log, gdn_dt_bias, gdn_norm):
    depth = norm_g.shape[0]
    lb_all = jnp.cumsum(jax.nn.softmax(hg_lb_logits.astype(f32), axis=0), axis=0)
    lb_all = lb_all - lb_all[0:1]
    lg = _pad_lanes(jnp.log(1.0 - jnp.exp2(-5.0 - jnp.arange(RET_HEADS, dtype=f32))))

    bp, lp, _ = x_prompt.shape
    bs, ls, _ = x_sample.shape
    rope_p = _rope_tables(jnp.arange(lp))
    rope_s = _rope_tables(PAST_LEN + jnp.arange(ls))
    zero_states = (
        (jnp.zeros((1, bp, MB_HEADS, MB_STATE, MB_HEADDIM), f32), 0),
        jnp.zeros((bp, CONV_W - 1, MB_CONV_DIM), f32),
        (jnp.zeros((1, bp, HG_HEADS, HG_DK, HG_DV), f32), 0),
        (jnp.zeros((1, bp, RET_HEADS, RET_DK, RET_DV), f32), 0),
        (jnp.zeros((1, bp, GDN_HEADS, GDN_DK, GDN_DV), f32), 0),
        jnp.zeros((bp, CONV_W - 1, GDN_CONV_DIM), f32),
    )
    bt_p = math.gcd(bp, 8)
    bt_s = math.gcd(bs, 16)

    ffn_w_in = ffn_w_in.astype(bf16)
    ffn_w_out = ffn_w_out.astype(bf16)
    hp, hs = x_prompt, x_sample
    st_p = st_s = None
    cv_p, cv_s = [], []
    for l in range(depth):
        w = _layer_weights(l, norm_g, ffn_w_in, ffn_w_out, w_in, w_branch, w_out, mb_conv_w, mb_conv_b,
                           mb_a_log, mb_dt_bias, mb_d, mb_norm, lb_all, hg_norm, gdn_conv_w, gdn_a_log,
                           gdn_dt_bias, gdn_norm)
        hp, st_p, cp = _layer(hp, w, zero_states, rope_p, lg, _chunks(lp), bt_p, depth, st_p)
        past = ((state_ssm, l), state_ssm_conv[l], (state_hgrn, l), (state_ret, l), (state_gdn, l), state_gdn_conv[l])
        hs, st_s, cs = _layer(hs, w, past, rope_s, lg, _chunks(ls), bt_s, depth, st_s)
        cv_p.append(cp)
        cv_s.append(cs)

    stack = lambda cvs, i: jnp.stack([cv[i] for cv in cvs], axis=0)
    return (hp, hs,
            st_p[0], stack(cv_p, 0), st_p[1], st_p[2], st_p[3], stack(cv_p, 1),
            st_s[0], stack(cv_s, 0), st_s[1], st_s[2], st_s[3], stack(cv_s, 1))
```

```python
import functools
import math

import numpy as np
import jax
import jax.numpy as jnp
from jax import lax
from jax.experimental import pallas as pl
from jax.experimental.pallas import tpu as pltpu

f32 = jnp.float32
bf16 = jnp.bfloat16

D_MODEL = 1024
D_FF = 2816
BRANCH_W = 512
N_BRANCH = 4
CONV_W = 4
EPS = 1e-6
ROPE_BASE = 10000.0
PAST_LEN = 16384

MB_HEADS, MB_HEADDIM, MB_STATE, MB_GROUPS = 8, 64, 64, 2
MB_CONV_DIM = BRANCH_W + 2 * MB_GROUPS * MB_STATE
HG_HEADS, HG_DK, HG_DV = 4, 128, 128
RET_HEADS, RET_DK, RET_DV = 4, 64, 128
GDN_HEADS, GDN_DK, GDN_DV = 4, 128, 128
GDN_CONV_DIM = 2 * GDN_HEADS * GDN_DK + GDN_HEADS * GDN_DV

IN_SPLITS = (
    BRANCH_W, MB_CONV_DIM, MB_HEADS,
    HG_HEADS * HG_DK, HG_HEADS * HG_DK, BRANCH_W, BRANCH_W,
    RET_HEADS * RET_DK, RET_HEADS * RET_DK, BRANCH_W, BRANCH_W,
    GDN_CONV_DIM, BRANCH_W, GDN_HEADS, GDN_HEADS,
    N_BRANCH * D_MODEL,
)
IN_OFFSETS = tuple(int(v) for v in np.cumsum((0,) + IN_SPLITS))

LANES = 128
SUBLANES = 8
MXU_DIM = 256
STACK_ROWS = 128
CONV_PAD = SUBLANES
VMEM_LIMIT = 56 * 1024 * 1024
FFN_TM = 512
MERGE_TM = 512
HG_SUB = SUBLANES


def _dot(a, b):
    return jnp.dot(a.astype(bf16), b.astype(bf16), preferred_element_type=f32)


def _dot_nt(a, b):
    return lax.dot_general(a.astype(bf16), b.astype(bf16), (((1,), (1,)), ((), ())), preferred_element_type=f32)


def _dot_tn(a, b):
    return lax.dot_general(a.astype(bf16), b.astype(bf16), (((0,), (0,)), ((), ())), preferred_element_type=f32)


def _split3(x):
    hi = x.astype(bf16)
    r = x - hi.astype(f32)
    mid = r.astype(bf16)
    lo = (r - mid.astype(f32)).astype(bf16)
    return hi, mid, lo


def _dot_sel_left(m01, x):
    hi, mid, lo = _split3(x)
    return (jnp.dot(m01, hi, preferred_element_type=f32) + jnp.dot(m01, mid, preferred_element_type=f32)
            + jnp.dot(m01, lo, preferred_element_type=f32))


def _rms(x):
    return x * lax.rsqrt(jnp.mean(x * x, axis=-1, keepdims=True) + EPS)


def _silu(x):
    return x * jax.nn.sigmoid(x)


def _iota2(n, m, axis):
    return lax.broadcasted_iota(jnp.int32, (n, m), axis)


def _log2(n):
    assert n > 0 and n & (n - 1) == 0, n
    return n.bit_length() - 1


def _ltri(c):
    return jnp.where(_iota2(c, c, 0) >= _iota2(c, c, 1), 1.0, 0.0).astype(bf16)


def _blockdiag_ltri(n, c):
    ii, jj = _iota2(n, n, 0), _iota2(n, n, 1)
    lc = _log2(c)
    return jnp.where(((ii >> lc) == (jj >> lc)) & (ii >= jj), 1.0, 0.0).astype(bf16)


def _row_bcast(col, c):
    dg = jnp.where(_iota2(c, c, 0) == _iota2(c, c, 1), jnp.broadcast_to(col, (c, c)), 0.0)
    return _dot_sel_left(jnp.ones((c, c), bf16), dg)


def _decay_matrix(col, c):
    diff = col - _row_bcast(col, c)
    return jnp.exp(jnp.where(_iota2(c, c, 0) >= _iota2(c, c, 1), diff, -jnp.inf))


def _causal_conv(xp_ref, w_ref, c):
    y = xp_ref[:, pl.ds(CONV_PAD - 3, c), :] * w_ref[0:1, :]
    for j in range(1, CONV_W):
        y = y + xp_ref[:, pl.ds(CONV_PAD - 3 + j, c), :] * w_ref[j:j + 1, :]
    return y


def _stack_geometry(bt, c, heads):
    n = bt * heads * c
    r = min(STACK_ROWS, n)
    assert n % r == 0 and r % c == 0 and r % LANES == 0
    return n, r, n // r


def _group_rows(ref, g, r):
    _, heads, c, w = ref.shape
    bpg = r // c
    if bpg <= heads:
        assert heads % bpg == 0
        b, h0 = divmod(g * bpg, heads)
        return ref[b, h0:h0 + bpg].reshape(r, w)
    assert bpg % heads == 0
    nseq = bpg // heads
    return ref[g * nseq:(g + 1) * nseq].reshape(r, w)


def _widen(x, r):
    return x if r == LANES else jnp.concatenate([x] * (r // LANES), axis=-1)


def _ffn_body(x_ref, g_ref, wg_ref, wu_ref, wo_ref, o_ref, *un_ref):
    x = x_ref[...]
    xn = (_rms(x) * g_ref[0:1, :]).astype(bf16)
    gate = jnp.dot(xn, wg_ref[...], preferred_element_type=f32)
    up = jnp.dot(xn, wu_ref[...], preferred_element_type=f32)
    act = (_silu(gate) * up).astype(bf16)
    y = jnp.dot(act, wo_ref[...], preferred_element_type=f32)
    out = x + 0.5 * (_rms(y) * g_ref[1:2, :])
    o_ref[...] = out
    if un_ref:
        un_ref[0][...] = (_rms(out) * g_ref[2:3, :]).astype(bf16)


def _ffn(x, g, w_in, w_out, l, k, emit_normed=False):
    t = x.shape[0]
    tm = min(FFN_TM, t)
    resident = pl.Buffered(1)
    row = pl.BlockSpec((tm, D_MODEL), lambda i: (i, 0))
    out_specs, out_shape = row, jax.ShapeDtypeStruct((t, D_MODEL), f32)
    if emit_normed:
        out_specs, out_shape = [row, row], [out_shape, jax.ShapeDtypeStruct((t, D_MODEL), bf16)]
    return pl.pallas_call(
        _ffn_body,
        grid=(t // tm,),
        in_specs=[
            row,
            pl.BlockSpec(g.shape, lambda i: (0, 0)),
            pl.BlockSpec((None, None, D_MODEL, D_FF), lambda i: (l, k, 0, 0), pipeline_mode=resident),
            pl.BlockSpec((None, None, D_MODEL, D_FF), lambda i: (l, k, 0, 1), pipeline_mode=resident),
            pl.BlockSpec((None, None, D_FF, D_MODEL), lambda i: (l, k, 0, 0), pipeline_mode=resident),
        ],
        out_specs=out_specs,
        out_shape=out_shape,
        compiler_params=pltpu.CompilerParams(dimension_semantics=("arbitrary",), vmem_limit_bytes=VMEM_LIMIT),
        name="ffn",
    )(x, g, w_in, w_in, w_out)


def _merge_body(h_ref, u_ref, g_ref, y0_ref, y1_ref, y2_ref, y3_ref, wg_ref, wb_ref, wo_ref, o_ref):
    h = h_ref[...]
    u = u_ref[...]
    merged = jnp.zeros(h.shape, f32)
    for n, y_ref in enumerate((y0_ref, y1_ref, y2_ref, y3_ref)):
        gate = jax.nn.sigmoid(jnp.dot(u, wg_ref[:, n * D_MODEL:(n + 1) * D_MODEL], preferred_element_type=f32))
        merged = merged + gate * jnp.dot(y_ref[...], wb_ref[n], preferred_element_type=f32)
    mixed = jnp.dot(merged.astype(bf16), wo_ref[...], preferred_element_type=f32)
    o_ref[...] = h + _rms(mixed) * g_ref[...]


def _merge(h, u, g_post, ys, w_gate, w_branch, w_out):
    t = h.shape[0]
    tm = min(MERGE_TM, t)
    row = lambda i: (i, 0)
    resident = pl.Buffered(1)
    return pl.pallas_call(
        _merge_body,
        grid=(t // tm,),
        in_specs=[pl.BlockSpec((tm, D_MODEL), row), pl.BlockSpec((tm, D_MODEL), row),
                  pl.BlockSpec((1, D_MODEL), lambda i: (0, 0))]
                 + [pl.BlockSpec((tm, BRANCH_W), row)] * N_BRANCH
                 + [pl.BlockSpec((D_MODEL, N_BRANCH * D_MODEL), lambda i: (0, 0), pipeline_mode=resident),
                    pl.BlockSpec((N_BRANCH, BRANCH_W, D_MODEL), lambda i: (0, 0, 0), pipeline_mode=resident),
                    pl.BlockSpec((D_MODEL, D_MODEL), lambda i: (0, 0), pipeline_mode=resident)],
        out_specs=pl.BlockSpec((tm, D_MODEL), row),
        out_shape=jax.ShapeDtypeStruct((t, D_MODEL), f32),
        compiler_params=pltpu.CompilerParams(dimension_semantics=("arbitrary",), vmem_limit_bytes=VMEM_LIMIT),
        name="merge",
    )(h, u, g_post, *ys, w_gate, w_branch, w_out)


def _const_spec(a):
    nd = a.ndim
    return pl.BlockSpec(a.shape, lambda i, j, _nd=nd: (0,) * _nd)


def _batch_spec(bt, a):
    nd = a.ndim
    return pl.BlockSpec((bt,) + a.shape[1:], lambda i, j, _nd=nd: (i,) + (0,) * (_nd - 1))


def _mixer_call(body, name, u, consts, batch_ins, c, bt, batch_outs, scratch, slot, chunk_ins=()):
    b, l, d = u.shape
    layer, depth, prev = slot
    seq = lambda i, j: (i, j, 0)
    if c == l:
        u, u_spec = u.reshape(b * l, d), pl.BlockSpec((bt * c, d), lambda i, j: (i, 0))
    else:
        u_spec = pl.BlockSpec((bt, c, d), seq)
    st_in, st_entry = batch_ins[0]
    st = tuple(st_in.shape[1:])
    st_block = (None, bt) + st[1:]
    out_shape = ([jax.ShapeDtypeStruct((b, l, BRANCH_W), bf16), jax.ShapeDtypeStruct((depth,) + st, f32)]
                 + [jax.ShapeDtypeStruct(s, f32) for s in batch_outs])
    out_specs = ([pl.BlockSpec((bt, c, BRANCH_W), seq),
                  pl.BlockSpec(st_block, lambda i, j: (layer, i) + (0,) * (len(st) - 1))]
                 + [pl.BlockSpec((bt,) + s[1:], lambda i, j, _nd=len(s): (i,) + (0,) * (_nd - 1))
                    for s in batch_outs])
    operands = [u, *consts, *chunk_ins, st_in, *batch_ins[1:]]
    in_specs = ([u_spec] + [_const_spec(a) for a in consts]
                + [pl.BlockSpec((c, a.shape[1]), lambda i, j: (j, 0)) for a in chunk_ins]
                + [pl.BlockSpec(st_block, lambda i, j: (st_entry, i) + (0,) * (len(st) - 1))]
                + [_batch_spec(bt, a) for a in batch_ins[1:]])
    n_in = len(operands)
    aliases = {}
    fn = body
    if prev is not None:
        operands.append(prev)
        in_specs.append(pl.BlockSpec(memory_space=pl.ANY))
        aliases = {n_in: 1}
        fn = lambda *refs: body(*refs[:n_in], *refs[n_in + 1:])
    elif depth > 1:
        out_specs[1] = pl.BlockSpec((depth, bt) + st[1:], lambda i, j: (0, i) + (0,) * (len(st) - 1))

        def fn(*refs):
            so_ref = refs[n_in + 1]

            @pl.when(pl.program_id(1) == 0)
            def _():
                for e in range(depth):
                    if e != layer:
                        so_ref[e] = jnp.zeros(so_ref.shape[1:], f32)

            body(*refs[:n_in + 1], so_ref.at[layer], *refs[n_in + 2:])

    return pl.pallas_call(
        fn,
        grid=(b // bt, l // c),
        in_specs=in_specs,
        out_specs=out_specs,
        out_shape=out_shape,
        scratch_shapes=scratch,
        input_output_aliases=aliases,
        compiler_params=pltpu.CompilerParams(dimension_semantics=("arbitrary", "arbitrary"),
                                             vmem_limit_bytes=VMEM_LIMIT),
        name=name,
    )(*operands)


def _mamba_body(u_ref, wz_ref, wx_ref, wdt_ref, cw_ref, cb_ref, dtb_ref, alog_ref, d_ref, nrm_ref,
                s0_ref, c0_ref, y_ref, so_ref, co_ref, s_scr, xp_scr, z_scr, xs_scr, v_scr, cum_scr):
    bt, c, _ = y_ref.shape
    j = pl.program_id(1)
    nh = MB_HEADS
    npair = nh // 2
    hpg = nh // MB_GROUPS
    n = bt * nh * c
    r = min(MXU_DIM, n)
    assert n % r == 0 and r % c == 0
    ng = n // r

    @pl.when(j == 0)
    def _():
        s_scr[...] = jnp.zeros_like(s_scr)
        for p in range(npair):
            g = (2 * p) // hpg
            s_scr[:, p, g * MB_STATE:(g + 1) * MB_STATE, :] = jnp.concatenate(
                [s0_ref[:, 2 * p], s0_ref[:, 2 * p + 1]], axis=-1)
        xp_scr[:, 0:CONV_PAD, :] = c0_ref[...]

    u = u_ref[...].reshape(bt * c, D_MODEL)
    z_scr[...] = jnp.dot(u, wz_ref[...], preferred_element_type=f32).reshape(bt, c, BRANCH_W)
    xp_scr[:, CONV_PAD:CONV_PAD + c, :] = jnp.dot(u, wx_ref[...], preferred_element_type=f32).reshape(
        bt, c, MB_CONV_DIM)
    dt = jax.nn.softplus(jnp.dot(u, wdt_ref[...], preferred_element_type=f32) + dtb_ref[...])
    cum = _dot_sel_left(_blockdiag_ltri(bt * c, c), -jnp.exp(alog_ref[...]) * dt)
    xa = _silu(_causal_conv(xp_scr, cw_ref, c) + cb_ref[...])
    tail = xp_scr[:, c:c + CONV_PAD, :]
    xp_scr[:, 0:CONV_PAD, :] = tail
    co_ref[...] = tail
    xs = xa[:, :, :BRANCH_W]
    bm = xa[:, :, BRANCH_W:BRANCH_W + LANES]
    cm = xa[:, :, BRANCH_W + LANES:BRANCH_W + 2 * LANES]
    xs_scr[...] = xs
    lane3 = lax.broadcasted_iota(jnp.int32, (bt, c, LANES), 2)
    lane_bcast = lambda col: jnp.broadcast_to(col, (bt * c, LANES)).reshape(bt, c, LANES)
    for h in range(nh):
        own_half = (lane3 >> _log2(MB_HEADDIM)) == (h % 2)
        v_scr[:, h] = jnp.where(own_half, xs[:, :, (h // 2) * LANES:(h // 2 + 1) * LANES] * lane_bcast(dt[:, h:h + 1]),
                                0.0)
        cum_scr[:, h] = lane_bcast(cum[:, h:h + 1])
    vf = v_scr[...].reshape(n, LANES)
    cbf = cum_scr[...].reshape(n, LANES)
    cm_g = [jnp.where((lane3 >> _log2(MB_STATE)) == g, cm, 0.0) for g in range(MB_GROUPS)]
    blocks_of = lambda first, count: [divmod(first + i, nh) for i in range(count)]
    c_rows = lambda blocks: jnp.concatenate([cm_g[h // hpg][b] for b, h in blocks], axis=0)
    b_rows = lambda blocks: jnp.concatenate([bm[b] for b, _ in blocks], axis=0)

    ii, jj = _iota2(r, r, 0), _iota2(r, r, 1)
    incl = ((ii >> _log2(c)) == (jj >> _log2(c))) & (ii >= jj)
    grp = lambda a, g: a[g * r:(g + 1) * r]
    gblocks = [blocks_of(g * (r // c), r // c) for g in range(ng)]
    dec = [jnp.exp(jnp.where(incl, _widen(grp(cbf, g), r) - grp(cbf, g).T[0:1, :], -jnp.inf)) for g in range(ng)]
    o_loc = [_dot(_dot_nt(c_rows(gblocks[g]), b_rows(gblocks[g])) * dec[g], grp(vf, g)) for g in range(ng)]
    o_loc = o_loc[0] if ng == 1 else jnp.concatenate(o_loc, axis=0)

    cb3 = cbf.reshape(n // c, c, LANES)
    k_out = (vf.reshape(n // c, c, LANES) * jnp.exp(cb3[:, c - 1:c, :] - cb3)).reshape(n, LANES)
    e_in = jnp.exp(cbf)
    lo_half = _iota2(1, LANES, 1) < MB_HEADDIM
    row_lo = _iota2(2 * c, LANES, 0) < c
    own = row_lo == (_iota2(2 * c, LANES, 1) < MB_HEADDIM)
    pairs = [(b, p) for b in range(bt) for p in range(npair)]
    prow = lambda a, i: a[i * 2 * c:(i + 1) * 2 * c]
    s_old = [s_scr[b, p] for b, p in pairs]
    pblocks = [blocks_of(i * 2, 2) for i in range(len(pairs))]
    o_st = [jnp.where(own, _dot(c_rows(pblocks[i]), s_old[i]) * prow(e_in, i), 0.0) for i in range(len(pairs))]
    for i, (b, p) in enumerate(pairs):
        cum_i = prow(cbf, i)
        e_all = jnp.exp(jnp.where(lo_half, cum_i[c - 1:c], cum_i[2 * c - 1:2 * c]))
        s_scr[b, p] = s_old[i] * e_all + _dot_tn(b_rows(pblocks[i]), prow(k_out, i))
    for b in range(bt):
        ys = []
        for p in range(npair):
            o2 = prow(o_loc, b * npair + p) + o_st[b * npair + p]
            sl = slice(p * LANES, (p + 1) * LANES)
            ys.append(o2[:c] + o2[c:] + d_ref[:, sl] * xs_scr[b, :, sl])
        y = jnp.concatenate(ys, axis=-1) * _silu(z_scr[b])
        y_ref[b] = (_rms(y) * nrm_ref[...]).astype(bf16)

    @pl.when(j == pl.num_programs(1) - 1)
    def _():
        for p in range(npair):
            g = (2 * p) // hpg
            so_ref[:, 2 * p] = s_scr[:, p, g * MB_STATE:(g + 1) * MB_STATE, 0:MB_HEADDIM]
            so_ref[:, 2 * p + 1] = s_scr[:, p, g * MB_STATE:(g + 1) * MB_STATE, MB_HEADDIM:2 * MB_HEADDIM]


def _mamba(u, w, s0, conv0, c, bt, slot):
    b, l, _ = u.shape
    npair = MB_HEADS // 2
    consts = [w["wz"], w["wx"], w["wdt"], w["cw"], w["cb"], w["dtb"], w["alog"], w["d"], w["nrm"]]
    scratch = [
        pltpu.VMEM((bt, npair, LANES, LANES), f32),
        pltpu.VMEM((bt, CONV_PAD + c, MB_CONV_DIM), f32),
        pltpu.VMEM((bt, c, BRANCH_W), f32),
        pltpu.VMEM((bt, c, BRANCH_W), f32),
    ] + [pltpu.VMEM((bt, MB_HEADS, c, LANES), f32)] * 2
    return _mixer_call(_mamba_body, "mamba", u, consts, [s0, conv0], c, bt,
                       [(b, CONV_PAD, MB_CONV_DIM)], scratch, slot)


def _ret_body(u_ref, wq_ref, wk_ref, wv_ref, wg_ref, lg_ref, cos_ref, sin_ref,
              s0_ref, y_ref, so_ref, s_scr, q_scr, k_scr, v_scr, gt_scr):
    bt, c, _ = y_ref.shape
    j = pl.program_id(1)
    kw = RET_HEADS * RET_DK
    hw = RET_DK // 2

    @pl.when(j == 0)
    def _():
        s_scr[...] = jnp.zeros_like(s_scr)
        for h in range(RET_HEADS):
            s_scr[:, h, h * hw:(h + 1) * hw, :] = s0_ref[:, h, 0:hw, :]
            s_scr[:, h, LANES + h * hw:LANES + (h + 1) * hw, :] = s0_ref[:, h, hw:2 * hw, :]

    u = u_ref[...].reshape(bt * c, D_MODEL)
    cos = cos_ref[...]
    sin = sin_ref[...]

    def rot(x):
        x = x.reshape(bt, c, kw)
        x1, x2 = x[:, :, :LANES], x[:, :, LANES:]
        return jnp.concatenate([x1 * cos - x2 * sin, x1 * sin + x2 * cos], axis=-1)

    nh = RET_HEADS
    n = bt * nh * c
    r = min(MXU_DIM, n)
    assert n % r == 0 and r % c == 0
    ng = n // r
    q = rot(jnp.dot(u, wq_ref[...], preferred_element_type=f32))
    k = rot(jnp.dot(u, wk_ref[...], preferred_element_type=f32)) * (RET_DK ** -0.5)
    v = jnp.dot(u, wv_ref[...], preferred_element_type=f32).reshape(bt, c, BRANCH_W)
    gt_scr[...] = jnp.dot(u, wg_ref[...], preferred_element_type=f32).reshape(bt, c, BRANCH_W)
    klane = lax.broadcasted_iota(jnp.int32, (bt, c, kw), 2)
    for h in range(nh):
        own = ((klane & (LANES - 1)) >> _log2(hw)) == h
        q_scr[:, h] = jnp.where(own, q, 0.0)
        k_scr[:, h] = jnp.where(own, k, 0.0)
        v_scr[:, h] = v[:, :, h * RET_DV:(h + 1) * RET_DV]
    qf = q_scr[...].reshape(n, kw)
    kf = k_scr[...].reshape(n, kw)
    vf = v_scr[...].reshape(n, RET_DV)

    row = _iota2(n, 1, 0)
    head = (row >> _log2(c)) & (nh - 1)
    pos = (row & (c - 1)).astype(f32)
    lg_row = jnp.zeros((n, 1), f32)
    for h in range(nh):
        lg_row = jnp.where(head == h, lg_ref[:, h:h + 1], lg_row)
    q_in = qf * jnp.exp((pos + 1.0) * lg_row)
    k_out = kf * jnp.exp((float(c) - 1.0 - pos) * lg_row)
    e_all = jnp.exp(float(c) * lg_row)

    ii, jj = _iota2(r, r, 0), _iota2(r, r, 1)
    incl = ((ii >> _log2(c)) == (jj >> _log2(c))) & (ii >= jj)
    dist = (ii - jj).astype(f32)
    grp = lambda a, g: a[g * r:(g + 1) * r]
    dec = {}
    for g in range(ng):
        key = (g * r) % (nh * c)
        if key not in dec:
            dec[key] = jnp.exp(jnp.where(incl, dist * grp(lg_row, g), -jnp.inf))
    o_loc = [_dot(_dot_nt(grp(qf, g), grp(kf, g)) * dec[(g * r) % (nh * c)], grp(vf, g)) for g in range(ng)]
    o_loc = o_loc[0] if ng == 1 else jnp.concatenate(o_loc, axis=0)

    blocks = [(b, h) for b in range(bt) for h in range(nh)]
    rows = lambda a, i: a[i * c:(i + 1) * c]
    s_old = [s_scr[b, h] for b, h in blocks]
    o_st = [_dot(rows(q_in, i), s_old[i]) for i in range(len(blocks))]
    for i, (b, h) in enumerate(blocks):
        s_scr[b, h] = s_old[i] * rows(e_all, i)[0:1] + _dot_tn(rows(k_out, i), rows(vf, i))
    for b in range(bt):
        y = jnp.concatenate([_rms(rows(o_loc, b * nh + h) + o_st[b * nh + h]) for h in range(nh)], axis=-1)
        y_ref[b] = (y * _silu(gt_scr[b])).astype(bf16)

    @pl.when(j == pl.num_programs(1) - 1)
    def _():
        for h in range(RET_HEADS):
            so_ref[:, h, 0:hw, :] = s_scr[:, h, h * hw:(h + 1) * hw, :]
            so_ref[:, h, hw:2 * hw, :] = s_scr[:, h, LANES + h * hw:LANES + (h + 1) * hw, :]


def _ret(u, w, cos, sin, lg, s0, c, bt, slot):
    kw = RET_HEADS * RET_DK
    consts = [w["wq"], w["wk"], w["wv"], w["wg"], lg]
    scratch = [
        pltpu.VMEM((bt, RET_HEADS, kw, RET_DV), f32),
        pltpu.VMEM((bt, RET_HEADS, c, kw), f32),
        pltpu.VMEM((bt, RET_HEADS, c, kw), f32),
        pltpu.VMEM((bt, RET_HEADS, c, RET_DV), f32),
        pltpu.VMEM((bt, c, BRANCH_W), f32),
    ]
    return _mixer_call(_ret_body, "ret", u, consts, [s0], c, bt, [], scratch, slot, chunk_ins=(cos, sin))


def _hgrn_body(u_ref, wq_ref, wf_ref, wi_ref, wg_ref, lb_ref, nrm_ref,
               s0_ref, y_ref, so_ref, st_scr, q_scr, k_scr, v_scr, cum_scr, gt_scr):
    bt, c, _ = y_ref.shape
    j = pl.program_id(1)
    nh = HG_HEADS
    n, r, ng = _stack_geometry(bt, c, nh)
    sub = HG_SUB

    @pl.when(j == 0)
    def _():
        st_scr[...] = s0_ref[...]

    u = u_ref[...].reshape(bt * c, D_MODEL)
    q = _silu(jnp.dot(u, wq_ref[...], preferred_element_type=f32))
    lb = lb_ref[...]
    fg = lb + (1.0 - lb) * jax.nn.sigmoid(jnp.dot(u, wf_ref[...], preferred_element_type=f32))
    v = jnp.dot(u, wi_ref[...], preferred_element_type=f32)
    gt_scr[...] = jnp.dot(u, wg_ref[...], preferred_element_type=f32).reshape(bt, c, BRANCH_W)
    cum = _dot_sel_left(_blockdiag_ltri(bt * c, c), jnp.log(fg))
    k = 1.0 - fg
    for h in range(nh):
        sl = slice(h * HG_DK, (h + 1) * HG_DK)
        q_scr[:, h] = q[:, sl].reshape(bt, c, HG_DK)
        k_scr[:, h] = k[:, sl].reshape(bt, c, HG_DK)
        v_scr[:, h] = v[:, sl].reshape(bt, c, HG_DV)
        cum_scr[:, h] = cum[:, sl].reshape(bt, c, HG_DK)
    qf = q_scr[...].reshape(n, HG_DK)
    kf = k_scr[...].reshape(n, HG_DK)
    vf = v_scr[...].reshape(n, HG_DV)
    cf = cum_scr[...].reshape(n, HG_DK)

    ii, jj = _iota2(r, r, 0), _iota2(r, r, 1)
    scores = [jnp.zeros((r, r), f32) for _ in range(ng)]

    q3, k3, c3 = (a.reshape(n // sub, sub, HG_DK) for a in (qf, kf, cf))
    sub_i = lax.broadcasted_iota(jnp.int32, q3.shape, 1)
    ones = jnp.ones((HG_DK, r), bf16)
    same_sub = (ii >> _log2(sub)) == (jj >> _log2(sub))
    for t in range(sub):
        pt = jnp.where(sub_i >= t, q3 * k3[:, t:t + 1, :] * jnp.exp(c3 - c3[:, t:t + 1, :]), 0.0)
        row_sum = jnp.dot(pt.reshape(n, HG_DK).astype(bf16), ones, preferred_element_type=f32)
        place = same_sub & ((jj & (sub - 1)) == t)
        scores = [jnp.where(place, row_sum[g * r:(g + 1) * r], scores[g]) for g in range(ng)]

    m = sub
    while m < c:
        q4, k4, c4 = (a.reshape(n // (2 * m), 2, m, HG_DK) for a in (qf, kf, cf))
        bnd = c4[:, 0:1, m - 1:m, :]
        odd = lax.broadcasted_iota(jnp.int32, q4.shape, 1) == 1
        e = jnp.exp(jnp.where(odd, c4 - bnd, bnd - c4))
        q_t = jnp.where(odd, q4 * e, 0.0).reshape(n, HG_DK)
        k_t = jnp.where(odd, 0.0, k4 * e).reshape(n, HG_DK)
        same_pair = (ii >> _log2(2 * m)) == (jj >> _log2(2 * m))
        scores = [scores[g] + jnp.where(same_pair, _dot_nt(q_t[g * r:(g + 1) * r], k_t[g * r:(g + 1) * r]), 0.0)
                  for g in range(ng)]
        m *= 2

    cb3 = cf.reshape(n // c, c, HG_DK)
    clast3 = cb3[:, c - 1:c, :]
    q_in = qf * jnp.exp(cf)
    k_out = (kf.reshape(n // c, c, HG_DK) * jnp.exp(clast3 - cb3)).reshape(n, HG_DK)
    nblk = n // c
    e_rows = jnp.exp(cb3[:, c - 1, :])
    if nblk < HG_DK:
        e_rows = jnp.concatenate([e_rows, jnp.zeros((HG_DK - nblk, HG_DK), f32)], axis=0)
    e_cols = e_rows.T
    blocks = [(b, h) for b in range(bt) for h in range(nh)]
    rows = lambda a, i: a[i * c:(i + 1) * c]
    st_old = [st_scr[b, h] for b, h in blocks]
    o_state = [_dot(rows(q_in, i), st_old[i]) for i in range(len(blocks))]
    for i, (b, h) in enumerate(blocks):
        st_scr[b, h] = st_old[i] * e_cols[:, i:i + 1] + _dot_tn(rows(k_out, i), rows(vf, i))
    o = [_dot(scores[g], vf[g * r:(g + 1) * r]) for g in range(ng)]
    o = o[0] if ng == 1 else jnp.concatenate(o, axis=0)
    for b in range(bt):
        y = jnp.concatenate([_rms(rows(o, b * nh + h) + o_state[b * nh + h]) * nrm_ref[...] for h in range(nh)],
                            axis=-1)
        y_ref[b] = (y * _silu(gt_scr[b])).astype(bf16)

    @pl.when(j == pl.num_programs(1) - 1)
    def _():
        so_ref[...] = st_scr[...]


def _hgrn(u, w, s0, c, bt, slot):
    consts = [w["wq"], w["wf"], w["wi"], w["wg"], w["lb"], w["nrm"]]
    scratch = ([pltpu.VMEM((bt, HG_HEADS, HG_DK, HG_DV), f32)] + [pltpu.VMEM((bt, HG_HEADS, c, HG_DK), f32)] * 4
               + [pltpu.VMEM((bt, c, BRANCH_W), f32)])
    return _mixer_call(_hgrn_body, "hgrn", u, consts, [s0], c, bt, [], scratch, slot)


def _unit_lower_inverses(a_list, r, c):
    ii, jj = _iota2(r, r, 0), _iota2(r, r, 1)
    eye = jnp.where(ii == jj, 1.0, 0.0)
    t_list = [eye for _ in a_list]
    m = 1
    while m < c:
        lm = _log2(m)
        join = ((ii >> (lm + 1)) == (jj >> (lm + 1))) & (((ii >> lm) & 1) == 1) & (((jj >> lm) & 1) == 0)
        f_list = [jnp.where(join, a, 0.0) for a in a_list]
        if m == 1:
            t_list = [eye - f for f in f_list]
        else:
            ft_list = [_dot(f, t) for f, t in zip(f_list, t_list)]
            t_list = [t - _dot(t, ft) for t, ft in zip(t_list, ft_list)]
        m *= 2
    return t_list


def _gdn_body(u_ref, wqkv_ref, wz_ref, wab_ref, cw_ref, dtb_ref, alog_ref, nrm_ref,
              s0_ref, c0_ref, y_ref, so_ref, co_ref, s_scr, xp_scr, z_scr, q_scr, k_scr, v_scr, cum_scr, beta_scr):
    bt, c, _ = y_ref.shape
    j = pl.program_id(1)
    nh = GDN_HEADS
    hk = nh * GDN_DK
    n, r, ng = _stack_geometry(bt, c, nh)

    @pl.when(j == 0)
    def _():
        s_scr[...] = s0_ref[...]
        xp_scr[:, 0:CONV_PAD, :] = c0_ref[...]

    u = u_ref[...].reshape(bt * c, D_MODEL)
    xp_scr[:, CONV_PAD:CONV_PAD + c, :] = jnp.dot(u, wqkv_ref[...], preferred_element_type=f32).reshape(
        bt, c, GDN_CONV_DIM)
    z_scr[...] = jnp.dot(u, wz_ref[...], preferred_element_type=f32).reshape(bt, c, BRANCH_W)
    ab = jnp.dot(u, wab_ref[...], preferred_element_type=f32)
    log_g = -jnp.exp(alog_ref[...]) * jax.nn.softplus(ab + dtb_ref[...])
    cum = _dot_sel_left(_blockdiag_ltri(bt * c, c), log_g)
    beta = jax.nn.sigmoid(ab)
    qkv = _silu(_causal_conv(xp_scr, cw_ref, c))
    tail = xp_scr[:, c:c + CONV_PAD, :]
    xp_scr[:, 0:CONV_PAD, :] = tail
    co_ref[...] = tail
    lane_bcast = lambda col: jnp.broadcast_to(col, (bt * c, LANES)).reshape(bt, c, LANES)
    for h in range(nh):
        q = qkv[:, :, h * GDN_DK:(h + 1) * GDN_DK]
        k = qkv[:, :, hk + h * GDN_DK:hk + (h + 1) * GDN_DK]
        q_scr[:, h] = q * lax.rsqrt(jnp.sum(q * q, axis=-1, keepdims=True) + EPS) * (GDN_DK ** -0.5)
        k_scr[:, h] = k * lax.rsqrt(jnp.sum(k * k, axis=-1, keepdims=True) + EPS)
        v_scr[:, h] = qkv[:, :, 2 * hk + h * GDN_DV:2 * hk + (h + 1) * GDN_DV]
        cum_scr[:, h] = lane_bcast(cum[:, h:h + 1])
        beta_scr[:, h] = lane_bcast(beta[:, nh + h:nh + h + 1])

    grp = lambda ref, g: _group_rows(ref, g, r)
    groups = range(ng)
    qs = [grp(q_scr, g) for g in groups]
    ks = [grp(k_scr, g) for g in groups]
    vs = [grp(v_scr, g) for g in groups]
    cb = [grp(cum_scr, g) for g in groups]
    bb = [grp(beta_scr, g) for g in groups]
    ii, jj = _iota2(r, r, 0), _iota2(r, r, 1)
    same = (ii >> _log2(c)) == (jj >> _log2(c))
    incl = same & (ii >= jj)
    strict = same & (ii > jj)
    dec = [jnp.exp(jnp.where(incl, _widen(cb[g], r) - cb[g].T[0:1, :], -jnp.inf)) for g in groups]
    a_mat = [jnp.where(strict, _widen(bb[g], r) * dec[g] * _dot_nt(ks[g], ks[g]), 0.0) for g in groups]
    t_inv = _unit_lower_inverses(a_mat, r, c)
    e_in = [jnp.exp(cb[g]) for g in groups]
    sol = [_dot(t_inv[g], jnp.concatenate([vs[g] * bb[g], ks[g] * (e_in[g] * bb[g])], axis=-1)) for g in groups]
    scd = [_dot_nt(qs[g], ks[g]) * dec[g] for g in groups]
    q_in = [qs[g] * e_in[g] for g in groups]

    rows = lambda a, i: a[i * c:(i + 1) * c]
    nb = r // c
    o_all = []
    for g in groups:
        blocks = [divmod(g * nb + i, nh) for i in range(nb)]
        s_old = [s_scr[b, h] for b, h in blocks]
        proj = [_dot(jnp.concatenate([rows(sol[g], i)[:, GDN_DV:], rows(q_in[g], i)], axis=0), s_old[i])
                for i in range(nb)]
        w_new = [rows(sol[g], i)[:, :GDN_DV] - proj[i][:c] for i in range(nb)]
        for i, (b, h) in enumerate(blocks):
            cum_i = rows(cb[g], i)
            clast = cum_i[c - 1:c]
            s_scr[b, h] = s_old[i] * jnp.exp(clast) + _dot_tn(rows(ks[g], i) * jnp.exp(clast - cum_i), w_new[i])
        o_all.append(_dot(scd[g], jnp.concatenate(w_new, axis=0)) + jnp.concatenate([p[c:] for p in proj], axis=0))
    o = o_all[0] if ng == 1 else jnp.concatenate(o_all, axis=0)
    for b in range(bt):
        y = jnp.concatenate([_rms(rows(o, b * nh + h)) * nrm_ref[...] for h in range(nh)], axis=-1)
        y_ref[b] = (y * _silu(z_scr[b])).astype(bf16)

    @pl.when(j == pl.num_programs(1) - 1)
    def _():
        so_ref[...] = s_scr[...]


def _gdn(u, w, s0, conv0, c, bt, slot):
    b, l, _ = u.shape
    consts = [w["wqkv"], w["wz"], w["wab"], w["cw"], w["dtb"], w["alog"], w["nrm"]]
    scratch = [
        pltpu.VMEM((bt, GDN_HEADS, GDN_DK, GDN_DV), f32),
        pltpu.VMEM((bt, CONV_PAD + c, GDN_CONV_DIM), f32),
        pltpu.VMEM((bt, c, BRANCH_W), f32),
    ] + [pltpu.VMEM((bt, GDN_HEADS, c, LANES), f32)] * 5
    return _mixer_call(_gdn_body, "gdn", u, consts, [s0, conv0], c, bt,
                       [(b, CONV_PAD, GDN_CONV_DIM)], scratch, slot)


def _pad_lanes(v, offset=0):
    row = jnp.zeros((1, LANES), f32)
    return row.at[0, offset:offset + v.shape[0]].set(v.astype(f32))


def _layer_weights(l, norm_g, ffn_w_in, ffn_w_out, w_in, w_branch, w_out, mb_conv_w, mb_conv_b, mb_a_log,
                   mb_dt_bias, mb_d, mb_norm, lb_all, hg_norm, gdn_conv_w, gdn_a_log, gdn_dt_bias, gdn_norm):
    wi = w_in[l]
    col = lambda n: wi[:, IN_OFFSETS[n]:IN_OFFSETS[n + 1]]
    colb = lambda n: col(n).astype(bf16)
    pad_cols = lambda a: jnp.pad(a, ((0, 0), (0, LANES - a.shape[1]))).astype(bf16)
    hw = RET_DK // 2
    perm = np.arange(RET_HEADS * RET_DK).reshape(RET_HEADS, 2, hw).transpose(1, 0, 2).reshape(-1)
    row = lambda v: v.astype(f32).reshape(1, -1)
    return {
        "g_ffn1": norm_g[l, 0:3], "g_post": norm_g[l, 3:4], "g_ffn2": norm_g[l, 4:6],
        "layer": l, "ffn_in": ffn_w_in, "ffn_out": ffn_w_out,
        "mamba": {"wz": colb(0), "wx": colb(1), "wdt": pad_cols(col(2)), "cw": mb_conv_w[l],
                  "cb": row(mb_conv_b[l]),
                  "dtb": _pad_lanes(mb_dt_bias[l]), "alog": _pad_lanes(mb_a_log[l]),
                  "d": row(jnp.repeat(mb_d[l], MB_HEADDIM)), "nrm": row(mb_norm[l])},
        "hgrn": {"wq": colb(3), "wf": colb(4), "wi": colb(5), "wg": colb(6), "lb": row(lb_all[l]),
                 "nrm": row(hg_norm[l])},
        "ret": {"wq": col(7)[:, perm].astype(bf16), "wk": col(8)[:, perm].astype(bf16),
                "wv": colb(9), "wg": colb(10)},
        "gdn": {"wqkv": colb(11), "wz": colb(12),
                "wab": pad_cols(jnp.concatenate([col(13), col(14)], axis=1)),
                "cw": gdn_conv_w[l], "dtb": _pad_lanes(gdn_dt_bias[l]), "alog": _pad_lanes(gdn_a_log[l]),
                "nrm": row(gdn_norm[l])},
        "w_gate": colb(15), "w_branch": w_branch[l].astype(bf16), "w_out": w_out[l].astype(bf16),
    }


def _rope_tables(pos):
    half = RET_DK // 2
    inv_freq = 1.0 / (ROPE_BASE ** jnp.linspace(0.0, 1.0, half, dtype=f32))
    ang = pos.astype(f32)[:, None] * inv_freq[None, :]
    reps = LANES // half
    return jnp.tile(jnp.cos(ang), (1, reps)), jnp.tile(jnp.sin(ang), (1, reps))


def _pad_conv(buf):
    return jnp.pad(buf, ((0, 0), (CONV_PAD - (CONV_W - 1), 0), (0, 0)))


def _layer(h, w, states, rope, lg, chunks, bt, depth, stacked):
    b, l, d = h.shape
    s_ssm, s_ssm_conv, s_hg, s_ret, s_gdn, s_gdn_conv = states
    slot = lambda n: (w["layer"], depth, None if stacked is None else stacked[n])
    h, u2 = _ffn(h.reshape(b * l, d), w["g_ffn1"], w["ffn_in"], w["ffn_out"], w["layer"], 0, emit_normed=True)
    u = u2.reshape(b, l, d)
    y_mb, n_ssm, n_ssm_conv = _mamba(u, w["mamba"], s_ssm, _pad_conv(s_ssm_conv), chunks["mamba"], bt, slot(0))
    y_hg, n_hg = _hgrn(u, w["hgrn"], s_hg, chunks["hgrn"], bt, slot(1))
    y_rt, n_ret = _ret(u, w["ret"], rope[0], rope[1], lg, s_ret, chunks["ret"], bt, slot(2))
    y_gd, n_gdn, n_gdn_conv = _gdn(u, w["gdn"], s_gdn, _pad_conv(s_gdn_conv), chunks["gdn"], bt, slot(3))
    ys = [y.reshape(b * l, BRANCH_W) for y in (y_mb, y_hg, y_rt, y_gd)]
    hf = _merge(h, u2, w["g_post"], ys, w["w_gate"], w["w_branch"], w["w_out"])
    hf = _ffn(hf, w["g_ffn2"], w["ffn_in"], w["ffn_out"], w["layer"], 1)
    convs = (n_ssm_conv[:, CONV_PAD - (CONV_W - 1):], n_gdn_conv[:, CONV_PAD - (CONV_W - 1):])
    return hf.reshape(b, l, d), (n_ssm, n_hg, n_ret, n_gdn), convs


def _chunks(l):
    pick = lambda pref: pref if l % pref == 0 else math.gcd(l, pref)
    return {"mamba": pick(128), "ret": pick(128), "hgrn": pick(64), "gdn": pick(64)}


def kernel(x_prompt, x_sample, state_ssm, state_ssm_conv, state_hgrn, state_ret, state_gdn, state_gdn_conv,
           norm_g, ffn_w_in, ffn_w_out, w_in, w_branch, w_out, mb_conv_w, mb_conv_b, mb_a_log, mb_dt_bias, mb_d,
           mb_norm, hg_lb_logits, hg_norm, gdn_conv_w, gdn_a_log, gdn_dt_bias, gdn_norm):
    depth = norm_g.shape[0]
    lb_all = jnp.cumsum(jax.nn.softmax(hg_lb_logits.astype(f32), axis=0), axis=0)
    lb_all = lb_all - lb_all[0:1]
    lg = _pad_lanes(jnp.log(1.0 - jnp.exp2(-5.0 - jnp.arange(RET_HEADS, dtype=f32))))

    bp, lp, _ = x_prompt.shape
    bs, ls, _ = x_sample.shape
    rope_p = _rope_tables(jnp.arange(lp))
    rope_s = _rope_tables(PAST_LEN + jnp.arange(ls))
    zero_states = (
        (jnp.zeros((1, bp, MB_HEADS, MB_STATE, MB_HEADDIM), f32), 0),
        jnp.zeros((bp, CONV_W - 1, MB_CONV_DIM), f32),
        (jnp.zeros((1, bp, HG_HEADS, HG_DK, HG_DV), f32), 0),
        (jnp.zeros((1, bp, RET_HEADS, RET_DK, RET_DV), f32), 0),
        (jnp.zeros((1, bp, GDN_HEADS, GDN_DK, GDN_DV), f32), 0),
        jnp.zeros((bp, CONV_W - 1, GDN_CONV_DIM), f32),
    )
    bt_p = math.gcd(bp, 8)
    bt_s = math.gcd(bs, 16)

    ffn_w_in = ffn_w_in.astype(bf16)
    ffn_w_out = ffn_w_out.astype(bf16)
    hp, hs = x_prompt, x_sample
    st_p = st_s = None
    cv_p, cv_s = [], []
    for l in range(depth):
        w = _layer_weights(l, norm_g, ffn_w_in, ffn_w_out, w_in, w_branch, w_out, mb_conv_w, mb_conv_b,
                           mb_a_log, mb_dt_bias, mb_d, mb_norm, lb_all, hg_norm, gdn_conv_w, gdn_a_log,
                           gdn_dt_bias, gdn_norm)
        hp, st_p, cp = _layer(hp, w, zero_states, rope_p, lg, _chunks(lp), bt_p, depth, st_p)
        past = ((state_ssm, l), state_ssm_conv[l], (state_hgrn, l), (state_ret, l), (state_gdn, l), state_gdn_conv[l])
        hs, st_s, cs = _layer(hs, w, past, rope_s, lg, _chunks(ls), bt_s, depth, st_s)
        cv_p.append(cp)
        cv_s.append(cs)

    stack = lambda cvs, i: jnp.stack([cv[i] for cv in cvs], axis=0)
    return (hp, hs,
            st_p[0], stack(cv_p, 0), st_p[1], st_p[2], st_p[3], stack(cv_p, 1),
            st_s[0], stack(cv_s, 0), st_s[1], st_s[2], st_s[3], stack(cv_s, 1))
```

```python
import functools
import math

import numpy as np
import jax
import jax.numpy as jnp
from jax import lax
from jax.experimental import pallas as pl
from jax.experimental.pallas import tpu as pltpu

f32 = jnp.float32
bf16 = jnp.bfloat16

D_MODEL = 1024
D_FF = 2816
BRANCH_W = 512
N_BRANCH = 4
CONV_W = 4
EPS = 1e-6
ROPE_BASE = 10000.0
PAST_LEN = 16384

MB_HEADS, MB_HEADDIM, MB_STATE, MB_GROUPS = 8, 64, 64, 2
MB_CONV_DIM = BRANCH_W + 2 * MB_GROUPS * MB_STATE
HG_HEADS, HG_DK, HG_DV = 4, 128, 128
RET_HEADS, RET_DK, RET_DV = 4, 64, 128
GDN_HEADS, GDN_DK, GDN_DV = 4, 128, 128
GDN_CONV_DIM = 2 * GDN_HEADS * GDN_DK + GDN_HEADS * GDN_DV

IN_SPLITS = (
    BRANCH_W, MB_CONV_DIM, MB_HEADS,
    HG_HEADS * HG_DK, HG_HEADS * HG_DK, BRANCH_W, BRANCH_W,
    RET_HEADS * RET_DK, RET_HEADS * RET_DK, BRANCH_W, BRANCH_W,
    GDN_CONV_DIM, BRANCH_W, GDN_HEADS, GDN_HEADS,
    N_BRANCH * D_MODEL,
)
IN_OFFSETS = tuple(int(v) for v in np.cumsum((0,) + IN_SPLITS))

LANES = 128
SUBLANES = 8
MXU_DIM = 256
STACK_ROWS = 128
CONV_PAD = SUBLANES
VMEM_LIMIT = 56 * 1024 * 1024
FFN_TM = 512
MERGE_TM = 512
HG_SUB = SUBLANES
FRONT_ROWS = 128


def _dot(a, b):
    return jnp.dot(a.astype(bf16), b.astype(bf16), preferred_element_type=f32)


def _dot_nt(a, b):
    return lax.dot_general(a.astype(bf16), b.astype(bf16), (((1,), (1,)), ((), ())), preferred_element_type=f32)


def _dot_tn(a, b):
    return lax.dot_general(a.astype(bf16), b.astype(bf16), (((0,), (0,)), ((), ())), preferred_element_type=f32)


def _split3(x):
    hi = x.astype(bf16)
    r = x - hi.astype(f32)
    mid = r.astype(bf16)
    lo = (r - mid.astype(f32)).astype(bf16)
    return hi, mid, lo


def _dot_sel_left(m01, x):
    hi, mid, lo = _split3(x)
    return (jnp.dot(m01, hi, preferred_element_type=f32) + jnp.dot(m01, mid, preferred_element_type=f32)
            + jnp.dot(m01, lo, preferred_element_type=f32))


def _rms(x):
    return x * lax.rsqrt(jnp.mean(x * x, axis=-1, keepdims=True) + EPS)


def _silu(x):
    return x * jax.nn.sigmoid(x)


def _iota2(n, m, axis):
    return lax.broadcasted_iota(jnp.int32, (n, m), axis)


def _log2(n):
    assert n > 0 and n & (n - 1) == 0, n
    return n.bit_length() - 1


def _ltri(c):
    return jnp.where(_iota2(c, c, 0) >= _iota2(c, c, 1), 1.0, 0.0).astype(bf16)


def _blockdiag_ltri(n, c):
    ii, jj = _iota2(n, n, 0), _iota2(n, n, 1)
    lc = _log2(c)
    return jnp.where(((ii >> lc) == (jj >> lc)) & (ii >= jj), 1.0, 0.0).astype(bf16)


def _row_bcast(col, c):
    dg = jnp.where(_iota2(c, c, 0) == _iota2(c, c, 1), jnp.broadcast_to(col, (c, c)), 0.0)
    return _dot_sel_left(jnp.ones((c, c), bf16), dg)


def _decay_matrix(col, c):
    diff = col - _row_bcast(col, c)
    return jnp.exp(jnp.where(_iota2(c, c, 0) >= _iota2(c, c, 1), diff, -jnp.inf))


def _causal_conv(xp_ref, w_ref, c):
    y = xp_ref[:, pl.ds(CONV_PAD - 3, c), :] * w_ref[0:1, :]
    for j in range(1, CONV_W):
        y = y + xp_ref[:, pl.ds(CONV_PAD - 3 + j, c), :] * w_ref[j:j + 1, :]
    return y


def _stack_geometry(bt, c, heads):
    n = bt * heads * c
    r = min(STACK_ROWS, n)
    assert n % r == 0 and r % c == 0 and r % LANES == 0
    return n, r, n // r


def _row_parts(bt, c):
    parts = max(1, (bt * c) // FRONT_ROWS)
    assert bt % parts == 0
    return parts


def _pipeline_parts(parts, *stages):
    ns = len(stages)
    for t in range(parts + ns - 1):
        for s, stage in enumerate(stages):
            if 0 <= t - s < parts:
                stage(t - s)


def _group_rows(ref, g, r):
    _, heads, c, w = ref.shape
    bpg = r // c
    if bpg <= heads:
        assert heads % bpg == 0
        b, h0 = divmod(g * bpg, heads)
        return ref[b, h0:h0 + bpg].reshape(r, w)
    assert bpg % heads == 0
    nseq = bpg // heads
    return ref[g * nseq:(g + 1) * nseq].reshape(r, w)


def _widen(x, r):
    return x if r == LANES else jnp.concatenate([x] * (r // LANES), axis=-1)


def _ffn_body(x_ref, g_ref, wg_ref, wu_ref, wo_ref, o_ref, *un_ref):
    x = x_ref[...]
    xn = (_rms(x) * g_ref[0:1, :]).astype(bf16)
    gate = jnp.dot(xn, wg_ref[...], preferred_element_type=f32)
    up = jnp.dot(xn, wu_ref[...], preferred_element_type=f32)
    act = (_silu(gate) * up).astype(bf16)
    y = jnp.dot(act, wo_ref[...], preferred_element_type=f32)
    out = x + 0.5 * (_rms(y) * g_ref[1:2, :])
    o_ref[...] = out
    if un_ref:
        un_ref[0][...] = (_rms(out) * g_ref[2:3, :]).astype(bf16)


def _ffn(x, g, w_in, w_out, l, k, emit_normed=False):
    t = x.shape[0]
    tm = min(FFN_TM, t)
    resident = pl.Buffered(1)
    row = pl.BlockSpec((tm, D_MODEL), lambda i: (i, 0))
    out_specs, out_shape = row, jax.ShapeDtypeStruct((t, D_MODEL), f32)
    if emit_normed:
        out_specs, out_shape = [row, row], [out_shape, jax.ShapeDtypeStruct((t, D_MODEL), bf16)]
    return pl.pallas_call(
        _ffn_body,
        grid=(t // tm,),
        in_specs=[
            row,
            pl.BlockSpec(g.shape, lambda i: (0, 0)),
            pl.BlockSpec((None, None, D_MODEL, D_FF), lambda i: (l, k, 0, 0), pipeline_mode=resident),
            pl.BlockSpec((None, None, D_MODEL, D_FF), lambda i: (l, k, 0, 1), pipeline_mode=resident),
            pl.BlockSpec((None, None, D_FF, D_MODEL), lambda i: (l, k, 0, 0), pipeline_mode=resident),
        ],
        out_specs=out_specs,
        out_shape=out_shape,
        compiler_params=pltpu.CompilerParams(dimension_semantics=("arbitrary",), vmem_limit_bytes=VMEM_LIMIT),
        name="ffn",
    )(x, g, w_in, w_in, w_out)


def _merge_body(h_ref, u_ref, g_ref, y0_ref, y1_ref, y2_ref, y3_ref, wg_ref, wb_ref, wo_ref, o_ref):
    h = h_ref[...]
    u = u_ref[...]
    merged = jnp.zeros(h.shape, f32)
    for n, y_ref in enumerate((y0_ref, y1_ref, y2_ref, y3_ref)):
        gate = jax.nn.sigmoid(jnp.dot(u, wg_ref[:, n * D_MODEL:(n + 1) * D_MODEL], preferred_element_type=f32))
        merged = merged + gate * jnp.dot(y_ref[...], wb_ref[n], preferred_element_type=f32)
    mixed = jnp.dot(merged.astype(bf16), wo_ref[...], preferred_element_type=f32)
    o_ref[...] = h + _rms(mixed) * g_ref[...]


def _merge(h, u, g_post, ys, w_gate, w_branch, w_out):
    t = h.shape[0]
    tm = min(MERGE_TM, t)
    row = lambda i: (i, 0)
    resident = pl.Buffered(1)
    return pl.pallas_call(
        _merge_body,
        grid=(t // tm,),
        in_specs=[pl.BlockSpec((tm, D_MODEL), row), pl.BlockSpec((tm, D_MODEL), row),
                  pl.BlockSpec((1, D_MODEL), lambda i: (0, 0))]
                 + [pl.BlockSpec((tm, BRANCH_W), row)] * N_BRANCH
                 + [pl.BlockSpec((D_MODEL, N_BRANCH * D_MODEL), lambda i: (0, 0), pipeline_mode=resident),
                    pl.BlockSpec((N_BRANCH, BRANCH_W, D_MODEL), lambda i: (0, 0, 0), pipeline_mode=resident),
                    pl.BlockSpec((D_MODEL, D_MODEL), lambda i: (0, 0), pipeline_mode=resident)],
        out_specs=pl.BlockSpec((tm, D_MODEL), row),
        out_shape=jax.ShapeDtypeStruct((t, D_MODEL), f32),
        compiler_params=pltpu.CompilerParams(dimension_semantics=("arbitrary",), vmem_limit_bytes=VMEM_LIMIT),
        name="merge",
    )(h, u, g_post, *ys, w_gate, w_branch, w_out)


def _const_spec(a):
    nd = a.ndim
    return pl.BlockSpec(a.shape, lambda i, j, _nd=nd: (0,) * _nd)


def _batch_spec(bt, a):
    nd = a.ndim
    return pl.BlockSpec((bt,) + a.shape[1:], lambda i, j, _nd=nd: (i,) + (0,) * (_nd - 1))


def _mixer_call(body, name, u, consts, batch_ins, c, bt, batch_outs, scratch, slot, chunk_ins=()):
    b, l, d = u.shape
    layer, depth, prev = slot
    seq = lambda i, j: (i, j, 0)
    if c == l:
        u, u_spec = u.reshape(b * l, d), pl.BlockSpec((bt * c, d), lambda i, j: (i, 0))
    else:
        u_spec = pl.BlockSpec((bt, c, d), seq)
    st_in, st_entry = batch_ins[0]
    st = tuple(st_in.shape[1:])
    st_block = (None, bt) + st[1:]
    out_shape = ([jax.ShapeDtypeStruct((b, l, BRANCH_W), bf16), jax.ShapeDtypeStruct((depth,) + st, f32)]
                 + [jax.ShapeDtypeStruct(s, f32) for s in batch_outs])
    out_specs = ([pl.BlockSpec((bt, c, BRANCH_W), seq),
                  pl.BlockSpec(st_block, lambda i, j: (layer, i) + (0,) * (len(st) - 1))]
                 + [pl.BlockSpec((bt,) + s[1:], lambda i, j, _nd=len(s): (i,) + (0,) * (_nd - 1))
                    for s in batch_outs])
    operands = [u, *consts, *chunk_ins, st_in, *batch_ins[1:]]
    in_specs = ([u_spec] + [_const_spec(a) for a in consts]
                + [pl.BlockSpec((c, a.shape[1]), lambda i, j: (j, 0)) for a in chunk_ins]
                + [pl.BlockSpec(st_block, lambda i, j: (st_entry, i) + (0,) * (len(st) - 1))]
                + [_batch_spec(bt, a) for a in batch_ins[1:]])
    n_in = len(operands)
    aliases = {}
    fn = body
    if prev is not None:
        operands.append(prev)
        in_specs.append(pl.BlockSpec(memory_space=pl.ANY))
        aliases = {n_in: 1}
        fn = lambda *refs: body(*refs[:n_in], *refs[n_in + 1:])
    elif depth > 1:
        out_specs[1] = pl.BlockSpec((depth, bt) + st[1:], lambda i, j: (0, i) + (0,) * (len(st) - 1))

        def fn(*refs):
            so_ref = refs[n_in + 1]

            @pl.when(pl.program_id(1) == 0)
            def _():
                for e in range(depth):
                    if e != layer:
                        so_ref[e] = jnp.zeros(so_ref.shape[1:], f32)

            body(*refs[:n_in + 1], so_ref.at[layer], *refs[n_in + 2:])

    return pl.pallas_call(
        fn,
        grid=(b // bt, l // c),
        in_specs=in_specs,
        out_specs=out_specs,
        out_shape=out_shape,
        scratch_shapes=scratch,
        input_output_aliases=aliases,
        compiler_params=pltpu.CompilerParams(dimension_semantics=("arbitrary", "arbitrary"),
                                             vmem_limit_bytes=VMEM_LIMIT),
        name=name,
    )(*operands)


def _mamba_body(u_ref, wz_ref, wx_ref, wdt_ref, cw_ref, cb_ref, dtb_ref, alog_ref, d_ref, nrm_ref,
                s0_ref, c0_ref, y_ref, so_ref, co_ref, s_scr, xp_scr, z_scr, xs_scr, v_scr, cum_scr):
    bt, c, _ = y_ref.shape
    j = pl.program_id(1)
    nh = MB_HEADS
    npair = nh // 2
    hpg = nh // MB_GROUPS
    n = bt * nh * c
    r = min(MXU_DIM, n)
    assert n % r == 0 and r % c == 0
    ng = n // r

    @pl.when(j == 0)
    def _():
        s_scr[...] = jnp.zeros_like(s_scr)
        for p in range(npair):
            g = (2 * p) // hpg
            s_scr[:, p, g * MB_STATE:(g + 1) * MB_STATE, :] = jnp.concatenate(
                [s0_ref[:, 2 * p], s0_ref[:, 2 * p + 1]], axis=-1)
        xp_scr[:, 0:CONV_PAD, :] = c0_ref[...]

    parts = _row_parts(bt, c)
    bp = bt // parts
    u = u_ref[...].reshape(bt * c, D_MODEL)
    ltri = _blockdiag_ltri(bp * c, c)
    lane3 = lax.broadcasted_iota(jnp.int32, (bp, c, LANES), 2)
    lane_bcast = lambda col: jnp.broadcast_to(col, (bp * c, LANES)).reshape(bp, c, LANES)
    dt_parts, bm_parts, cm_parts = {}, {}, {}

    def project(p):
        sq = slice(p * bp, (p + 1) * bp)
        up = u[p * bp * c:(p + 1) * bp * c]
        z_scr[sq] = jnp.dot(up, wz_ref[...], preferred_element_type=f32).reshape(bp, c, BRANCH_W)
        xp_scr[sq, CONV_PAD:CONV_PAD + c, :] = jnp.dot(up, wx_ref[...], preferred_element_type=f32).reshape(
            bp, c, MB_CONV_DIM)
        dt_parts[p] = jnp.dot(up, wdt_ref[...], preferred_element_type=f32)

    def front(p):
        sq = slice(p * bp, (p + 1) * bp)
        dt = jax.nn.softplus(dt_parts.pop(p) + dtb_ref[...])
        cum = _dot_sel_left(ltri, -jnp.exp(alog_ref[...]) * dt)
        xa = _silu(_causal_conv(xp_scr.at[sq], cw_ref, c) + cb_ref[...])
        tail = xp_scr[sq, c:c + CONV_PAD, :]
        xp_scr[sq, 0:CONV_PAD, :] = tail
        co_ref[sq] = tail
        xs = xa[:, :, :BRANCH_W]
        bm_parts[p] = xa[:, :, BRANCH_W:BRANCH_W + LANES]
        cm = xa[:, :, BRANCH_W + LANES:BRANCH_W + 2 * LANES]
        cm_parts[p] = [jnp.where((lane3 >> _log2(MB_STATE)) == g, cm, 0.0) for g in range(MB_GROUPS)]
        xs_scr[sq] = xs
        for h in range(nh):
            own_half = (lane3 >> _log2(MB_HEADDIM)) == (h % 2)
            v_scr[sq, h] = jnp.where(
                own_half, xs[:, :, (h // 2) * LANES:(h // 2 + 1) * LANES] * lane_bcast(dt[:, h:h + 1]), 0.0)
            cum_scr[sq, h] = lane_bcast(cum[:, h:h + 1])

    _pipeline_parts(parts, project, front)
    vf = v_scr[...].reshape(n, LANES)
    cbf = cum_scr[...].reshape(n, LANES)
    blocks_of = lambda first, count: [divmod(first + i, nh) for i in range(count)]
    c_rows = lambda blocks: jnp.concatenate([cm_parts[b // bp][h // hpg][b % bp] for b, h in blocks], axis=0)
    b_rows = lambda blocks: jnp.concatenate([bm_parts[b // bp][b % bp] for b, _ in blocks], axis=0)

    ii, jj = _iota2(r, r, 0), _iota2(r, r, 1)
    incl = ((ii >> _log2(c)) == (jj >> _log2(c))) & (ii >= jj)
    grp = lambda a, g: a[g * r:(g + 1) * r]
    gblocks = [blocks_of(g * (r // c), r // c) for g in range(ng)]
    dec = lambda g: jnp.exp(jnp.where(incl, _widen(grp(cbf, g), r) - grp(cbf, g).T[0:1, :], -jnp.inf))
    o_loc = [_dot(_dot_nt(c_rows(gblocks[g]), b_rows(gblocks[g])) * dec(g), grp(vf, g)) for g in range(ng)]
    o_loc = o_loc[0] if ng == 1 else jnp.concatenate(o_loc, axis=0)

    cb3 = cbf.reshape(n // c, c, LANES)
    k_out = (vf.reshape(n // c, c, LANES) * jnp.exp(cb3[:, c - 1:c, :] - cb3)).reshape(n, LANES)
    e_in = jnp.exp(cbf)
    lo_half = _iota2(1, LANES, 1) < MB_HEADDIM
    row_lo = _iota2(2 * c, LANES, 0) < c
    own = row_lo == (_iota2(2 * c, LANES, 1) < MB_HEADDIM)
    pairs = [(b, p) for b in range(bt) for p in range(npair)]
    prow = lambda a, i: a[i * 2 * c:(i + 1) * 2 * c]
    s_old = [s_scr[b, p] for b, p in pairs]
    pblocks = [blocks_of(i * 2, 2) for i in range(len(pairs))]
    o_st = [jnp.where(own, _dot(c_rows(pblocks[i]), s_old[i]) * prow(e_in, i), 0.0) for i in range(len(pairs))]
    for i, (b, p) in enumerate(pairs):
        cum_i = prow(cbf, i)
        e_all = jnp.exp(jnp.where(lo_half, cum_i[c - 1:c], cum_i[2 * c - 1:2 * c]))
        s_scr[b, p] = s_old[i] * e_all + _dot_tn(b_rows(pblocks[i]), prow(k_out, i))
    for b in range(bt):
        ys = []
        for p in range(npair):
            o2 = prow(o_loc, b * npair + p) + o_st[b * npair + p]
            sl = slice(p * LANES, (p + 1) * LANES)
            ys.append(o2[:c] + o2[c:] + d_ref[:, sl] * xs_scr[b, :, sl])
        y = jnp.concatenate(ys, axis=-1) * _silu(z_scr[b])
        y_ref[b] = (_rms(y) * nrm_ref[...]).astype(bf16)

    @pl.when(j == pl.num_programs(1) - 1)
    def _():
        for p in range(npair):
            g = (2 * p) // hpg
            so_ref[:, 2 * p] = s_scr[:, p, g * MB_STATE:(g + 1) * MB_STATE, 0:MB_HEADDIM]
            so_ref[:, 2 * p + 1] = s_scr[:, p, g * MB_STATE:(g + 1) * MB_STATE, MB_HEADDIM:2 * MB_HEADDIM]


def _mamba(u, w, s0, conv0, c, bt, slot):
    b, l, _ = u.shape
    npair = MB_HEADS // 2
    consts = [w["wz"], w["wx"], w["wdt"], w["cw"], w["cb"], w["dtb"], w["alog"], w["d"], w["nrm"]]
    scratch = [
        pltpu.VMEM((bt, npair, LANES, LANES), f32),
        pltpu.VMEM((bt, CONV_PAD + c, MB_CONV_DIM), f32),
        pltpu.VMEM((bt, c, BRANCH_W), f32),
        pltpu.VMEM((bt, c, BRANCH_W), f32),
    ] + [pltpu.VMEM((bt, MB_HEADS, c, LANES), f32)] * 2
    return _mixer_call(_mamba_body, "mamba", u, consts, [s0, conv0], c, bt,
                       [(b, CONV_PAD, MB_CONV_DIM)], scratch, slot)


def _ret_body(u_ref, wq_ref, wk_ref, wv_ref, wg_ref, lg_ref, cos_ref, sin_ref,
              s0_ref, y_ref, so_ref, s_scr, q_scr, k_scr, v_scr, gt_scr):
    bt, c, _ = y_ref.shape
    j = pl.program_id(1)
    kw = RET_HEADS * RET_DK
    hw = RET_DK // 2

    @pl.when(j == 0)
    def _():
        s_scr[...] = jnp.zeros_like(s_scr)
        for h in range(RET_HEADS):
            s_scr[:, h, h * hw:(h + 1) * hw, :] = s0_ref[:, h, 0:hw, :]
            s_scr[:, h, LANES + h * hw:LANES + (h + 1) * hw, :] = s0_ref[:, h, hw:2 * hw, :]

    u = u_ref[...].reshape(bt * c, D_MODEL)
    cos = cos_ref[...]
    sin = sin_ref[...]

    def rot(x):
        x = x.reshape(-1, c, kw)
        x1, x2 = x[:, :, :LANES], x[:, :, LANES:]
        return jnp.concatenate([x1 * cos - x2 * sin, x1 * sin + x2 * cos], axis=-1)

    nh = RET_HEADS
    n = bt * nh * c
    r = min(MXU_DIM, n)
    assert n % r == 0 and r % c == 0
    ng = n // r
    parts = _row_parts(bt, c)
    bp = bt // parts
    klane = lax.broadcasted_iota(jnp.int32, (bp, c, kw), 2)
    raw = {}

    def project(p):
        sq = slice(p * bp, (p + 1) * bp)
        up = u[p * bp * c:(p + 1) * bp * c]
        raw[p] = (jnp.dot(up, wq_ref[...], preferred_element_type=f32),
                  jnp.dot(up, wk_ref[...], preferred_element_type=f32),
                  jnp.dot(up, wv_ref[...], preferred_element_type=f32))
        gt_scr[sq] = jnp.dot(up, wg_ref[...], preferred_element_type=f32).reshape(bp, c, BRANCH_W)

    def front(p):
        sq = slice(p * bp, (p + 1) * bp)
        q_raw, k_raw, v_raw = raw.pop(p)
        q = rot(q_raw)
        k = rot(k_raw) * (RET_DK ** -0.5)
        v = v_raw.reshape(bp, c, BRANCH_W)
        for h in range(nh):
            own = ((klane & (LANES - 1)) >> _log2(hw)) == h
            q_scr[sq, h] = jnp.where(own, q, 0.0)
            k_scr[sq, h] = jnp.where(own, k, 0.0)
            v_scr[sq, h] = v[:, :, h * RET_DV:(h + 1) * RET_DV]

    _pipeline_parts(parts, project, front)
    qf = q_scr[...].reshape(n, kw)
    kf = k_scr[...].reshape(n, kw)
    vf = v_scr[...].reshape(n, RET_DV)

    row = _iota2(n, 1, 0)
    head = (row >> _log2(c)) & (nh - 1)
    pos = (row & (c - 1)).astype(f32)
    lg_row = jnp.zeros((n, 1), f32)
    for h in range(nh):
        lg_row = jnp.where(head == h, lg_ref[:, h:h + 1], lg_row)
    q_in = qf * jnp.exp((pos + 1.0) * lg_row)
    k_out = kf * jnp.exp((float(c) - 1.0 - pos) * lg_row)
    e_all = jnp.exp(float(c) * lg_row)

    ii, jj = _iota2(r, r, 0), _iota2(r, r, 1)
    incl = ((ii >> _log2(c)) == (jj >> _log2(c))) & (ii >= jj)
    dist = (ii - jj).astype(f32)
    grp = lambda a, g: a[g * r:(g + 1) * r]
    dec = {}
    for g in range(ng):
        key = (g * r) % (nh * c)
        if key not in dec:
            dec[key] = jnp.exp(jnp.where(incl, dist * grp(lg_row, g), -jnp.inf))
    o_loc = [_dot(_dot_nt(grp(qf, g), grp(kf, g)) * dec[(g * r) % (nh * c)], grp(vf, g)) for g in range(ng)]
    o_loc = o_loc[0] if ng == 1 else jnp.concatenate(o_loc, axis=0)

    blocks = [(b, h) for b in range(bt) for h in range(nh)]
    rows = lambda a, i: a[i * c:(i + 1) * c]
    s_old = [s_scr[b, h] for b, h in blocks]
    o_st = [_dot(rows(q_in, i), s_old[i]) for i in range(len(blocks))]
    for i, (b, h) in enumerate(blocks):
        s_scr[b, h] = s_old[i] * rows(e_all, i)[0:1] + _dot_tn(rows(k_out, i), rows(vf, i))
    for b in range(bt):
        y = jnp.concatenate([_rms(rows(o_loc, b * nh + h) + o_st[b * nh + h]) for h in range(nh)], axis=-1)
        y_ref[b] = (y * _silu(gt_scr[b])).astype(bf16)

    @pl.when(j == pl.num_programs(1) - 1)
    def _():
        for h in range(RET_HEADS):
            so_ref[:, h, 0:hw, :] = s_scr[:, h, h * hw:(h + 1) * hw, :]
            so_ref[:, h, hw:2 * hw, :] = s_scr[:, h, LANES + h * hw:LANES + (h + 1) * hw, :]


def _ret(u, w, cos, sin, lg, s0, c, bt, slot):
    kw = RET_HEADS * RET_DK
    consts = [w["wq"], w["wk"], w["wv"], w["wg"], lg]
    scratch = [
        pltpu.VMEM((bt, RET_HEADS, kw, RET_DV), f32),
        pltpu.VMEM((bt, RET_HEADS, c, kw), f32),
        pltpu.VMEM((bt, RET_HEADS, c, kw), f32),
        pltpu.VMEM((bt, RET_HEADS, c, RET_DV), f32),
        pltpu.VMEM((bt, c, BRANCH_W), f32),
    ]
    return _mixer_call(_ret_body, "ret", u, consts, [s0], c, bt, [], scratch, slot, chunk_ins=(cos, sin))


def _hgrn_body(u_ref, wq_ref, wf_ref, wi_ref, wg_ref, lb_ref, nrm_ref,
               s0_ref, y_ref, so_ref, st_scr, q_scr, k_scr, v_scr, cum_scr, gt_scr):
    bt, c, _ = y_ref.shape
    j = pl.program_id(1)
    nh = HG_HEADS
    n, r, ng = _stack_geometry(bt, c, nh)
    sub = HG_SUB

    @pl.when(j == 0)
    def _():
        st_scr[...] = s0_ref[...]

    parts = _row_parts(bt, c)
    bp = bt // parts
    u = u_ref[...].reshape(bt * c, D_MODEL)
    ltri = _blockdiag_ltri(bp * c, c)
    lb = lb_ref[...]
    raw = {}

    def project(p):
        sq = slice(p * bp, (p + 1) * bp)
        up = u[p * bp * c:(p + 1) * bp * c]
        raw[p] = (jnp.dot(up, wq_ref[...], preferred_element_type=f32),
                  jnp.dot(up, wf_ref[...], preferred_element_type=f32),
                  jnp.dot(up, wi_ref[...], preferred_element_type=f32))
        gt_scr[sq] = jnp.dot(up, wg_ref[...], preferred_element_type=f32).reshape(bp, c, BRANCH_W)

    def front(p):
        sq = slice(p * bp, (p + 1) * bp)
        q_raw, f_raw, v = raw.pop(p)
        q = _silu(q_raw)
        fg = lb + (1.0 - lb) * jax.nn.sigmoid(f_raw)
        cum = _dot_sel_left(ltri, jnp.log(fg))
        k = 1.0 - fg
        for h in range(nh):
            sl = slice(h * HG_DK, (h + 1) * HG_DK)
            q_scr[sq, h] = q[:, sl].reshape(bp, c, HG_DK)
            k_scr[sq, h] = k[:, sl].reshape(bp, c, HG_DK)
            v_scr[sq, h] = v[:, sl].reshape(bp, c, HG_DV)
            cum_scr[sq, h] = cum[:, sl].reshape(bp, c, HG_DK)

    _pipeline_parts(parts, project, front)
    qf = q_scr[...].reshape(n, HG_DK)
    kf = k_scr[...].reshape(n, HG_DK)
    vf = v_scr[...].reshape(n, HG_DV)
    cf = cum_scr[...].reshape(n, HG_DK)

    ii, jj = _iota2(r, r, 0), _iota2(r, r, 1)
    scores = [jnp.zeros((r, r), f32) for _ in range(ng)]

    q3, k3, c3 = (a.reshape(n // sub, sub, HG_DK) for a in (qf, kf, cf))
    sub_i = lax.broadcasted_iota(jnp.int32, q3.shape, 1)
    ones = jnp.ones((HG_DK, r), bf16)
    same_sub = (ii >> _log2(sub)) == (jj >> _log2(sub))
    for t in range(sub):
        pt = jnp.where(sub_i >= t, q3 * k3[:, t:t + 1, :] * jnp.exp(c3 - c3[:, t:t + 1, :]), 0.0)
        row_sum = jnp.dot(pt.reshape(n, HG_DK).astype(bf16), ones, preferred_element_type=f32)
        place = same_sub & ((jj & (sub - 1)) == t)
        scores = [jnp.where(place, row_sum[g * r:(g + 1) * r], scores[g]) for g in range(ng)]

    m = sub
    while m < c:
        q4, k4, c4 = (a.reshape(n // (2 * m), 2, m, HG_DK) for a in (qf, kf, cf))
        bnd = c4[:, 0:1, m - 1:m, :]
        odd = lax.broadcasted_iota(jnp.int32, q4.shape, 1) == 1
        e = jnp.exp(jnp.where(odd, c4 - bnd, bnd - c4))
        q_t = jnp.where(odd, q4 * e, 0.0).reshape(n, HG_DK)
        k_t = jnp.where(odd, 0.0, k4 * e).reshape(n, HG_DK)
        same_pair = (ii >> _log2(2 * m)) == (jj >> _log2(2 * m))
        scores = [scores[g] + jnp.where(same_pair, _dot_nt(q_t[g * r:(g + 1) * r], k_t[g * r:(g + 1) * r]), 0.0)
                  for g in range(ng)]
        m *= 2

    cb3 = cf.reshape(n // c, c, HG_DK)
    clast3 = cb3[:, c - 1:c, :]
    q_in = qf * jnp.exp(cf)
    k_out = (kf.reshape(n // c, c, HG_DK) * jnp.exp(clast3 - cb3)).reshape(n, HG_DK)
    nblk = n // c
    e_rows = jnp.exp(cb3[:, c - 1, :])
    if nblk < HG_DK:
        e_rows = jnp.concatenate([e_rows, jnp.zeros((HG_DK - nblk, HG_DK), f32)], axis=0)
    e_cols = e_rows.T
    blocks = [(b, h) for b in range(bt) for h in range(nh)]
    rows = lambda a, i: a[i * c:(i + 1) * c]
    st_old = [st_scr[b, h] for b, h in blocks]
    o_state = [_dot(rows(q_in, i), st_old[i]) for i in range(len(blocks))]
    for i, (b, h) in enumerate(blocks):
        st_scr[b, h] = st_old[i] * e_cols[:, i:i + 1] + _dot_tn(rows(k_out, i), rows(vf, i))
    o = [_dot(scores[g], vf[g * r:(g + 1) * r]) for g in range(ng)]
    o = o[0] if ng == 1 else jnp.concatenate(o, axis=0)
    for b in range(bt):
        y = jnp.concatenate([_rms(rows(o, b * nh + h) + o_state[b * nh + h]) * nrm_ref[...] for h in range(nh)],
                            axis=-1)
        y_ref[b] = (y * _silu(gt_scr[b])).astype(bf16)

    @pl.when(j == pl.num_programs(1) - 1)
    def _():
        so_ref[...] = st_scr[...]


def _hgrn(u, w, s0, c, bt, slot):
    consts = [w["wq"], w["wf"], w["wi"], w["wg"], w["lb"], w["nrm"]]
    scratch = ([pltpu.VMEM((bt, HG_HEADS, HG_DK, HG_DV), f32)] + [pltpu.VMEM((bt, HG_HEADS, c, HG_DK), f32)] * 4
               + [pltpu.VMEM((bt, c, BRANCH_W), f32)])
    return _mixer_call(_hgrn_body, "hgrn", u, consts, [s0], c, bt, [], scratch, slot)


def _unit_lower_inverses(a_list, r, c):
    ii, jj = _iota2(r, r, 0), _iota2(r, r, 1)
    eye = jnp.where(ii == jj, 1.0, 0.0)
    t_list = [eye for _ in a_list]
    m = 1
    while m < c:
        lm = _log2(m)
        join = ((ii >> (lm + 1)) == (jj >> (lm + 1))) & (((ii >> lm) & 1) == 1) & (((jj >> lm) & 1) == 0)
        if m == 1:
            t_list = [eye - jnp.where(join, a, 0.0) for a in a_list]
        else:
            ft_list = [_dot(jnp.where(join, a, 0.0), t) for a, t in zip(a_list, t_list)]
            t_list = [t - _dot(t, ft) for t, ft in zip(t_list, ft_list)]
        m *= 2
    return t_list


def _gdn_body(u_ref, wqkv_ref, wz_ref, wab_ref, cw_ref, dtb_ref, alog_ref, nrm_ref,
              s0_ref, c0_ref, y_ref, so_ref, co_ref, s_scr, xp_scr, z_scr, q_scr, k_scr, v_scr, cum_scr, beta_scr):
    bt, c, _ = y_ref.shape
    j = pl.program_id(1)
    nh = GDN_HEADS
    hk = nh * GDN_DK
    n, r, ng = _stack_geometry(bt, c, nh)

    @pl.when(j == 0)
    def _():
        s_scr[...] = s0_ref[...]
        xp_scr[:, 0:CONV_PAD, :] = c0_ref[...]

    parts = _row_parts(bt, c)
    bp = bt // parts
    gpp = ng
    ltri = _blockdiag_ltri(bp * c, c)
    u = u_ref[...].reshape(bt * c, D_MODEL)
    lane_bcast = lambda col: jnp.broadcast_to(col, (bp * c, LANES)).reshape(bp, c, LANES)
    ii, jj = _iota2(r, r, 0), _iota2(r, r, 1)
    same = (ii >> _log2(c)) == (jj >> _log2(c))
    incl = same & (ii >= jj)
    strict = same & (ii > jj)
    rows = lambda a, i: a[i * c:(i + 1) * c]
    nb = r // c
    ab_parts = {}

    def project(p):
        sq = slice(p * bp, (p + 1) * bp)
        up = u[p * bp * c:(p + 1) * bp * c]
        xp_scr[sq, CONV_PAD:CONV_PAD + c, :] = jnp.dot(up, wqkv_ref[...], preferred_element_type=f32).reshape(
            bp, c, GDN_CONV_DIM)
        z_scr[sq] = jnp.dot(up, wz_ref[...], preferred_element_type=f32).reshape(bp, c, BRANCH_W)
        ab_parts[p] = jnp.dot(up, wab_ref[...], preferred_element_type=f32)

    def front(p):
        sq = slice(p * bp, (p + 1) * bp)
        ab = ab_parts.pop(p)
        log_g = -jnp.exp(alog_ref[...]) * jax.nn.softplus(ab + dtb_ref[...])
        cum = _dot_sel_left(ltri, log_g)
        beta = jax.nn.sigmoid(ab)
        qkv = _silu(_causal_conv(xp_scr.at[sq], cw_ref, c))
        tail = xp_scr[sq, c:c + CONV_PAD, :]
        xp_scr[sq, 0:CONV_PAD, :] = tail
        co_ref[sq] = tail
        for h in range(nh):
            q = qkv[:, :, h * GDN_DK:(h + 1) * GDN_DK]
            k = qkv[:, :, hk + h * GDN_DK:hk + (h + 1) * GDN_DK]
            q_scr[sq, h] = q * lax.rsqrt(jnp.sum(q * q, axis=-1, keepdims=True) + EPS) * (GDN_DK ** -0.5)
            k_scr[sq, h] = k * lax.rsqrt(jnp.sum(k * k, axis=-1, keepdims=True) + EPS)
            v_scr[sq, h] = qkv[:, :, 2 * hk + h * GDN_DV:2 * hk + (h + 1) * GDN_DV]
            cum_scr[sq, h] = lane_bcast(cum[:, h:h + 1])
            beta_scr[sq, h] = lane_bcast(beta[:, nh + h:nh + h + 1])

    def back(p):
        groups = range(p * gpp, (p + 1) * gpp)
        grp = lambda ref: [_group_rows(ref, g, r) for g in groups]
        qs, ks, vs, cb, bb = grp(q_scr), grp(k_scr), grp(v_scr), grp(cum_scr), grp(beta_scr)
        loc = range(gpp)
        dec, a_mat, e_in, sol, scd, q_in = [], [], [], [], [], []
        for g in loc:
            dec.append(jnp.exp(jnp.where(incl, _widen(cb[g], r) - cb[g].T[0:1, :], -jnp.inf)))
            a_mat.append(jnp.where(strict, _widen(bb[g], r) * dec[g] * _dot_nt(ks[g], ks[g]), 0.0))
        t_inv = _unit_lower_inverses(a_mat, r, c)
        for g in loc:
            e_in.append(jnp.exp(cb[g]))
            sol.append(_dot(t_inv[g], jnp.concatenate([vs[g] * bb[g], ks[g] * (e_in[g] * bb[g])], axis=-1)))
            scd.append(_dot_nt(qs[g], ks[g]) * dec[g])
            q_in.append(qs[g] * e_in[g])
        o_all = []
        for g in loc:
            blocks = [divmod((p * gpp + g) * nb + i, nh) for i in range(nb)]
            s_old = [s_scr[b, h] for b, h in blocks]
            proj = [_dot(jnp.concatenate([rows(sol[g], i)[:, GDN_DV:], rows(q_in[g], i)], axis=0), s_old[i])
                    for i in range(nb)]
            w_new = [rows(sol[g], i)[:, :GDN_DV] - proj[i][:c] for i in range(nb)]
            for i, (b, h) in enumerate(blocks):
                cum_i = rows(cb[g], i)
                clast = cum_i[c - 1:c]
                s_scr[b, h] = s_old[i] * jnp.exp(clast) + _dot_tn(rows(ks[g], i) * jnp.exp(clast - cum_i), w_new[i])
            o_all.append(_dot(scd[g], jnp.concatenate(w_new, axis=0))
                         + jnp.concatenate([pj[c:] for pj in proj], axis=0))
        o = o_all[0] if gpp == 1 else jnp.concatenate(o_all, axis=0)
        for b in range(bt):
            y = jnp.concatenate([_rms(rows(o, b * nh + h)) * nrm_ref[...] for h in range(nh)], axis=-1)
            y_ref[b] = (y * _silu(z_scr[b])).astype(bf16)

    _pipeline_parts(parts, project, front)
    back(0)

    @pl.when(j == pl.num_programs(1) - 1)
    def _():
        so_ref[...] = s_scr[...]


def _gdn(u, w, s0, conv0, c, bt, slot):
    b, l, _ = u.shape
    consts = [w["wqkv"], w["wz"], w["wab"], w["cw"], w["dtb"], w["alog"], w["nrm"]]
    scratch = [
        pltpu.VMEM((bt, GDN_HEADS, GDN_DK, GDN_DV), f32),
        pltpu.VMEM((bt, CONV_PAD + c, GDN_CONV_DIM), f32),
        pltpu.VMEM((bt, c, BRANCH_W), f32),
    ] + [pltpu.VMEM((bt, GDN_HEADS, c, LANES), f32)] * 5
    return _mixer_call(_gdn_body, "gdn", u, consts, [s0, conv0], c, bt,
                       [(b, CONV_PAD, GDN_CONV_DIM)], scratch, slot)


def _pad_lanes(v, offset=0):
    row = jnp.zeros((1, LANES), f32)
    return row.at[0, offset:offset + v.shape[0]].set(v.astype(f32))


def _layer_weights(l, norm_g, ffn_w_in, ffn_w_out, w_in, w_branch, w_out, mb_conv_w, mb_conv_b, mb_a_log,
                   mb_dt_bias, mb_d, mb_norm, lb_all, hg_norm, gdn_conv_w, gdn_a_log, gdn_dt_bias, gdn_norm):
    wi = w_in[l]
    col = lambda n: wi[:, IN_OFFSETS[n]:IN_OFFSETS[n + 1]]
    colb = lambda n: col(n).astype(bf16)
    pad_cols = lambda a: jnp.pad(a, ((0, 0), (0, LANES - a.shape[1]))).astype(bf16)
    hw = RET_DK // 2
    perm = np.arange(RET_HEADS * RET_DK).reshape(RET_HEADS, 2, hw).transpose(1, 0, 2).reshape(-1)
    row = lambda v: v.astype(f32).reshape(1, -1)
    return {
        "g_ffn1": norm_g[l, 0:3], "g_post": norm_g[l, 3:4], "g_ffn2": norm_g[l, 4:6],
        "layer": l, "ffn_in": ffn_w_in, "ffn_out": ffn_w_out,
        "mamba": {"wz": colb(0), "wx": colb(1), "wdt": pad_cols(col(2)), "cw": mb_conv_w[l],
                  "cb": row(mb_conv_b[l]),
                  "dtb": _pad_lanes(mb_dt_bias[l]), "alog": _pad_lanes(mb_a_log[l]),
                  "d": row(jnp.repeat(mb_d[l], MB_HEADDIM)), "nrm": row(mb_norm[l])},
        "hgrn": {"wq": colb(3), "wf": colb(4), "wi": colb(5), "wg": colb(6), "lb": row(lb_all[l]),
                 "nrm": row(hg_norm[l])},
        "ret": {"wq": col(7)[:, perm].astype(bf16), "wk": col(8)[:, perm].astype(bf16),
                "wv": colb(9), "wg": colb(10)},
        "gdn": {"wqkv": colb(11), "wz": colb(12),
                "wab": pad_cols(jnp.concatenate([col(13), col(14)], axis=1)),
                "cw": gdn_conv_w[l], "dtb": _pad_lanes(gdn_dt_bias[l]), "alog": _pad_lanes(gdn_a_log[l]),
                "nrm": row(gdn_norm[l])},
        "w_gate": colb(15), "w_branch": w_branch[l].astype(bf16), "w_out": w_out[l].astype(bf16),
    }


def _rope_tables(pos):
    half = RET_DK // 2
    inv_freq = 1.0 / (ROPE_BASE ** jnp.linspace(0.0, 1.0, half, dtype=f32))
    ang = pos.astype(f32)[:, None] * inv_freq[None, :]
    reps = LANES // half
    return jnp.tile(jnp.cos(ang), (1, reps)), jnp.tile(jnp.sin(ang), (1, reps))


def _pad_conv(buf):
    return jnp.pad(buf, ((0, 0), (CONV_PAD - (CONV_W - 1), 0), (0, 0)))


def _layer(h, w, states, rope, lg, chunks, bt, depth, stacked):
    b, l, d = h.shape
    s_ssm, s_ssm_conv, s_hg, s_ret, s_gdn, s_gdn_conv = states
    slot = lambda n: (w["layer"], depth, None if stacked is None else stacked[n])
    h, u2 = _ffn(h.reshape(b * l, d), w["g_ffn1"], w["ffn_in"], w["ffn_out"], w["layer"], 0, emit_normed=True)
    u = u2.reshape(b, l, d)
    y_mb, n_ssm, n_ssm_conv = _mamba(u, w["mamba"], s_ssm, _pad_conv(s_ssm_conv), chunks["mamba"], bt, slot(0))
    y_hg, n_hg = _hgrn(u, w["hgrn"], s_hg, chunks["hgrn"], bt, slot(1))
    y_rt, n_ret = _ret(u, w["ret"], rope[0], rope[1], lg, s_ret, chunks["ret"], bt, slot(2))
    y_gd, n_gdn, n_gdn_conv = _gdn(u, w["gdn"], s_gdn, _pad_conv(s_gdn_conv), chunks["gdn"], bt, slot(3))
    ys = [y.reshape(b * l, BRANCH_W) for y in (y_mb, y_hg, y_rt, y_gd)]
    hf = _merge(h, u2, w["g_post"], ys, w["w_gate"], w["w_branch"], w["w_out"])
    hf = _ffn(hf, w["g_ffn2"], w["ffn_in"], w["ffn_out"], w["layer"], 1)
    convs = (n_ssm_conv[:, CONV_PAD - (CONV_W - 1):], n_gdn_conv[:, CONV_PAD - (CONV_W - 1):])
    return hf.reshape(b, l, d), (n_ssm, n_hg, n_ret, n_gdn), convs


def _chunks(l):
    pick = lambda pref: pref if l % pref == 0 else math.gcd(l, pref)
    return {"mamba": pick(128), "ret": pick(128), "hgrn": pick(64), "gdn": pick(64)}


def kernel(x_prompt, x_sample, state_ssm, state_ssm_conv, state_hgrn, state_ret, state_gdn, state_gdn_conv,
           norm_g, ffn_w_in, ffn_w_out, w_in, w_branch, w_out, mb_conv_w, mb_conv_b, mb_a_log, mb_dt_bias, mb_d,
           mb_norm, hg_lb_logits, hg_norm, gdn_conv_w, gdn_a_log, gdn_dt_bias, gdn_norm):
    depth = norm_g.shape[0]
    lb_all = jnp.cumsum(jax.nn.softmax(hg_lb_logits.astype(f32), axis=0), axis=0)
    lb_all = lb_all - lb_all[0:1]
    lg = _pad_lanes(jnp.log(1.0 - jnp.exp2(-5.0 - jnp.arange(RET_HEADS, dtype=f32))))

    bp, lp, _ = x_prompt.shape
    bs, ls, _ = x_sample.shape
    rope_p = _rope_tables(jnp.arange(lp))
    rope_s = _rope_tables(PAST_LEN + jnp.arange(ls))
    zero_states = (
        (jnp.zeros((1, bp, MB_HEADS, MB_STATE, MB_HEADDIM), f32), 0),
        jnp.zeros((bp, CONV_W - 1, MB_CONV_DIM), f32),
        (jnp.zeros((1, bp, HG_HEADS, HG_DK, HG_DV), f32), 0),
        (jnp.zeros((1, bp, RET_HEADS, RET_DK, RET_DV), f32), 0),
        (jnp.zeros((1, bp, GDN_HEADS, GDN_DK, GDN_DV), f32), 0),
        jnp.zeros((bp, CONV_W - 1, GDN_CONV_DIM), f32),
    )
    bt_p = math.gcd(bp, 8)
    bt_s = math.gcd(bs, 16)

    ffn_w_in = ffn_w_in.astype(bf16)
    ffn_w_out = ffn_w_out.astype(bf16)
    hp, hs = x_prompt, x_sample
    st_p = st_s = None
    cv_p, cv_s = [], []
    for l in range(depth):
        w = _layer_weights(l, norm_g, ffn_w_in, ffn_w_out, w_in, w_branch, w_out, mb_conv_w, mb_conv_b,
                           mb_a_log, mb_dt_bias, mb_d, mb_norm, lb_all, hg_norm, gdn_conv_w, gdn_a_log,
                           gdn_dt_bias, gdn_norm)
        hp, st_p, cp = _layer(hp, w, zero_states, rope_p, lg, _chunks(lp), bt_p, depth, st_p)
        past = ((state_ssm, l), state_ssm_conv[l], (state_hgrn, l), (state_ret, l), (state_gdn, l), state_gdn_conv[l])
        hs, st_s, cs = _layer(hs, w, past, rope_s, lg, _chunks(ls), bt_s, depth, st_s)
        cv_p.append(cp)
        cv_s.append(cs)

    stack = lambda cvs, i: jnp.stack([cv[i] for cv in cvs], axis=0)
    return (hp, hs,
            st_p[0], stack(cv_p, 0), st_p[1], st_p[2], st_p[3], stack(cv_p, 1),
            st_s[0], stack(cv_s, 0), st_s[1], st_s[2], st_s[3], stack(cv_s, 1))
```

```python
import functools
import math

import numpy as np
import jax
import jax.numpy as jnp
from jax import lax
from jax.experimental import pallas as pl
from jax.experimental.pallas import tpu as pltpu

f32 = jnp.float32
bf16 = jnp.bfloat16

D_MODEL = 1024
D_FF = 2816
BRANCH_W = 512
N_BRANCH = 4
CONV_W = 4
EPS = 1e-6
ROPE_BASE = 10000.0
PAST_LEN = 16384

MB_HEADS, MB_HEADDIM, MB_STATE, MB_GROUPS = 8, 64, 64, 2
MB_CONV_DIM = BRANCH_W + 2 * MB_GROUPS * MB_STATE
HG_HEADS, HG_DK, HG_DV = 4, 128, 128
RET_HEADS, RET_DK, RET_DV = 4, 64, 128
GDN_HEADS, GDN_DK, GDN_DV = 4, 128, 128
GDN_CONV_DIM = 2 * GDN_HEADS * GDN_DK + GDN_HEADS * GDN_DV

IN_SPLITS = (
    BRANCH_W, MB_CONV_DIM, MB_HEADS,
    HG_HEADS * HG_DK, HG_HEADS * HG_DK, BRANCH_W, BRANCH_W,
    RET_HEADS * RET_DK, RET_HEADS * RET_DK, BRANCH_W, BRANCH_W,
    GDN_CONV_DIM, BRANCH_W, GDN_HEADS, GDN_HEADS,
    N_BRANCH * D_MODEL,
)
IN_OFFSETS = tuple(int(v) for v in np.cumsum((0,) + IN_SPLITS))

LANES = 128
SUBLANES = 8
MXU_DIM = 256
STACK_ROWS = 128
CONV_PAD = SUBLANES
VMEM_LIMIT = 56 * 1024 * 1024
FFN_TM = 512
MERGE_TM = 512
HG_SUB = SUBLANES
FRONT_ROWS = 128
FFN_PART_ROWS = 256


def _dot(a, b):
    return jnp.dot(a.astype(bf16), b.astype(bf16), preferred_element_type=f32)


def _dot_nt(a, b):
    return lax.dot_general(a.astype(bf16), b.astype(bf16), (((1,), (1,)), ((), ())), preferred_element_type=f32)


def _dot_tn(a, b):
    return lax.dot_general(a.astype(bf16), b.astype(bf16), (((0,), (0,)), ((), ())), preferred_element_type=f32)


def _split3(x):
    hi = x.astype(bf16)
    r = x - hi.astype(f32)
    mid = r.astype(bf16)
    lo = (r - mid.astype(f32)).astype(bf16)
    return hi, mid, lo


def _dot_sel_left(m01, x):
    hi, mid, lo = _split3(x)
    return (jnp.dot(m01, hi, preferred_element_type=f32) + jnp.dot(m01, mid, preferred_element_type=f32)
            + jnp.dot(m01, lo, preferred_element_type=f32))


def _rms(x):
    return x * lax.rsqrt(jnp.mean(x * x, axis=-1, keepdims=True) + EPS)


def _silu(x):
    return x * jax.nn.sigmoid(x)


def _iota2(n, m, axis):
    return lax.broadcasted_iota(jnp.int32, (n, m), axis)


def _log2(n):
    assert n > 0 and n & (n - 1) == 0, n
    return n.bit_length() - 1


def _ltri(c):
    return jnp.where(_iota2(c, c, 0) >= _iota2(c, c, 1), 1.0, 0.0).astype(bf16)


def _blockdiag_ltri(n, c):
    ii, jj = _iota2(n, n, 0), _iota2(n, n, 1)
    lc = _log2(c)
    return jnp.where(((ii >> lc) == (jj >> lc)) & (ii >= jj), 1.0, 0.0).astype(bf16)


def _row_bcast(col, c):
    dg = jnp.where(_iota2(c, c, 0) == _iota2(c, c, 1), jnp.broadcast_to(col, (c, c)), 0.0)
    return _dot_sel_left(jnp.ones((c, c), bf16), dg)


def _decay_matrix(col, c):
    diff = col - _row_bcast(col, c)
    return jnp.exp(jnp.where(_iota2(c, c, 0) >= _iota2(c, c, 1), diff, -jnp.inf))


def _causal_conv(xp_ref, w_ref, c):
    y = xp_ref[:, pl.ds(CONV_PAD - 3, c), :] * w_ref[0:1, :]
    for j in range(1, CONV_W):
        y = y + xp_ref[:, pl.ds(CONV_PAD - 3 + j, c), :] * w_ref[j:j + 1, :]
    return y


def _stack_geometry(bt, c, heads):
    n = bt * heads * c
    r = min(STACK_ROWS, n)
    assert n % r == 0 and r % c == 0 and r % LANES == 0
    return n, r, n // r


def _row_parts(bt, c):
    parts = max(1, (bt * c) // FRONT_ROWS)
    assert bt % parts == 0
    return parts


def _pipeline_parts(parts, *stages):
    ns = len(stages)
    for t in range(parts + ns - 1):
        for s, stage in enumerate(stages):
            if 0 <= t - s < parts:
                stage(t - s)


def _group_rows(ref, g, r):
    _, heads, c, w = ref.shape
    bpg = r // c
    if bpg <= heads:
        assert heads % bpg == 0
        b, h0 = divmod(g * bpg, heads)
        return ref[b, h0:h0 + bpg].reshape(r, w)
    assert bpg % heads == 0
    nseq = bpg // heads
    return ref[g * nseq:(g + 1) * nseq].reshape(r, w)


def _widen(x, r):
    return x if r == LANES else jnp.concatenate([x] * (r // LANES), axis=-1)


def _ffn_body(x_ref, g_ref, wg_ref, wu_ref, wo_ref, o_ref, *un_ref):
    tm = x_ref.shape[0]
    parts = max(1, tm // FFN_PART_ROWS)
    rp = tm // parts
    xn, act = {}, {}

    def pre_norm(p):
        xn[p] = (_rms(x_ref[p * rp:(p + 1) * rp, :]) * g_ref[0:1, :]).astype(bf16)

    def expand(p):
        x_p = xn.pop(p)
        gate = jnp.dot(x_p, wg_ref[...], preferred_element_type=f32)
        up = jnp.dot(x_p, wu_ref[...], preferred_element_type=f32)
        act[p] = (_silu(gate) * up).astype(bf16)

    def contract(p):
        rows = slice(p * rp, (p + 1) * rp)
        y = jnp.dot(act.pop(p), wo_ref[...], preferred_element_type=f32)
        out = x_ref[rows, :] + 0.5 * (_rms(y) * g_ref[1:2, :])
        o_ref[rows, :] = out
        if un_ref:
            un_ref[0][rows, :] = (_rms(out) * g_ref[2:3, :]).astype(bf16)

    _pipeline_parts(parts, pre_norm, expand, contract)


def _ffn(x, g, w_in, w_out, l, k, emit_normed=False):
    t = x.shape[0]
    tm = min(FFN_TM, t)
    resident = pl.Buffered(1)
    row = pl.BlockSpec((tm, D_MODEL), lambda i: (i, 0))
    out_specs, out_shape = row, jax.ShapeDtypeStruct((t, D_MODEL), f32)
    if emit_normed:
        out_specs, out_shape = [row, row], [out_shape, jax.ShapeDtypeStruct((t, D_MODEL), bf16)]
    return pl.pallas_call(
        _ffn_body,
        grid=(t // tm,),
        in_specs=[
            row,
            pl.BlockSpec(g.shape, lambda i: (0, 0)),
            pl.BlockSpec((None, None, D_MODEL, D_FF), lambda i: (l, k, 0, 0), pipeline_mode=resident),
            pl.BlockSpec((None, None, D_MODEL, D_FF), lambda i: (l, k, 0, 1), pipeline_mode=resident),
            pl.BlockSpec((None, None, D_FF, D_MODEL), lambda i: (l, k, 0, 0), pipeline_mode=resident),
        ],
        out_specs=out_specs,
        out_shape=out_shape,
        compiler_params=pltpu.CompilerParams(dimension_semantics=("arbitrary",), vmem_limit_bytes=VMEM_LIMIT),
        name="ffn",
    )(x, g, w_in, w_in, w_out)


def _merge_body(h_ref, u_ref, g_ref, y0_ref, y1_ref, y2_ref, y3_ref, wg_ref, wb_ref, wo_ref, o_ref):
    tm = h_ref.shape[0]
    parts = max(1, tm // FFN_PART_ROWS)
    rp = tm // parts
    merged_parts = {}

    def gate_and_project(p):
        rows = slice(p * rp, (p + 1) * rp)
        u = u_ref[rows, :]
        merged = jnp.zeros((rp, D_MODEL), f32)
        for n, y_ref in enumerate((y0_ref, y1_ref, y2_ref, y3_ref)):
            gate = jax.nn.sigmoid(jnp.dot(u, wg_ref[:, n * D_MODEL:(n + 1) * D_MODEL], preferred_element_type=f32))
            merged = merged + gate * jnp.dot(y_ref[rows, :], wb_ref[n], preferred_element_type=f32)
        merged_parts[p] = merged.astype(bf16)

    def mix_out(p):
        rows = slice(p * rp, (p + 1) * rp)
        mixed = jnp.dot(merged_parts.pop(p), wo_ref[...], preferred_element_type=f32)
        o_ref[rows, :] = h_ref[rows, :] + _rms(mixed) * g_ref[...]

    _pipeline_parts(parts, gate_and_project, mix_out)


def _merge(h, u, g_post, ys, w_gate, w_branch, w_out):
    t = h.shape[0]
    tm = min(MERGE_TM, t)
    row = lambda i: (i, 0)
    resident = pl.Buffered(1)
    return pl.pallas_call(
        _merge_body,
        grid=(t // tm,),
        in_specs=[pl.BlockSpec((tm, D_MODEL), row), pl.BlockSpec((tm, D_MODEL), row),
                  pl.BlockSpec((1, D_MODEL), lambda i: (0, 0))]
                 + [pl.BlockSpec((tm, BRANCH_W), row)] * N_BRANCH
                 + [pl.BlockSpec((D_MODEL, N_BRANCH * D_MODEL), lambda i: (0, 0), pipeline_mode=resident),
                    pl.BlockSpec((N_BRANCH, BRANCH_W, D_MODEL), lambda i: (0, 0, 0), pipeline_mode=resident),
                    pl.BlockSpec((D_MODEL, D_MODEL), lambda i: (0, 0), pipeline_mode=resident)],
        out_specs=pl.BlockSpec((tm, D_MODEL), row),
        out_shape=jax.ShapeDtypeStruct((t, D_MODEL), f32),
        compiler_params=pltpu.CompilerParams(dimension_semantics=("arbitrary",), vmem_limit_bytes=VMEM_LIMIT),
        name="merge",
    )(h, u, g_post, *ys, w_gate, w_branch, w_out)


def _const_spec(a):
    nd = a.ndim
    return pl.BlockSpec(a.shape, lambda i, j, _nd=nd: (0,) * _nd)


def _batch_spec(bt, a):
    nd = a.ndim
    return pl.BlockSpec((bt,) + a.shape[1:], lambda i, j, _nd=nd: (i,) + (0,) * (_nd - 1))


def _mixer_call(body, name, u, consts, batch_ins, c, bt, batch_outs, scratch, slot, chunk_ins=()):
    b, l, d = u.shape
    layer, depth, prev = slot
    seq = lambda i, j: (i, j, 0)
    if c == l:
        u, u_spec = u.reshape(b * l, d), pl.BlockSpec((bt * c, d), lambda i, j: (i, 0))
    else:
        u_spec = pl.BlockSpec((bt, c, d), seq)
    st_in, st_entry = batch_ins[0]
    st = tuple(st_in.shape[1:])
    st_block = (None, bt) + st[1:]
    out_shape = ([jax.ShapeDtypeStruct((b, l, BRANCH_W), bf16), jax.ShapeDtypeStruct((depth,) + st, f32)]
                 + [jax.ShapeDtypeStruct(s, f32) for s in batch_outs])
    out_specs = ([pl.BlockSpec((bt, c, BRANCH_W), seq),
                  pl.BlockSpec(st_block, lambda i, j: (layer, i) + (0,) * (len(st) - 1))]
                 + [pl.BlockSpec((bt,) + s[1:], lambda i, j, _nd=len(s): (i,) + (0,) * (_nd - 1))
                    for s in batch_outs])
    operands = [u, *consts, *chunk_ins, st_in, *batch_ins[1:]]
    in_specs = ([u_spec] + [_const_spec(a) for a in consts]
                + [pl.BlockSpec((c, a.shape[1]), lambda i, j: (j, 0)) for a in chunk_ins]
                + [pl.BlockSpec(st_block, lambda i, j: (st_entry, i) + (0,) * (len(st) - 1))]
                + [_batch_spec(bt, a) for a in batch_ins[1:]])
    n_in = len(operands)
    aliases = {}
    fn = body
    if prev is not None:
        operands.append(prev)
        in_specs.append(pl.BlockSpec(memory_space=pl.ANY))
        aliases = {n_in: 1}
        fn = lambda *refs: body(*refs[:n_in], *refs[n_in + 1:])
    elif depth > 1:
        out_specs[1] = pl.BlockSpec((depth, bt) + st[1:], lambda i, j: (0, i) + (0,) * (len(st) - 1))

        def fn(*refs):
            so_ref = refs[n_in + 1]

            @pl.when(pl.program_id(1) == 0)
            def _():
                for e in range(depth):
                    if e != layer:
                        so_ref[e] = jnp.zeros(so_ref.shape[1:], f32)

            body(*refs[:n_in + 1], so_ref.at[layer], *refs[n_in + 2:])

    return pl.pallas_call(
        fn,
        grid=(b // bt, l // c),
        in_specs=in_specs,
        out_specs=out_specs,
        out_shape=out_shape,
        scratch_shapes=scratch,
        input_output_aliases=aliases,
        compiler_params=pltpu.CompilerParams(dimension_semantics=("arbitrary", "arbitrary"),
                                             vmem_limit_bytes=VMEM_LIMIT),
        name=name,
    )(*operands)


def _mamba_body(u_ref, wz_ref, wx_ref, wdt_ref, cw_ref, cb_ref, dtb_ref, alog_ref, d_ref, nrm_ref,
                s0_ref, c0_ref, y_ref, so_ref, co_ref, s_scr, xp_scr, z_scr, xs_scr, v_scr, cum_scr):
    bt, c, _ = y_ref.shape
    j = pl.program_id(1)
    nh = MB_HEADS
    npair = nh // 2
    hpg = nh // MB_GROUPS
    n = bt * nh * c
    r = min(MXU_DIM, n)
    assert n % r == 0 and r % c == 0
    ng = n // r

    @pl.when(j == 0)
    def _():
        s_scr[...] = jnp.zeros_like(s_scr)
        for p in range(npair):
            g = (2 * p) // hpg
            s_scr[:, p, g * MB_STATE:(g + 1) * MB_STATE, :] = jnp.concatenate(
                [s0_ref[:, 2 * p], s0_ref[:, 2 * p + 1]], axis=-1)
        xp_scr[:, 0:CONV_PAD, :] = c0_ref[...]

    parts = _row_parts(bt, c)
    bp = bt // parts
    u = u_ref[...].reshape(bt * c, D_MODEL)
    ltri = _blockdiag_ltri(bp * c, c)
    lane3 = lax.broadcasted_iota(jnp.int32, (bp, c, LANES), 2)
    lane_bcast = lambda col: jnp.broadcast_to(col, (bp * c, LANES)).reshape(bp, c, LANES)
    dt_parts, bm_parts, cm_parts = {}, {}, {}

    def project(p):
        sq = slice(p * bp, (p + 1) * bp)
        up = u[p * bp * c:(p + 1) * bp * c]
        z_scr[sq] = jnp.dot(up, wz_ref[...], preferred_element_type=f32).reshape(bp, c, BRANCH_W)
        xp_scr[sq, CONV_PAD:CONV_PAD + c, :] = jnp.dot(up, wx_ref[...], preferred_element_type=f32).reshape(
            bp, c, MB_CONV_DIM)
        dt_parts[p] = jnp.dot(up, wdt_ref[...], preferred_element_type=f32)

    def front(p):
        sq = slice(p * bp, (p + 1) * bp)
        dt = jax.nn.softplus(dt_parts.pop(p) + dtb_ref[...])
        cum = _dot_sel_left(ltri, -jnp.exp(alog_ref[...]) * dt)
        xa = _silu(_causal_conv(xp_scr.at[sq], cw_ref, c) + cb_ref[...])
        tail = xp_scr[sq, c:c + CONV_PAD, :]
        xp_scr[sq, 0:CONV_PAD, :] = tail
        co_ref[sq] = tail
        xs = xa[:, :, :BRANCH_W]
        bm_parts[p] = xa[:, :, BRANCH_W:BRANCH_W + LANES]
        cm = xa[:, :, BRANCH_W + LANES:BRANCH_W + 2 * LANES]
        cm_parts[p] = [jnp.where((lane3 >> _log2(MB_STATE)) == g, cm, 0.0) for g in range(MB_GROUPS)]
        xs_scr[sq] = xs
        for h in range(nh):
            own_half = (lane3 >> _log2(MB_HEADDIM)) == (h % 2)
            v_scr[sq, h] = jnp.where(
                own_half, xs[:, :, (h // 2) * LANES:(h // 2 + 1) * LANES] * lane_bcast(dt[:, h:h + 1]), 0.0)
            cum_scr[sq, h] = lane_bcast(cum[:, h:h + 1])

    _pipeline_parts(parts, project, front)
    vf = v_scr[...].reshape(n, LANES)
    cbf = cum_scr[...].reshape(n, LANES)
    blocks_of = lambda first, count: [divmod(first + i, nh) for i in range(count)]
    c_rows = lambda blocks: jnp.concatenate([cm_parts[b // bp][h // hpg][b % bp] for b, h in blocks], axis=0)
    b_rows = lambda blocks: jnp.concatenate([bm_parts[b // bp][b % bp] for b, _ in blocks], axis=0)

    ii, jj = _iota2(r, r, 0), _iota2(r, r, 1)
    incl = ((ii >> _log2(c)) == (jj >> _log2(c))) & (ii >= jj)
    grp = lambda a, g: a[g * r:(g + 1) * r]
    gblocks = [blocks_of(g * (r // c), r // c) for g in range(ng)]
    dec = lambda g: jnp.exp(jnp.where(incl, _widen(grp(cbf, g), r) - grp(cbf, g).T[0:1, :], -jnp.inf))
    o_loc = [_dot(_dot_nt(c_rows(gblocks[g]), b_rows(gblocks[g])) * dec(g), grp(vf, g)) for g in range(ng)]
    o_loc = o_loc[0] if ng == 1 else jnp.concatenate(o_loc, axis=0)

    cb3 = cbf.reshape(n // c, c, LANES)
    k_out = (vf.reshape(n // c, c, LANES) * jnp.exp(cb3[:, c - 1:c, :] - cb3)).reshape(n, LANES)
    e_in = jnp.exp(cbf)
    lo_half = _iota2(1, LANES, 1) < MB_HEADDIM
    row_lo = _iota2(2 * c, LANES, 0) < c
    own = row_lo == (_iota2(2 * c, LANES, 1) < MB_HEADDIM)
    pairs = [(b, p) for b in range(bt) for p in range(npair)]
    prow = lambda a, i: a[i * 2 * c:(i + 1) * 2 * c]
    s_old = [s_scr[b, p] for b, p in pairs]
    pblocks = [blocks_of(i * 2, 2) for i in range(len(pairs))]
    o_st = [jnp.where(own, _dot(c_rows(pblocks[i]), s_old[i]) * prow(e_in, i), 0.0) for i in range(len(pairs))]
    for i, (b, p) in enumerate(pairs):
        cum_i = prow(cbf, i)
        e_all = jnp.exp(jnp.where(lo_half, cum_i[c - 1:c], cum_i[2 * c - 1:2 * c]))
        s_scr[b, p] = s_old[i] * e_all + _dot_tn(b_rows(pblocks[i]), prow(k_out, i))
    for b in range(bt):
        ys = []
        for p in range(npair):
            o2 = prow(o_loc, b * npair + p) + o_st[b * npair + p]
            sl = slice(p * LANES, (p + 1) * LANES)
            ys.append(o2[:c] + o2[c:] + d_ref[:, sl] * xs_scr[b, :, sl])
        y = jnp.concatenate(ys, axis=-1) * _silu(z_scr[b])
        y_ref[b] = (_rms(y) * nrm_ref[...]).astype(bf16)

    @pl.when(j == pl.num_programs(1) - 1)
    def _():
        for p in range(npair):
            g = (2 * p) // hpg
            so_ref[:, 2 * p] = s_scr[:, p, g * MB_STATE:(g + 1) * MB_STATE, 0:MB_HEADDIM]
            so_ref[:, 2 * p + 1] = s_scr[:, p, g * MB_STATE:(g + 1) * MB_STATE, MB_HEADDIM:2 * MB_HEADDIM]


def _mamba(u, w, s0, conv0, c, bt, slot):
    b, l, _ = u.shape
    npair = MB_HEADS // 2
    consts = [w["wz"], w["wx"], w["wdt"], w["cw"], w["cb"], w["dtb"], w["alog"], w["d"], w["nrm"]]
    scratch = [
        pltpu.VMEM((bt, npair, LANES, LANES), f32),
        pltpu.VMEM((bt, CONV_PAD + c, MB_CONV_DIM), f32),
        pltpu.VMEM((bt, c, BRANCH_W), f32),
        pltpu.VMEM((bt, c, BRANCH_W), f32),
    ] + [pltpu.VMEM((bt, MB_HEADS, c, LANES), f32)] * 2
    return _mixer_call(_mamba_body, "mamba", u, consts, [s0, conv0], c, bt,
                       [(b, CONV_PAD, MB_CONV_DIM)], scratch, slot)


def _ret_body(u_ref, wq_ref, wk_ref, wv_ref, wg_ref, lg_ref, cos_ref, sin_ref,
              s0_ref, y_ref, so_ref, s_scr, q_scr, k_scr, v_scr, gt_scr):
    bt, c, _ = y_ref.shape
    j = pl.program_id(1)
    kw = RET_HEADS * RET_DK
    hw = RET_DK // 2

    @pl.when(j == 0)
    def _():
        s_scr[...] = jnp.zeros_like(s_scr)
        for h in range(RET_HEADS):
            s_scr[:, h, h * hw:(h + 1) * hw, :] = s0_ref[:, h, 0:hw, :]
            s_scr[:, h, LANES + h * hw:LANES + (h + 1) * hw, :] = s0_ref[:, h, hw:2 * hw, :]

    u = u_ref[...].reshape(bt * c, D_MODEL)
    cos = cos_ref[...]
    sin = sin_ref[...]

    def rot(x):
        x = x.reshape(-1, c, kw)
        x1, x2 = x[:, :, :LANES], x[:, :, LANES:]
        return jnp.concatenate([x1 * cos - x2 * sin, x1 * sin + x2 * cos], axis=-1)

    nh = RET_HEADS
    n = bt * nh * c
    r = min(MXU_DIM, n)
    assert n % r == 0 and r % c == 0
    ng = n // r
    parts = _row_parts(bt, c)
    bp = bt // parts
    klane = lax.broadcasted_iota(jnp.int32, (bp, c, kw), 2)
    raw = {}

    def project(p):
        sq = slice(p * bp, (p + 1) * bp)
        up = u[p * bp * c:(p + 1) * bp * c]
        raw[p] = (jnp.dot(up, wq_ref[...], preferred_element_type=f32),
                  jnp.dot(up, wk_ref[...], preferred_element_type=f32),
                  jnp.dot(up, wv_ref[...], preferred_element_type=f32))
        gt_scr[sq] = jnp.dot(up, wg_ref[...], preferred_element_type=f32).reshape(bp, c, BRANCH_W)

    def front(p):
        sq = slice(p * bp, (p + 1) * bp)
        q_raw, k_raw, v_raw = raw.pop(p)
        q = rot(q_raw)
        k = rot(k_raw) * (RET_DK ** -0.5)
        v = v_raw.reshape(bp, c, BRANCH_W)
        for h in range(nh):
            own = ((klane & (LANES - 1)) >> _log2(hw)) == h
            q_scr[sq, h] = jnp.where(own, q, 0.0)
            k_scr[sq, h] = jnp.where(own, k, 0.0)
            v_scr[sq, h] = v[:, :, h * RET_DV:(h + 1) * RET_DV]

    _pipeline_parts(parts, project, front)
    qf = q_scr[...].reshape(n, kw)
    kf = k_scr[...].reshape(n, kw)
    vf = v_scr[...].reshape(n, RET_DV)

    row = _iota2(n, 1, 0)
    head = (row >> _log2(c)) & (nh - 1)
    pos = (row & (c - 1)).astype(f32)
    lg_row = jnp.zeros((n, 1), f32)
    for h in range(nh):
        lg_row = jnp.where(head == h, lg_ref[:, h:h + 1], lg_row)
    q_in = qf * jnp.exp((pos + 1.0) * lg_row)
    k_out = kf * jnp.exp((float(c) - 1.0 - pos) * lg_row)
    e_all = jnp.exp(float(c) * lg_row)

    ii, jj = _iota2(r, r, 0), _iota2(r, r, 1)
    incl = ((ii >> _log2(c)) == (jj >> _log2(c))) & (ii >= jj)
    dist = (ii - jj).astype(f32)
    grp = lambda a, g: a[g * r:(g + 1) * r]
    dec = {}
    for g in range(ng):
        key = (g * r) % (nh * c)
        if key not in dec:
            dec[key] = jnp.exp(jnp.where(incl, dist * grp(lg_row, g), -jnp.inf))
    o_loc = [_dot(_dot_nt(grp(qf, g), grp(kf, g)) * dec[(g * r) % (nh * c)], grp(vf, g)) for g in range(ng)]
    o_loc = o_loc[0] if ng == 1 else jnp.concatenate(o_loc, axis=0)

    blocks = [(b, h) for b in range(bt) for h in range(nh)]
    rows = lambda a, i: a[i * c:(i + 1) * c]
    s_old = [s_scr[b, h] for b, h in blocks]
    o_st = [_dot(rows(q_in, i), s_old[i]) for i in range(len(blocks))]
    for i, (b, h) in enumerate(blocks):
        s_scr[b, h] = s_old[i] * rows(e_all, i)[0:1] + _dot_tn(rows(k_out, i), rows(vf, i))
    for b in range(bt):
        y = jnp.concatenate([_rms(rows(o_loc, b * nh + h) + o_st[b * nh + h]) for h in range(nh)], axis=-1)
        y_ref[b] = (y * _silu(gt_scr[b])).astype(bf16)

    @pl.when(j == pl.num_programs(1) - 1)
    def _():
        for h in range(RET_HEADS):
            so_ref[:, h, 0:hw, :] = s_scr[:, h, h * hw:(h + 1) * hw, :]
            so_ref[:, h, hw:2 * hw, :] = s_scr[:, h, LANES + h * hw:LANES + (h + 1) * hw, :]


def _ret(u, w, cos, sin, lg, s0, c, bt, slot):
    kw = RET_HEADS * RET_DK
    consts = [w["wq"], w["wk"], w["wv"], w["wg"], lg]
    scratch = [
        pltpu.VMEM((bt, RET_HEADS, kw, RET_DV), f32),
        pltpu.VMEM((bt, RET_HEADS, c, kw), f32),
        pltpu.VMEM((bt, RET_HEADS, c, kw), f32),
        pltpu.VMEM((bt, RET_HEADS, c, RET_DV), f32),
        pltpu.VMEM((bt, c, BRANCH_W), f32),
    ]
    return _mixer_call(_ret_body, "ret", u, consts, [s0], c, bt, [], scratch, slot, chunk_ins=(cos, sin))


def _hgrn_body(u_ref, wq_ref, wf_ref, wi_ref, wg_ref, lb_ref, nrm_ref,
               s0_ref, y_ref, so_ref, st_scr, q_scr, k_scr, v_scr, cum_scr, gt_scr):
    bt, c, _ = y_ref.shape
    j = pl.program_id(1)
    nh = HG_HEADS
    n, r, ng = _stack_geometry(bt, c, nh)
    sub = HG_SUB

    @pl.when(j == 0)
    def _():
        st_scr[...] = s0_ref[...]

    parts = _row_parts(bt, c)
    bp = bt // parts
    u = u_ref[...].reshape(bt * c, D_MODEL)
    ltri = _blockdiag_ltri(bp * c, c)
    lb = lb_ref[...]
    raw = {}

    def project(p):
        sq = slice(p * bp, (p + 1) * bp)
        up = u[p * bp * c:(p + 1) * bp * c]
        raw[p] = (jnp.dot(up, wq_ref[...], preferred_element_type=f32),
                  jnp.dot(up, wf_ref[...], preferred_element_type=f32),
                  jnp.dot(up, wi_ref[...], preferred_element_type=f32))
        gt_scr[sq] = jnp.dot(up, wg_ref[...], preferred_element_type=f32).reshape(bp, c, BRANCH_W)

    def front(p):
        sq = slice(p * bp, (p + 1) * bp)
        q_raw, f_raw, v = raw.pop(p)
        q = _silu(q_raw)
        fg = lb + (1.0 - lb) * jax.nn.sigmoid(f_raw)
        cum = _dot_sel_left(ltri, jnp.log(fg))
        k = 1.0 - fg
        for h in range(nh):
            sl = slice(h * HG_DK, (h + 1) * HG_DK)
            q_scr[sq, h] = q[:, sl].reshape(bp, c, HG_DK)
            k_scr[sq, h] = k[:, sl].reshape(bp, c, HG_DK)
            v_scr[sq, h] = v[:, sl].reshape(bp, c, HG_DV)
            cum_scr[sq, h] = cum[:, sl].reshape(bp, c, HG_DK)

    ii, jj = _iota2(r, r, 0), _iota2(r, r, 1)
    n_p = n // parts
    g_p = n_p // r
    assert n_p % r == 0
    ones = jnp.ones((HG_DK, r), bf16)
    same_sub = (ii >> _log2(sub)) == (jj >> _log2(sub))
    part_scores = {}

    def local(p):
        sq = slice(p * bp, (p + 1) * bp)
        qf, kf, cf = (ref[sq].reshape(n_p, HG_DK) for ref in (q_scr, k_scr, cum_scr))
        scores = [jnp.zeros((r, r), f32) for _ in range(g_p)]
        q3, k3, c3 = (a.reshape(n_p // sub, sub, HG_DK) for a in (qf, kf, cf))
        sub_i = lax.broadcasted_iota(jnp.int32, q3.shape, 1)
        for t in range(sub):
            pt = jnp.where(sub_i >= t, q3 * k3[:, t:t + 1, :] * jnp.exp(c3 - c3[:, t:t + 1, :]), 0.0)
            row_sum = jnp.dot(pt.reshape(n_p, HG_DK).astype(bf16), ones, preferred_element_type=f32)
            place = same_sub & ((jj & (sub - 1)) == t)
            scores = [jnp.where(place, row_sum[g * r:(g + 1) * r], scores[g]) for g in range(g_p)]
        m = sub
        while m < c:
            q4, k4, c4 = (a.reshape(n_p // (2 * m), 2, m, HG_DK) for a in (qf, kf, cf))
            bnd = c4[:, 0:1, m - 1:m, :]
            odd = lax.broadcasted_iota(jnp.int32, q4.shape, 1) == 1
            e = jnp.exp(jnp.where(odd, c4 - bnd, bnd - c4))
            q_t = jnp.where(odd, q4 * e, 0.0).reshape(n_p, HG_DK)
            k_t = jnp.where(odd, 0.0, k4 * e).reshape(n_p, HG_DK)
            same_pair = (ii >> _log2(2 * m)) == (jj >> _log2(2 * m))
            scores = [scores[g] + jnp.where(same_pair, _dot_nt(q_t[g * r:(g + 1) * r], k_t[g * r:(g + 1) * r]), 0.0)
                      for g in range(g_p)]
            m *= 2
        part_scores[p] = scores

    _pipeline_parts(parts, project, front, local)
    scores = [s for p in range(parts) for s in part_scores[p]]
    qf = q_scr[...].reshape(n, HG_DK)
    kf = k_scr[...].reshape(n, HG_DK)
    vf = v_scr[...].reshape(n, HG_DV)
    cf = cum_scr[...].reshape(n, HG_DK)

    cb3 = cf.reshape(n // c, c, HG_DK)
    clast3 = cb3[:, c - 1:c, :]
    q_in = qf * jnp.exp(cf)
    k_out = (kf.reshape(n // c, c, HG_DK) * jnp.exp(clast3 - cb3)).reshape(n, HG_DK)
    nblk = n // c
    e_rows = jnp.exp(cb3[:, c - 1, :])
    if nblk < HG_DK:
        e_rows = jnp.concatenate([e_rows, jnp.zeros((HG_DK - nblk, HG_DK), f32)], axis=0)
    e_cols = e_rows.T
    blocks = [(b, h) for b in range(bt) for h in range(nh)]
    rows = lambda a, i: a[i * c:(i + 1) * c]
    st_old = [st_scr[b, h] for b, h in blocks]
    o_state = [_dot(rows(q_in, i), st_old[i]) for i in range(len(blocks))]
    for i, (b, h) in enumerate(blocks):
        st_scr[b, h] = st_old[i] * e_cols[:, i:i + 1] + _dot_tn(rows(k_out, i), rows(vf, i))
    o = [_dot(scores[g], vf[g * r:(g + 1) * r]) for g in range(ng)]
    o = o[0] if ng == 1 else jnp.concatenate(o, axis=0)
    for b in range(bt):
        y = jnp.concatenate([_rms(rows(o, b * nh + h) + o_state[b * nh + h]) * nrm_ref[...] for h in range(nh)],
                            axis=-1)
        y_ref[b] = (y * _silu(gt_scr[b])).astype(bf16)

    @pl.when(j == pl.num_programs(1) - 1)
    def _():
        so_ref[...] = st_scr[...]


def _hgrn(u, w, s0, c, bt, slot):
    consts = [w["wq"], w["wf"], w["wi"], w["wg"], w["lb"], w["nrm"]]
    scratch = ([pltpu.VMEM((bt, HG_HEADS, HG_DK, HG_DV), f32)] + [pltpu.VMEM((bt, HG_HEADS, c, HG_DK), f32)] * 4
               + [pltpu.VMEM((bt, c, BRANCH_W), f32)])
    return _mixer_call(_hgrn_body, "hgrn", u, consts, [s0], c, bt, [], scratch, slot)


def _unit_lower_inverses(a_list, r, c):
    ii, jj = _iota2(r, r, 0), _iota2(r, r, 1)
    eye = jnp.where(ii == jj, 1.0, 0.0)
    t_list = [eye for _ in a_list]
    m = 1
    while m < c:
        lm = _log2(m)
        join = ((ii >> (lm + 1)) == (jj >> (lm + 1))) & (((ii >> lm) & 1) == 1) & (((jj >> lm) & 1) == 0)
        if m == 1:
            t_list = [eye - jnp.where(join, a, 0.0) for a in a_list]
        else:
            ft_list = [_dot(jnp.where(join, a, 0.0), t) for a, t in zip(a_list, t_list)]
            t_list = [t - _dot(t, ft) for t, ft in zip(t_list, ft_list)]
        m *= 2
    return t_list


def _gdn_body(u_ref, wqkv_ref, wz_ref, wab_ref, cw_ref, dtb_ref, alog_ref, nrm_ref,
              s0_ref, c0_ref, y_ref, so_ref, co_ref, s_scr, xp_scr, z_scr, q_scr, k_scr, v_scr, cum_scr, beta_scr):
    bt, c, _ = y_ref.shape
    j = pl.program_id(1)
    nh = GDN_HEADS
    hk = nh * GDN_DK
    n, r, ng = _stack_geometry(bt, c, nh)

    @pl.when(j == 0)
    def _():
        s_scr[...] = s0_ref[...]
        xp_scr[:, 0:CONV_PAD, :] = c0_ref[...]

    parts = _row_parts(bt, c)
    bp = bt // parts
    ltri = _blockdiag_ltri(bp * c, c)
    u = u_ref[...].reshape(bt * c, D_MODEL)
    lane_bcast = lambda col: jnp.broadcast_to(col, (bp * c, LANES)).reshape(bp, c, LANES)
    ii, jj = _iota2(r, r, 0), _iota2(r, r, 1)
    same = (ii >> _log2(c)) == (jj >> _log2(c))
    incl = same & (ii >= jj)
    strict = same & (ii > jj)
    rows = lambda a, i: a[i * c:(i + 1) * c]
    nb = r // c
    ab_parts = {}

    def project(p):
        sq = slice(p * bp, (p + 1) * bp)
        up = u[p * bp * c:(p + 1) * bp * c]
        xp_scr[sq, CONV_PAD:CONV_PAD + c, :] = jnp.dot(up, wqkv_ref[...], preferred_element_type=f32).reshape(
            bp, c, GDN_CONV_DIM)
        z_scr[sq] = jnp.dot(up, wz_ref[...], preferred_element_type=f32).reshape(bp, c, BRANCH_W)
        ab_parts[p] = jnp.dot(up, wab_ref[...], preferred_element_type=f32)

    def front(p):
        sq = slice(p * bp, (p + 1) * bp)
        ab = ab_parts.pop(p)
        log_g = -jnp.exp(alog_ref[...]) * jax.nn.softplus(ab + dtb_ref[...])
        cum = _dot_sel_left(ltri, log_g)
        beta = jax.nn.sigmoid(ab)
        qkv = _silu(_causal_conv(xp_scr.at[sq], cw_ref, c))
        tail = xp_scr[sq, c:c + CONV_PAD, :]
        xp_scr[sq, 0:CONV_PAD, :] = tail
        co_ref[sq] = tail
        for h in range(nh):
            q = qkv[:, :, h * GDN_DK:(h + 1) * GDN_DK]
            k = qkv[:, :, hk + h * GDN_DK:hk + (h + 1) * GDN_DK]
            q_scr[sq, h] = q * lax.rsqrt(jnp.sum(q * q, axis=-1, keepdims=True) + EPS) * (GDN_DK ** -0.5)
            k_scr[sq, h] = k * lax.rsqrt(jnp.sum(k * k, axis=-1, keepdims=True) + EPS)
            v_scr[sq, h] = qkv[:, :, 2 * hk + h * GDN_DV:2 * hk + (h + 1) * GDN_DV]
            cum_scr[sq, h] = lane_bcast(cum[:, h:h + 1])
            beta_scr[sq, h] = lane_bcast(beta[:, nh + h:nh + h + 1])

    nbatch = 1
    gpp = ng // nbatch
    loc = range(gpp)
    held = {}

    def build(p):
        groups = range(p * gpp, (p + 1) * gpp)
        grp = lambda ref: [_group_rows(ref, g, r) for g in groups]
        qs, ks, vs, cb, bb = grp(q_scr), grp(k_scr), grp(v_scr), grp(cum_scr), grp(beta_scr)
        dec, a_mat = [], []
        for g in loc:
            dec.append(jnp.exp(jnp.where(incl, _widen(cb[g], r) - cb[g].T[0:1, :], -jnp.inf)))
            a_mat.append(jnp.where(strict, _widen(bb[g], r) * dec[g] * _dot_nt(ks[g], ks[g]), 0.0))
        held[p] = (qs, ks, vs, cb, bb, dec, a_mat)

    def invert(p):
        held[p] = held[p][:6] + (_unit_lower_inverses(held[p][6], r, c),)

    def finish(p):
        qs, ks, vs, cb, bb, dec, t_inv = held.pop(p)
        e_in, sol, scd, q_in = [], [], [], []
        for g in loc:
            e_in.append(jnp.exp(cb[g]))
            sol.append(_dot(t_inv[g], jnp.concatenate([vs[g] * bb[g], ks[g] * (e_in[g] * bb[g])], axis=-1)))
            scd.append(_dot_nt(qs[g], ks[g]) * dec[g])
            q_in.append(qs[g] * e_in[g])
        gb = [(g, i) for g in loc for i in range(nb)]
        blk = lambda g, i: divmod((p * gpp + g) * nb + i, nh)
        s_old = {k: s_scr[blk(*k)] for k in gb}
        proj = {(g, i): _dot(jnp.concatenate([rows(sol[g], i)[:, GDN_DV:], rows(q_in[g], i)], axis=0), s_old[g, i])
                for g, i in gb}
        w_new = {(g, i): rows(sol[g], i)[:, :GDN_DV] - proj[g, i][:c] for g, i in gb}
        for g, i in gb:
            cum_i = rows(cb[g], i)
            clast = cum_i[c - 1:c]
            s_scr[blk(g, i)] = (s_old[g, i] * jnp.exp(clast)
                                + _dot_tn(rows(ks[g], i) * jnp.exp(clast - cum_i), w_new[g, i]))
        o_all = [_dot(scd[g], jnp.concatenate([w_new[g, i] for i in range(nb)], axis=0))
                 + jnp.concatenate([proj[g, i][c:] for i in range(nb)], axis=0) for g in loc]
        o = o_all[0] if gpp == 1 else jnp.concatenate(o_all, axis=0)
        bpb = bt // nbatch
        for s in range(bpb):
            b = p * bpb + s
            y = jnp.concatenate([_rms(rows(o, s * nh + h)) * nrm_ref[...] for h in range(nh)], axis=-1)
            y_ref[b] = (y * _silu(z_scr[b])).astype(bf16)

    _pipeline_parts(parts, project, front)
    _pipeline_parts(nbatch, build, invert, finish)

    @pl.when(j == pl.num_programs(1) - 1)
    def _():
        so_ref[...] = s_scr[...]


def _gdn(u, w, s0, conv0, c, bt, slot):
    b, l, _ = u.shape
    consts = [w["wqkv"], w["wz"], w["wab"], w["cw"], w["dtb"], w["alog"], w["nrm"]]
    scratch = [
        pltpu.VMEM((bt, GDN_HEADS, GDN_DK, GDN_DV), f32),
        pltpu.VMEM((bt, CONV_PAD + c, GDN_CONV_DIM), f32),
        pltpu.VMEM((bt, c, BRANCH_W), f32),
    ] + [pltpu.VMEM((bt, GDN_HEADS, c, LANES), f32)] * 5
    return _mixer_call(_gdn_body, "gdn", u, consts, [s0, conv0], c, bt,
                       [(b, CONV_PAD, GDN_CONV_DIM)], scratch, slot)


def _pad_lanes(v, offset=0):
    row = jnp.zeros((1, LANES), f32)
    return row.at[0, offset:offset + v.shape[0]].set(v.astype(f32))


def _layer_weights(l, norm_g, ffn_w_in, ffn_w_out, w_in, w_branch, w_out, mb_conv_w, mb_conv_b, mb_a_log,
                   mb_dt_bias, mb_d, mb_norm, lb_all, hg_norm, gdn_conv_w, gdn_a_log, gdn_dt_bias, gdn_norm):
    wi = w_in[l]
    col = lambda n: wi[:, IN_OFFSETS[n]:IN_OFFSETS[n + 1]]
    colb = lambda n: col(n).astype(bf16)
    pad_cols = lambda a: jnp.pad(a, ((0, 0), (0, LANES - a.shape[1]))).astype(bf16)
    hw = RET_DK // 2
    perm = np.arange(RET_HEADS * RET_DK).reshape(RET_HEADS, 2, hw).transpose(1, 0, 2).reshape(-1)
    row = lambda v: v.astype(f32).reshape(1, -1)
    return {
        "g_ffn1": norm_g[l, 0:3], "g_post": norm_g[l, 3:4], "g_ffn2": norm_g[l, 4:6],
        "layer": l, "ffn_in": ffn_w_in, "ffn_out": ffn_w_out,
        "mamba": {"wz": colb(0), "wx": colb(1), "wdt": pad_cols(col(2)), "cw": mb_conv_w[l],
                  "cb": row(mb_conv_b[l]),
                  "dtb": _pad_lanes(mb_dt_bias[l]), "alog": _pad_lanes(mb_a_log[l]),
                  "d": row(jnp.repeat(mb_d[l], MB_HEADDIM)), "nrm": row(mb_norm[l])},
        "hgrn": {"wq": colb(3), "wf": colb(4), "wi": colb(5), "wg": colb(6), "lb": row(lb_all[l]),
                 "nrm": row(hg_norm[l])},
        "ret": {"wq": col(7)[:, perm].astype(bf16), "wk": col(8)[:, perm].astype(bf16),
                "wv": colb(9), "wg": colb(10)},
        "gdn": {"wqkv": colb(11), "wz": colb(12),
                "wab": pad_cols(jnp.concatenate([col(13), col(14)], axis=1)),
                "cw": gdn_conv_w[l], "dtb": _pad_lanes(gdn_dt_bias[l]), "alog": _pad_lanes(gdn_a_log[l]),
                "nrm": row(gdn_norm[l])},
        "w_gate": colb(15), "w_branch": w_branch[l].astype(bf16), "w_out": w_out[l].astype(bf16),
    }


def _rope_tables(pos):
    half = RET_DK // 2
    inv_freq = 1.0 / (ROPE_BASE ** jnp.linspace(0.0, 1.0, half, dtype=f32))
    ang = pos.astype(f32)[:, None] * inv_freq[None, :]
    reps = LANES // half
    return jnp.tile(jnp.cos(ang), (1, reps)), jnp.tile(jnp.sin(ang), (1, reps))


def _pad_conv(buf):
    return jnp.pad(buf, ((0, 0), (CONV_PAD - (CONV_W - 1), 0), (0, 0)))


def _layer(h, w, states, rope, lg, chunks, bt, depth, stacked):
    b, l, d = h.shape
    s_ssm, s_ssm_conv, s_hg, s_ret, s_gdn, s_gdn_conv = states
    slot = lambda n: (w["layer"], depth, None if stacked is None else stacked[n])
    h, u2 = _ffn(h.reshape(b * l, d), w["g_ffn1"], w["ffn_in"], w["ffn_out"], w["layer"], 0, emit_normed=True)
    u = u2.reshape(b, l, d)
    y_mb, n_ssm, n_ssm_conv = _mamba(u, w["mamba"], s_ssm, _pad_conv(s_ssm_conv), chunks["mamba"], bt, slot(0))
    y_hg, n_hg = _hgrn(u, w["hgrn"], s_hg, chunks["hgrn"], bt, slot(1))
    y_rt, n_ret = _ret(u, w["ret"], rope[0], rope[1], lg, s_ret, chunks["ret"], bt, slot(2))
    y_gd, n_gdn, n_gdn_conv = _gdn(u, w["gdn"], s_gdn, _pad_conv(s_gdn_conv), chunks["gdn"], bt, slot(3))
    ys = [y.reshape(b * l, BRANCH_W) for y in (y_mb, y_hg, y_rt, y_gd)]
    hf = _merge(h, u2, w["g_post"], ys, w["w_gate"], w["w_branch"], w["w_out"])
    hf = _ffn(hf, w["g_ffn2"], w["ffn_in"], w["ffn_out"], w["layer"], 1)
    convs = (n_ssm_conv[:, CONV_PAD - (CONV_W - 1):], n_gdn_conv[:, CONV_PAD - (CONV_W - 1):])
    return hf.reshape(b, l, d), (n_ssm, n_hg, n_ret, n_gdn), convs


def _chunks(l):
    pick = lambda pref: pref if l % pref == 0 else math.gcd(l, pref)
    return {"mamba": pick(128), "ret": pick(128), "hgrn": pick(64), "gdn": pick(64)}


def kernel(x_prompt, x_sample, state_ssm, state_ssm_conv, state_hgrn, state_ret, state_gdn, state_gdn_conv,
           norm_g, ffn_w_in, ffn_w_out, w_in, w_branch, w_out, mb_conv_w, mb_conv_b, mb_a_log, mb_dt_bias, mb_d,
           mb_norm, hg_lb_logits, hg_norm, gdn_conv_w, gdn_a_log, gdn_dt_bias, gdn_norm):
    depth = norm_g.shape[0]
    lb_all = jnp.cumsum(jax.nn.softmax(hg_lb_logits.astype(f32), axis=0), axis=0)
    lb_all = lb_all - lb_all[0:1]
    lg = _pad_lanes(jnp.log(1.0 - jnp.exp2(-5.0 - jnp.arange(RET_HEADS, dtype=f32))))

    bp, lp, _ = x_prompt.shape
    bs, ls, _ = x_sample.shape
    rope_p = _rope_tables(jnp.arange(lp))
    rope_s = _rope_tables(PAST_LEN + jnp.arange(ls))
    zero_states = (
        (jnp.zeros((1, bp, MB_HEADS, MB_STATE, MB_HEADDIM), f32), 0),
        jnp.zeros((bp, CONV_W - 1, MB_CONV_DIM), f32),
        (jnp.zeros((1, bp, HG_HEADS, HG_DK, HG_DV), f32), 0),
        (jnp.zeros((1, bp, RET_HEADS, RET_DK, RET_DV), f32), 0),
        (jnp.zeros((1, bp, GDN_HEADS, GDN_DK, GDN_DV), f32), 0),
        jnp.zeros((bp, CONV_W - 1, GDN_CONV_DIM), f32),
    )
    bt_p = math.gcd(bp, 8)
    bt_s = math.gcd(bs, 16)

    ffn_w_in = ffn_w_in.astype(bf16)
    ffn_w_out = ffn_w_out.astype(bf16)
    hp, hs = x_prompt, x_sample
    st_p = st_s = None
    cv_p, cv_s = [], []
    for l in range(depth):
        w = _layer_weights(l, norm_g, ffn_w_in, ffn_w_out, w_in, w_branch, w_out, mb_conv_w, mb_conv_b,
                           mb_a_log, mb_dt_bias, mb_d, mb_norm, lb_all, hg_norm, gdn_conv_w, gdn_a_log,
                           gdn_dt_bias, gdn_norm)
        hp, st_p, cp = _layer(hp, w, zero_states, rope_p, lg, _chunks(lp), bt_p, depth, st_p)
        past = ((state_ssm, l), state_ssm_conv[l], (state_hgrn, l), (state_ret, l), (state_gdn, l), state_gdn_conv[l])
        hs, st_s, cs = _layer(hs, w, past, rope_s, lg, _chunks(ls), bt_s, depth, st_s)
        cv_p.append(cp)
        cv_s.append(cs)

    stack = lambda cvs, i: jnp.stack([cv[i] for cv in cvs], axis=0)
    return (hp, hs,
            st_p[0], stack(cv_p, 0), st_p[1], st_p[2], st_p[3], stack(cv_p, 1),
            st_s[0], stack(cv_s, 0), st_s[1], st_s[2], st_s[3], stack(cv_s, 1))
```

```python
import functools
import math

import numpy as np
import jax
import jax.numpy as jnp
from jax import lax
from jax.experimental import pallas as pl
from jax.experimental.pallas import tpu as pltpu

f32 = jnp.float32
bf16 = jnp.bfloat16

D_MODEL = 1024
D_FF = 2816
BRANCH_W = 512
N_BRANCH = 4
CONV_W = 4
EPS = 1e-6
ROPE_BASE = 10000.0
PAST_LEN = 16384

MB_HEADS, MB_HEADDIM, MB_STATE, MB_GROUPS = 8, 64, 64, 2
MB_CONV_DIM = BRANCH_W + 2 * MB_GROUPS * MB_STATE
HG_HEADS, HG_DK, HG_DV = 4, 128, 128
RET_HEADS, RET_DK, RET_DV = 4, 64, 128
GDN_HEADS, GDN_DK, GDN_DV = 4, 128, 128
GDN_CONV_DIM = 2 * GDN_HEADS * GDN_DK + GDN_HEADS * GDN_DV

IN_SPLITS = (
    BRANCH_W, MB_CONV_DIM, MB_HEADS,
    HG_HEADS * HG_DK, HG_HEADS * HG_DK, BRANCH_W, BRANCH_W,
    RET_HEADS * RET_DK, RET_HEADS * RET_DK, BRANCH_W, BRANCH_W,
    GDN_CONV_DIM, BRANCH_W, GDN_HEADS, GDN_HEADS,
    N_BRANCH * D_MODEL,
)
IN_OFFSETS = tuple(int(v) for v in np.cumsum((0,) + IN_SPLITS))

LANES = 128
SUBLANES = 8
MXU_DIM = 256
STACK_ROWS = 128
CONV_PAD = SUBLANES
VMEM_LIMIT = 56 * 1024 * 1024
FFN_TM = 512
MERGE_TM = 512
HG_SUB = SUBLANES
FRONT_ROWS = 128
FFN_PART_ROWS = 128


def _dot(a, b):
    return jnp.dot(a.astype(bf16), b.astype(bf16), preferred_element_type=f32)


def _dot_nt(a, b):
    return lax.dot_general(a.astype(bf16), b.astype(bf16), (((1,), (1,)), ((), ())), preferred_element_type=f32)


def _dot_tn(a, b):
    return lax.dot_general(a.astype(bf16), b.astype(bf16), (((0,), (0,)), ((), ())), preferred_element_type=f32)


def _split3(x):
    hi = x.astype(bf16)
    r = x - hi.astype(f32)
    mid = r.astype(bf16)
    lo = (r - mid.astype(f32)).astype(bf16)
    return hi, mid, lo


def _dot_sel_left(m01, x):
    hi, mid, lo = _split3(x)
    return (jnp.dot(m01, hi, preferred_element_type=f32) + jnp.dot(m01, mid, preferred_element_type=f32)
            + jnp.dot(m01, lo, preferred_element_type=f32))


def _rms(x):
    return x * lax.rsqrt(jnp.mean(x * x, axis=-1, keepdims=True) + EPS)


def _silu(x):
    return x * jax.nn.sigmoid(x)


def _iota2(n, m, axis):
    return lax.broadcasted_iota(jnp.int32, (n, m), axis)


def _log2(n):
    assert n > 0 and n & (n - 1) == 0, n
    return n.bit_length() - 1


def _ltri(c):
    return jnp.where(_iota2(c, c, 0) >= _iota2(c, c, 1), 1.0, 0.0).astype(bf16)


def _blockdiag_ltri(n, c):
    ii, jj = _iota2(n, n, 0), _iota2(n, n, 1)
    lc = _log2(c)
    return jnp.where(((ii >> lc) == (jj >> lc)) & (ii >= jj), 1.0, 0.0).astype(bf16)


def _row_bcast(col, c):
    dg = jnp.where(_iota2(c, c, 0) == _iota2(c, c, 1), jnp.broadcast_to(col, (c, c)), 0.0)
    return _dot_sel_left(jnp.ones((c, c), bf16), dg)


def _decay_matrix(col, c):
    diff = col - _row_bcast(col, c)
    return jnp.exp(jnp.where(_iota2(c, c, 0) >= _iota2(c, c, 1), diff, -jnp.inf))


def _causal_conv(xp_ref, w_ref, c):
    y = xp_ref[:, pl.ds(CONV_PAD - 3, c), :] * w_ref[0:1, :]
    for j in range(1, CONV_W):
        y = y + xp_ref[:, pl.ds(CONV_PAD - 3 + j, c), :] * w_ref[j:j + 1, :]
    return y


def _stack_geometry(bt, c, heads):
    n = bt * heads * c
    r = min(STACK_ROWS, n)
    assert n % r == 0 and r % c == 0 and r % LANES == 0
    return n, r, n // r


def _row_parts(bt, c, rows=FRONT_ROWS):
    parts = max(1, (bt * c) // rows)
    assert bt % parts == 0
    return parts


def _pipeline_parts(parts, *stages):
    ns = len(stages)
    for t in range(parts + ns - 1):
        for s, stage in enumerate(stages):
            if 0 <= t - s < parts:
                stage(t - s)


def _group_rows(ref, g, r):
    _, heads, c, w = ref.shape
    bpg = r // c
    if bpg <= heads:
        assert heads % bpg == 0
        b, h0 = divmod(g * bpg, heads)
        return ref[b, h0:h0 + bpg].reshape(r, w)
    assert bpg % heads == 0
    nseq = bpg // heads
    return ref[g * nseq:(g + 1) * nseq].reshape(r, w)


def _widen(x, r):
    return x if r == LANES else jnp.concatenate([x] * (r // LANES), axis=-1)


def _ffn_body(x_ref, g_ref, wg_ref, wu_ref, wo_ref, o_ref, *un_ref):
    tm = x_ref.shape[0]
    parts = max(1, tm // FFN_PART_ROWS)
    rp = tm // parts
    xn, act = {}, {}

    def pre_norm(p):
        xn[p] = (_rms(x_ref[p * rp:(p + 1) * rp, :]) * g_ref[0:1, :]).astype(bf16)

    def expand(p):
        x_p = xn.pop(p)
        gate = jnp.dot(x_p, wg_ref[...], preferred_element_type=f32)
        up = jnp.dot(x_p, wu_ref[...], preferred_element_type=f32)
        act[p] = (_silu(gate) * up).astype(bf16)

    def contract(p):
        rows = slice(p * rp, (p + 1) * rp)
        y = jnp.dot(act.pop(p), wo_ref[...], preferred_element_type=f32)
        out = x_ref[rows, :] + 0.5 * (_rms(y) * g_ref[1:2, :])
        o_ref[rows, :] = out
        if un_ref:
            un_ref[0][rows, :] = (_rms(out) * g_ref[2:3, :]).astype(bf16)

    _pipeline_parts(parts, pre_norm, expand, contract)


def _ffn(x, g, w_in, w_out, l, k, emit_normed=False):
    t = x.shape[0]
    tm = min(FFN_TM, t)
    resident = pl.Buffered(1)
    row = pl.BlockSpec((tm, D_MODEL), lambda i: (i, 0))
    out_specs, out_shape = row, jax.ShapeDtypeStruct((t, D_MODEL), f32)
    if emit_normed:
        out_specs, out_shape = [row, row], [out_shape, jax.ShapeDtypeStruct((t, D_MODEL), bf16)]
    return pl.pallas_call(
        _ffn_body,
        grid=(t // tm,),
        in_specs=[
            row,
            pl.BlockSpec(g.shape, lambda i: (0, 0)),
            pl.BlockSpec((None, None, D_MODEL, D_FF), lambda i: (l, k, 0, 0), pipeline_mode=resident),
            pl.BlockSpec((None, None, D_MODEL, D_FF), lambda i: (l, k, 0, 1), pipeline_mode=resident),
            pl.BlockSpec((None, None, D_FF, D_MODEL), lambda i: (l, k, 0, 0), pipeline_mode=resident),
        ],
        out_specs=out_specs,
        out_shape=out_shape,
        compiler_params=pltpu.CompilerParams(dimension_semantics=("arbitrary",), vmem_limit_bytes=VMEM_LIMIT),
        name="ffn",
    )(x, g, w_in, w_in, w_out)


def _merge_body(h_ref, u_ref, g_ref, y0_ref, y1_ref, y2_ref, y3_ref, wg_ref, wb_ref, wo_ref, o_ref):
    tm = h_ref.shape[0]
    parts = max(1, tm // FFN_PART_ROWS)
    rp = tm // parts
    merged_parts = {}

    def gate_and_project(p):
        rows = slice(p * rp, (p + 1) * rp)
        u = u_ref[rows, :]
        merged = jnp.zeros((rp, D_MODEL), f32)
        for n, y_ref in enumerate((y0_ref, y1_ref, y2_ref, y3_ref)):
            gate = jax.nn.sigmoid(jnp.dot(u, wg_ref[:, n * D_MODEL:(n + 1) * D_MODEL], preferred_element_type=f32))
            merged = merged + gate * jnp.dot(y_ref[rows, :], wb_ref[n], preferred_element_type=f32)
        merged_parts[p] = merged.astype(bf16)

    def mix_out(p):
        rows = slice(p * rp, (p + 1) * rp)
        mixed = jnp.dot(merged_parts.pop(p), wo_ref[...], preferred_element_type=f32)
        o_ref[rows, :] = h_ref[rows, :] + _rms(mixed) * g_ref[...]

    _pipeline_parts(parts, gate_and_project, mix_out)


def _merge(h, u, g_post, ys, w_gate, w_branch, w_out):
    t = h.shape[0]
    tm = min(MERGE_TM, t)
    row = lambda i: (i, 0)
    resident = pl.Buffered(1)
    return pl.pallas_call(
        _merge_body,
        grid=(t // tm,),
        in_specs=[pl.BlockSpec((tm, D_MODEL), row), pl.BlockSpec((tm, D_MODEL), row),
                  pl.BlockSpec((1, D_MODEL), lambda i: (0, 0))]
                 + [pl.BlockSpec((tm, BRANCH_W), row)] * N_BRANCH
                 + [pl.BlockSpec((D_MODEL, N_BRANCH * D_MODEL), lambda i: (0, 0), pipeline_mode=resident),
                    pl.BlockSpec((N_BRANCH, BRANCH_W, D_MODEL), lambda i: (0, 0, 0), pipeline_mode=resident),
                    pl.BlockSpec((D_MODEL, D_MODEL), lambda i: (0, 0), pipeline_mode=resident)],
        out_specs=pl.BlockSpec((tm, D_MODEL), row),
        out_shape=jax.ShapeDtypeStruct((t, D_MODEL), f32),
        compiler_params=pltpu.CompilerParams(dimension_semantics=("arbitrary",), vmem_limit_bytes=VMEM_LIMIT),
        name="merge",
    )(h, u, g_post, *ys, w_gate, w_branch, w_out)


def _const_spec(a):
    nd = a.ndim
    return pl.BlockSpec(a.shape, lambda i, j, _nd=nd: (0,) * _nd)


def _batch_spec(bt, a):
    nd = a.ndim
    return pl.BlockSpec((bt,) + a.shape[1:], lambda i, j, _nd=nd: (i,) + (0,) * (_nd - 1))


def _mixer_call(body, name, u, consts, batch_ins, c, bt, batch_outs, scratch, slot, chunk_ins=()):
    b, l, d = u.shape
    layer, depth, prev = slot
    seq = lambda i, j: (i, j, 0)
    if c == l:
        u, u_spec = u.reshape(b * l, d), pl.BlockSpec((bt * c, d), lambda i, j: (i, 0))
    else:
        u_spec = pl.BlockSpec((bt, c, d), seq)
    st_in, st_entry = batch_ins[0]
    st = tuple(st_in.shape[1:])
    st_block = (None, bt) + st[1:]
    out_shape = ([jax.ShapeDtypeStruct((b, l, BRANCH_W), bf16), jax.ShapeDtypeStruct((depth,) + st, f32)]
                 + [jax.ShapeDtypeStruct(s, f32) for s in batch_outs])
    out_specs = ([pl.BlockSpec((bt, c, BRANCH_W), seq),
                  pl.BlockSpec(st_block, lambda i, j: (layer, i) + (0,) * (len(st) - 1))]
                 + [pl.BlockSpec((bt,) + s[1:], lambda i, j, _nd=len(s): (i,) + (0,) * (_nd - 1))
                    for s in batch_outs])
    operands = [u, *consts, *chunk_ins, st_in, *batch_ins[1:]]
    in_specs = ([u_spec] + [_const_spec(a) for a in consts]
                + [pl.BlockSpec((c, a.shape[1]), lambda i, j: (j, 0)) for a in chunk_ins]
                + [pl.BlockSpec(st_block, lambda i, j: (st_entry, i) + (0,) * (len(st) - 1))]
                + [_batch_spec(bt, a) for a in batch_ins[1:]])
    n_in = len(operands)
    aliases = {}
    fn = body
    if prev is not None:
        operands.append(prev)
        in_specs.append(pl.BlockSpec(memory_space=pl.ANY))
        aliases = {n_in: 1}
        fn = lambda *refs: body(*refs[:n_in], *refs[n_in + 1:])
    elif depth > 1:
        out_specs[1] = pl.BlockSpec((depth, bt) + st[1:], lambda i, j: (0, i) + (0,) * (len(st) - 1))

        def fn(*refs):
            so_ref = refs[n_in + 1]

            @pl.when(pl.program_id(1) == 0)
            def _():
                for e in range(depth):
                    if e != layer:
                        so_ref[e] = jnp.zeros(so_ref.shape[1:], f32)

            body(*refs[:n_in + 1], so_ref.at[layer], *refs[n_in + 2:])

    return pl.pallas_call(
        fn,
        grid=(b // bt, l // c),
        in_specs=in_specs,
        out_specs=out_specs,
        out_shape=out_shape,
        scratch_shapes=scratch,
        input_output_aliases=aliases,
        compiler_params=pltpu.CompilerParams(dimension_semantics=("arbitrary", "arbitrary"),
                                             vmem_limit_bytes=VMEM_LIMIT),
        name=name,
    )(*operands)


def _mamba_body(u_ref, wz_ref, wx_ref, wdt_ref, cw_ref, cb_ref, dtb_ref, alog_ref, d_ref, nrm_ref,
                s0_ref, c0_ref, y_ref, so_ref, co_ref, s_scr, xp_scr, z_scr, xs_scr, v_scr, cum_scr):
    bt, c, _ = y_ref.shape
    j = pl.program_id(1)
    nh = MB_HEADS
    npair = nh // 2
    hpg = nh // MB_GROUPS
    n = bt * nh * c
    r = STACK_ROWS if c >= STACK_ROWS else min(MXU_DIM, n)
    assert n % r == 0 and r % c == 0
    ng = n // r

    @pl.when(j == 0)
    def _():
        s_scr[...] = jnp.zeros_like(s_scr)
        for p in range(npair):
            g = (2 * p) // hpg
            s_scr[:, p, g * MB_STATE:(g + 1) * MB_STATE, :] = jnp.concatenate(
                [s0_ref[:, 2 * p], s0_ref[:, 2 * p + 1]], axis=-1)
        xp_scr[:, 0:CONV_PAD, :] = c0_ref[...]

    parts = _row_parts(bt, c, 2 * FRONT_ROWS)
    bp = bt // parts
    u = u_ref[...].reshape(bt * c, D_MODEL)
    ltri = _blockdiag_ltri(bp * c, c)
    lane3 = lax.broadcasted_iota(jnp.int32, (bp, c, LANES), 2)
    lane_bcast = lambda col: jnp.broadcast_to(col, (bp * c, LANES)).reshape(bp, c, LANES)
    dt_parts, bm_parts, cm_parts = {}, {}, {}

    def project(p):
        sq = slice(p * bp, (p + 1) * bp)
        up = u[p * bp * c:(p + 1) * bp * c]
        z_scr[sq] = jnp.dot(up, wz_ref[...], preferred_element_type=f32).reshape(bp, c, BRANCH_W)
        xp_scr[sq, CONV_PAD:CONV_PAD + c, :] = jnp.dot(up, wx_ref[...], preferred_element_type=f32).reshape(
            bp, c, MB_CONV_DIM)
        dt_parts[p] = jnp.dot(up, wdt_ref[...], preferred_element_type=f32)

    def front(p):
        sq = slice(p * bp, (p + 1) * bp)
        dt = jax.nn.softplus(dt_parts.pop(p) + dtb_ref[...])
        cum = _dot_sel_left(ltri, -jnp.exp(alog_ref[...]) * dt)
        xa = _silu(_causal_conv(xp_scr.at[sq], cw_ref, c) + cb_ref[...])
        tail = xp_scr[sq, c:c + CONV_PAD, :]
        xp_scr[sq, 0:CONV_PAD, :] = tail
        co_ref[sq] = tail
        xs = xa[:, :, :BRANCH_W]
        bm_parts[p] = xa[:, :, BRANCH_W:BRANCH_W + LANES]
        cm = xa[:, :, BRANCH_W + LANES:BRANCH_W + 2 * LANES]
        cm_parts[p] = [jnp.where((lane3 >> _log2(MB_STATE)) == g, cm, 0.0) for g in range(MB_GROUPS)]
        xs_scr[sq] = xs
        for h in range(nh):
            own_half = (lane3 >> _log2(MB_HEADDIM)) == (h % 2)
            v_scr[sq, h] = jnp.where(
                own_half, xs[:, :, (h // 2) * LANES:(h // 2 + 1) * LANES] * lane_bcast(dt[:, h:h + 1]), 0.0)
            cum_scr[sq, h] = lane_bcast(cum[:, h:h + 1])

    _pipeline_parts(parts, project, front)
    vf = v_scr[...].reshape(n, LANES)
    cbf = cum_scr[...].reshape(n, LANES)
    blocks_of = lambda first, count: [divmod(first + i, nh) for i in range(count)]
    c_rows = lambda blocks: jnp.concatenate([cm_parts[b // bp][h // hpg][b % bp] for b, h in blocks], axis=0)
    b_rows = lambda blocks: jnp.concatenate([bm_parts[b // bp][b % bp] for b, _ in blocks], axis=0)

    ii, jj = _iota2(r, r, 0), _iota2(r, r, 1)
    incl = ((ii >> _log2(c)) == (jj >> _log2(c))) & (ii >= jj)
    grp = lambda a, g: a[g * r:(g + 1) * r]
    gblocks = [blocks_of(g * (r // c), r // c) for g in range(ng)]
    dec = lambda g: jnp.exp(jnp.where(incl, _widen(grp(cbf, g), r) - grp(cbf, g).T[0:1, :], -jnp.inf))
    o_loc = [_dot(_dot_nt(c_rows(gblocks[g]), b_rows(gblocks[g])) * dec(g), grp(vf, g)) for g in range(ng)]
    o_loc = o_loc[0] if ng == 1 else jnp.concatenate(o_loc, axis=0)

    cb3 = cbf.reshape(n // c, c, LANES)
    k_out = (vf.reshape(n // c, c, LANES) * jnp.exp(cb3[:, c - 1:c, :] - cb3)).reshape(n, LANES)
    e_in = jnp.exp(cbf)
    lo_half = _iota2(1, LANES, 1) < MB_HEADDIM
    row_lo = _iota2(2 * c, LANES, 0) < c
    own = row_lo == (_iota2(2 * c, LANES, 1) < MB_HEADDIM)
    pairs = [(b, p) for b in range(bt) for p in range(npair)]
    prow = lambda a, i: a[i * 2 * c:(i + 1) * 2 * c]
    s_old = [s_scr[b, p] for b, p in pairs]
    pblocks = [blocks_of(i * 2, 2) for i in range(len(pairs))]
    o_st = [jnp.where(own, _dot(c_rows(pblocks[i]), s_old[i]) * prow(e_in, i), 0.0) for i in range(len(pairs))]
    for i, (b, p) in enumerate(pairs):
        cum_i = prow(cbf, i)
        e_all = jnp.exp(jnp.where(lo_half, cum_i[c - 1:c], cum_i[2 * c - 1:2 * c]))
        s_scr[b, p] = s_old[i] * e_all + _dot_tn(b_rows(pblocks[i]), prow(k_out, i))
    for b in range(bt):
        ys = []
        for p in range(npair):
            o2 = prow(o_loc, b * npair + p) + o_st[b * npair + p]
            sl = slice(p * LANES, (p + 1) * LANES)
            ys.append(o2[:c] + o2[c:] + d_ref[:, sl] * xs_scr[b, :, sl])
        y = jnp.concatenate(ys, axis=-1) * _silu(z_scr[b])
        y_ref[b] = (_rms(y) * nrm_ref[...]).astype(bf16)

    @pl.when(j == pl.num_programs(1) - 1)
    def _():
        for p in range(npair):
            g = (2 * p) // hpg
            so_ref[:, 2 * p] = s_scr[:, p, g * MB_STATE:(g + 1) * MB_STATE, 0:MB_HEADDIM]
            so_ref[:, 2 * p + 1] = s_scr[:, p, g * MB_STATE:(g + 1) * MB_STATE, MB_HEADDIM:2 * MB_HEADDIM]


def _mamba(u, w, s0, conv0, c, bt, slot):
    b, l, _ = u.shape
    npair = MB_HEADS // 2
    consts = [w["wz"], w["wx"], w["wdt"], w["cw"], w["cb"], w["dtb"], w["alog"], w["d"], w["nrm"]]
    scratch = [
        pltpu.VMEM((bt, npair, LANES, LANES), f32),
        pltpu.VMEM((bt, CONV_PAD + c, MB_CONV_DIM), f32),
        pltpu.VMEM((bt, c, BRANCH_W), f32),
        pltpu.VMEM((bt, c, BRANCH_W), f32),
    ] + [pltpu.VMEM((bt, MB_HEADS, c, LANES), f32)] * 2
    return _mixer_call(_mamba_body, "mamba", u, consts, [s0, conv0], c, bt,
                       [(b, CONV_PAD, MB_CONV_DIM)], scratch, slot)


def _ret_body(u_ref, wq_ref, wk_ref, wv_ref, wg_ref, lg_ref, cos_ref, sin_ref,
              s0_ref, y_ref, so_ref, s_scr, q_scr, k_scr, v_scr, gt_scr):
    bt, c, _ = y_ref.shape
    j = pl.program_id(1)
    kw = RET_HEADS * RET_DK
    hw = RET_DK // 2

    @pl.when(j == 0)
    def _():
        s_scr[...] = jnp.zeros_like(s_scr)
        for h in range(RET_HEADS):
            s_scr[:, h, h * hw:(h + 1) * hw, :] = s0_ref[:, h, 0:hw, :]
            s_scr[:, h, LANES + h * hw:LANES + (h + 1) * hw, :] = s0_ref[:, h, hw:2 * hw, :]

    u = u_ref[...].reshape(bt * c, D_MODEL)
    cos = cos_ref[...]
    sin = sin_ref[...]

    def rot(x):
        x = x.reshape(-1, c, kw)
        x1, x2 = x[:, :, :LANES], x[:, :, LANES:]
        return jnp.concatenate([x1 * cos - x2 * sin, x1 * sin + x2 * cos], axis=-1)

    nh = RET_HEADS
    n = bt * nh * c
    r = min(MXU_DIM, n)
    assert n % r == 0 and r % c == 0
    ng = n // r
    parts = _row_parts(bt, c, 2 * FRONT_ROWS)
    bp = bt // parts
    klane = lax.broadcasted_iota(jnp.int32, (bp, c, kw), 2)
    raw = {}

    def project(p):
        sq = slice(p * bp, (p + 1) * bp)
        up = u[p * bp * c:(p + 1) * bp * c]
        raw[p] = (jnp.dot(up, wq_ref[...], preferred_element_type=f32),
                  jnp.dot(up, wk_ref[...], preferred_element_type=f32),
                  jnp.dot(up, wv_ref[...], preferred_element_type=f32))
        gt_scr[sq] = jnp.dot(up, wg_ref[...], preferred_element_type=f32).reshape(bp, c, BRANCH_W)

    def front(p):
        sq = slice(p * bp, (p + 1) * bp)
        q_raw, k_raw, v_raw = raw.pop(p)
        q = rot(q_raw)
        k = rot(k_raw) * (RET_DK ** -0.5)
        v = v_raw.reshape(bp, c, BRANCH_W)
        for h in range(nh):
            own = ((klane & (LANES - 1)) >> _log2(hw)) == h
            q_scr[sq, h] = jnp.where(own, q, 0.0)
            k_scr[sq, h] = jnp.where(own, k, 0.0)
            v_scr[sq, h] = v[:, :, h * RET_DV:(h + 1) * RET_DV]

    _pipeline_parts(parts, project, front)
    qf = q_scr[...].reshape(n, kw)
    kf = k_scr[...].reshape(n, kw)
    vf = v_scr[...].reshape(n, RET_DV)

    row = _iota2(n, 1, 0)
    head = (row >> _log2(c)) & (nh - 1)
    pos = (row & (c - 1)).astype(f32)
    lg_row = jnp.zeros((n, 1), f32)
    for h in range(nh):
        lg_row = jnp.where(head == h, lg_ref[:, h:h + 1], lg_row)
    q_in = qf * jnp.exp((pos + 1.0) * lg_row)
    k_out = kf * jnp.exp((float(c) - 1.0 - pos) * lg_row)
    e_all = jnp.exp(float(c) * lg_row)

    ii, jj = _iota2(r, r, 0), _iota2(r, r, 1)
    incl = ((ii >> _log2(c)) == (jj >> _log2(c))) & (ii >= jj)
    dist = (ii - jj).astype(f32)
    grp = lambda a, g: a[g * r:(g + 1) * r]
    dec = {}
    for g in range(ng):
        key = (g * r) % (nh * c)
        if key not in dec:
            dec[key] = jnp.exp(jnp.where(incl, dist * grp(lg_row, g), -jnp.inf))
    o_loc = [_dot(_dot_nt(grp(qf, g), grp(kf, g)) * dec[(g * r) % (nh * c)], grp(vf, g)) for g in range(ng)]
    o_loc = o_loc[0] if ng == 1 else jnp.concatenate(o_loc, axis=0)

    blocks = [(b, h) for b in range(bt) for h in range(nh)]
    rows = lambda a, i: a[i * c:(i + 1) * c]
    s_old = [s_scr[b, h] for b, h in blocks]
    o_st = [_dot(rows(q_in, i), s_old[i]) for i in range(len(blocks))]
    for i, (b, h) in enumerate(blocks):
        s_scr[b, h] = s_old[i] * rows(e_all, i)[0:1] + _dot_tn(rows(k_out, i), rows(vf, i))
    for b in range(bt):
        y = jnp.concatenate([_rms(rows(o_loc, b * nh + h) + o_st[b * nh + h]) for h in range(nh)], axis=-1)
        y_ref[b] = (y * _silu(gt_scr[b])).astype(bf16)

    @pl.when(j == pl.num_programs(1) - 1)
    def _():
        for h in range(RET_HEADS):
            so_ref[:, h, 0:hw, :] = s_scr[:, h, h * hw:(h + 1) * hw, :]
            so_ref[:, h, hw:2 * hw, :] = s_scr[:, h, LANES + h * hw:LANES + (h + 1) * hw, :]


def _ret(u, w, cos, sin, lg, s0, c, bt, slot):
    kw = RET_HEADS * RET_DK
    consts = [w["wq"], w["wk"], w["wv"], w["wg"], lg]
    scratch = [
        pltpu.VMEM((bt, RET_HEADS, kw, RET_DV), f32),
        pltpu.VMEM((bt, RET_HEADS, c, kw), f32),
        pltpu.VMEM((bt, RET_HEADS, c, kw), f32),
        pltpu.VMEM((bt, RET_HEADS, c, RET_DV), f32),
        pltpu.VMEM((bt, c, BRANCH_W), f32),
    ]
    return _mixer_call(_ret_body, "ret", u, consts, [s0], c, bt, [], scratch, slot, chunk_ins=(cos, sin))


def _hgrn_body(u_ref, wq_ref, wf_ref, wi_ref, wg_ref, lb_ref, nrm_ref,
               s0_ref, y_ref, so_ref, st_scr, q_scr, k_scr, v_scr, cum_scr, gt_scr):
    bt, c, _ = y_ref.shape
    j = pl.program_id(1)
    nh = HG_HEADS
    n, r, ng = _stack_geometry(bt, c, nh)
    sub = HG_SUB

    @pl.when(j == 0)
    def _():
        st_scr[...] = s0_ref[...]

    parts = _row_parts(bt, c)
    bp = bt // parts
    u = u_ref[...].reshape(bt * c, D_MODEL)
    ltri = _blockdiag_ltri(bp * c, c)
    lb = lb_ref[...]
    raw = {}

    def project(p):
        sq = slice(p * bp, (p + 1) * bp)
        up = u[p * bp * c:(p + 1) * bp * c]
        raw[p] = (jnp.dot(up, wq_ref[...], preferred_element_type=f32),
                  jnp.dot(up, wf_ref[...], preferred_element_type=f32),
                  jnp.dot(up, wi_ref[...], preferred_element_type=f32))
        gt_scr[sq] = jnp.dot(up, wg_ref[...], preferred_element_type=f32).reshape(bp, c, BRANCH_W)

    def front(p):
        sq = slice(p * bp, (p + 1) * bp)
        q_raw, f_raw, v = raw.pop(p)
        q = _silu(q_raw)
        fg = lb + (1.0 - lb) * jax.nn.sigmoid(f_raw)
        cum = _dot_sel_left(ltri, jnp.log(fg))
        k = 1.0 - fg
        for h in range(nh):
            sl = slice(h * HG_DK, (h + 1) * HG_DK)
            q_scr[sq, h] = q[:, sl].reshape(bp, c, HG_DK)
            k_scr[sq, h] = k[:, sl].reshape(bp, c, HG_DK)
            v_scr[sq, h] = v[:, sl].reshape(bp, c, HG_DV)
            cum_scr[sq, h] = cum[:, sl].reshape(bp, c, HG_DK)

    ii, jj = _iota2(r, r, 0), _iota2(r, r, 1)
    n_p = n // parts
    g_p = n_p // r
    assert n_p % r == 0
    ones = jnp.ones((HG_DK, r), bf16)
    same_sub = (ii >> _log2(sub)) == (jj >> _log2(sub))
    part_scores = {}

    def local(p):
        sq = slice(p * bp, (p + 1) * bp)
        qf, kf, cf = (ref[sq].reshape(n_p, HG_DK) for ref in (q_scr, k_scr, cum_scr))
        scores = [jnp.zeros((r, r), f32) for _ in range(g_p)]
        q3, k3, c3 = (a.reshape(n_p // sub, sub, HG_DK) for a in (qf, kf, cf))
        sub_i = lax.broadcasted_iota(jnp.int32, q3.shape, 1)
        for t in range(sub):
            pt = jnp.where(sub_i >= t, q3 * k3[:, t:t + 1, :] * jnp.exp(c3 - c3[:, t:t + 1, :]), 0.0)
            row_sum = jnp.dot(pt.reshape(n_p, HG_DK).astype(bf16), ones, preferred_element_type=f32)
            place = same_sub & ((jj & (sub - 1)) == t)
            scores = [jnp.where(place, row_sum[g * r:(g + 1) * r], scores[g]) for g in range(g_p)]
        m = sub
        while m < c:
            q4, k4, c4 = (a.reshape(n_p // (2 * m), 2, m, HG_DK) for a in (qf, kf, cf))
            bnd = c4[:, 0:1, m - 1:m, :]
            odd = lax.broadcasted_iota(jnp.int32, q4.shape, 1) == 1
            e = jnp.exp(jnp.where(odd, c4 - bnd, bnd - c4))
            q_t = jnp.where(odd, q4 * e, 0.0).reshape(n_p, HG_DK)
            k_t = jnp.where(odd, 0.0, k4 * e).reshape(n_p, HG_DK)
            same_pair = (ii >> _log2(2 * m)) == (jj >> _log2(2 * m))
            scores = [scores[g] + jnp.where(same_pair, _dot_nt(q_t[g * r:(g + 1) * r], k_t[g * r:(g + 1) * r]), 0.0)
                      for g in range(g_p)]
            m *= 2
        part_scores[p] = scores

    _pipeline_parts(parts, project, front, local)
    scores = [s for p in range(parts) for s in part_scores[p]]
    qf = q_scr[...].reshape(n, HG_DK)
    kf = k_scr[...].reshape(n, HG_DK)
    vf = v_scr[...].reshape(n, HG_DV)
    cf = cum_scr[...].reshape(n, HG_DK)

    cb3 = cf.reshape(n // c, c, HG_DK)
    clast3 = cb3[:, c - 1:c, :]
    q_in = qf * jnp.exp(cf)
    k_out = (kf.reshape(n // c, c, HG_DK) * jnp.exp(clast3 - cb3)).reshape(n, HG_DK)
    nblk = n // c
    e_rows = jnp.exp(cb3[:, c - 1, :])
    if nblk < HG_DK:
        e_rows = jnp.concatenate([e_rows, jnp.zeros((HG_DK - nblk, HG_DK), f32)], axis=0)
    e_cols = e_rows.T
    blocks = [(b, h) for b in range(bt) for h in range(nh)]
    rows = lambda a, i: a[i * c:(i + 1) * c]
    st_old = [st_scr[b, h] for b, h in blocks]
    o_state = [_dot(rows(q_in, i), st_old[i]) for i in range(len(blocks))]
    for i, (b, h) in enumerate(blocks):
        st_scr[b, h] = st_old[i] * e_cols[:, i:i + 1] + _dot_tn(rows(k_out, i), rows(vf, i))
    o = [_dot(scores[g], vf[g * r:(g + 1) * r]) for g in range(ng)]
    o = o[0] if ng == 1 else jnp.concatenate(o, axis=0)
    for b in range(bt):
        y = jnp.concatenate([_rms(rows(o, b * nh + h) + o_state[b * nh + h]) * nrm_ref[...] for h in range(nh)],
                            axis=-1)
        y_ref[b] = (y * _silu(gt_scr[b])).astype(bf16)

    @pl.when(j == pl.num_programs(1) - 1)
    def _():
        so_ref[...] = st_scr[...]


def _hgrn(u, w, s0, c, bt, slot):
    consts = [w["wq"], w["wf"], w["wi"], w["wg"], w["lb"], w["nrm"]]
    scratch = ([pltpu.VMEM((bt, HG_HEADS, HG_DK, HG_DV), f32)] + [pltpu.VMEM((bt, HG_HEADS, c, HG_DK), f32)] * 4
               + [pltpu.VMEM((bt, c, BRANCH_W), f32)])
    return _mixer_call(_hgrn_body, "hgrn", u, consts, [s0], c, bt, [], scratch, slot)


def _unit_lower_inverses(a_list, r, c):
    ii, jj = _iota2(r, r, 0), _iota2(r, r, 1)
    eye = jnp.where(ii == jj, 1.0, 0.0)
    t_list = [eye for _ in a_list]
    m = 1
    while m < c:
        lm = _log2(m)
        join = ((ii >> (lm + 1)) == (jj >> (lm + 1))) & (((ii >> lm) & 1) == 1) & (((jj >> lm) & 1) == 0)
        if m == 1:
            t_list = [eye - jnp.where(join, a, 0.0) for a in a_list]
        else:
            ft_list = [_dot(jnp.where(join, a, 0.0), t) for a, t in zip(a_list, t_list)]
            t_list = [t - _dot(t, ft) for t, ft in zip(t_list, ft_list)]
        m *= 2
    return t_list


def _gdn_body(u_ref, wqkv_ref, wz_ref, wab_ref, cw_ref, dtb_ref, alog_ref, nrm_ref,
              s0_ref, c0_ref, y_ref, so_ref, co_ref, s_scr, xp_scr, z_scr, q_scr, k_scr, v_scr, cum_scr, beta_scr):
    bt, c, _ = y_ref.shape
    j = pl.program_id(1)
    nh = GDN_HEADS
    hk = nh * GDN_DK
    n, r, ng = _stack_geometry(bt, c, nh)

    @pl.when(j == 0)
    def _():
        s_scr[...] = s0_ref[...]
        xp_scr[:, 0:CONV_PAD, :] = c0_ref[...]

    parts = _row_parts(bt, c)
    bp = bt // parts
    ltri = _blockdiag_ltri(bp * c, c)
    u = u_ref[...].reshape(bt * c, D_MODEL)
    lane_bcast = lambda col: jnp.broadcast_to(col, (bp * c, LANES)).reshape(bp, c, LANES)
    ii, jj = _iota2(r, r, 0), _iota2(r, r, 1)
    same = (ii >> _log2(c)) == (jj >> _log2(c))
    incl = same & (ii >= jj)
    strict = same & (ii > jj)
    rows = lambda a, i: a[i * c:(i + 1) * c]
    nb = r // c
    ab_parts = {}

    def project(p):
        sq = slice(p * bp, (p + 1) * bp)
        up = u[p * bp * c:(p + 1) * bp * c]
        xp_scr[sq, CONV_PAD:CONV_PAD + c, :] = jnp.dot(up, wqkv_ref[...], preferred_element_type=f32).reshape(
            bp, c, GDN_CONV_DIM)
        z_scr[sq] = jnp.dot(up, wz_ref[...], preferred_element_type=f32).reshape(bp, c, BRANCH_W)
        ab_parts[p] = jnp.dot(up, wab_ref[...], preferred_element_type=f32)

    def front(p):
        sq = slice(p * bp, (p + 1) * bp)
        ab = ab_parts.pop(p)
        log_g = -jnp.exp(alog_ref[...]) * jax.nn.softplus(ab + dtb_ref[...])
        cum = _dot_sel_left(ltri, log_g)
        beta = jax.nn.sigmoid(ab)
        qkv = _silu(_causal_conv(xp_scr.at[sq], cw_ref, c))
        tail = xp_scr[sq, c:c + CONV_PAD, :]
        xp_scr[sq, 0:CONV_PAD, :] = tail
        co_ref[sq] = tail
        for h in range(nh):
            q = qkv[:, :, h * GDN_DK:(h + 1) * GDN_DK]
            k = qkv[:, :, hk + h * GDN_DK:hk + (h + 1) * GDN_DK]
            q_scr[sq, h] = q * lax.rsqrt(jnp.sum(q * q, axis=-1, keepdims=True) + EPS) * (GDN_DK ** -0.5)
            k_scr[sq, h] = k * lax.rsqrt(jnp.sum(k * k, axis=-1, keepdims=True) + EPS)
            v_scr[sq, h] = qkv[:, :, 2 * hk + h * GDN_DV:2 * hk + (h + 1) * GDN_DV]
            cum_scr[sq, h] = lane_bcast(cum[:, h:h + 1])
            beta_scr[sq, h] = lane_bcast(beta[:, nh + h:nh + h + 1])

    nbatch = 1
    gpp = ng // nbatch
    loc = range(gpp)
    held = {}

    def build(p):
        groups = range(p * gpp, (p + 1) * gpp)
        grp = lambda ref: [_group_rows(ref, g, r) for g in groups]
        qs, ks, vs, cb, bb = grp(q_scr), grp(k_scr), grp(v_scr), grp(cum_scr), grp(beta_scr)
        dec, a_mat = [], []
        for g in loc:
            dec.append(jnp.exp(jnp.where(incl, _widen(cb[g], r) - cb[g].T[0:1, :], -jnp.inf)))
            a_mat.append(jnp.where(strict, _widen(bb[g], r) * dec[g] * _dot_nt(ks[g], ks[g]), 0.0))
        held[p] = (qs, ks, vs, cb, bb, dec, a_mat)

    def invert(p):
        held[p] = held[p][:6] + (_unit_lower_inverses(held[p][6], r, c),)

    def finish(p):
        qs, ks, vs, cb, bb, dec, t_inv = held.pop(p)
        e_in, sol, scd, q_in = [], [], [], []
        for g in loc:
            e_in.append(jnp.exp(cb[g]))
            sol.append(_dot(t_inv[g], jnp.concatenate([vs[g] * bb[g], ks[g] * (e_in[g] * bb[g])], axis=-1)))
            scd.append(_dot_nt(qs[g], ks[g]) * dec[g])
            q_in.append(qs[g] * e_in[g])
        gb = [(g, i) for g in loc for i in range(nb)]
        blk = lambda g, i: divmod((p * gpp + g) * nb + i, nh)
        s_old = {k: s_scr[blk(*k)] for k in gb}
        proj = {(g, i): _dot(jnp.concatenate([rows(sol[g], i)[:, GDN_DV:], rows(q_in[g], i)], axis=0), s_old[g, i])
                for g, i in gb}
        w_new = {(g, i): rows(sol[g], i)[:, :GDN_DV] - proj[g, i][:c] for g, i in gb}
        for g, i in gb:
            cum_i = rows(cb[g], i)
            clast = cum_i[c - 1:c]
            s_scr[blk(g, i)] = (s_old[g, i] * jnp.exp(clast)
                                + _dot_tn(rows(ks[g], i) * jnp.exp(clast - cum_i), w_new[g, i]))
        o_all = [_dot(scd[g], jnp.concatenate([w_new[g, i] for i in range(nb)], axis=0))
                 + jnp.concatenate([proj[g, i][c:] for i in range(nb)], axis=0) for g in loc]
        o = o_all[0] if gpp == 1 else jnp.concatenate(o_all, axis=0)
        bpb = bt // nbatch
        for s in range(bpb):
            b = p * bpb + s
            y = jnp.concatenate([_rms(rows(o, s * nh + h)) * nrm_ref[...] for h in range(nh)], axis=-1)
            y_ref[b] = (y * _silu(z_scr[b])).astype(bf16)

    _pipeline_parts(parts, project, front)
    _pipeline_parts(nbatch, build, invert, finish)

    @pl.when(j == pl.num_programs(1) - 1)
    def _():
        so_ref[...] = s_scr[...]


def _gdn(u, w, s0, conv0, c, bt, slot):
    b, l, _ = u.shape
    consts = [w["wqkv"], w["wz"], w["wab"], w["cw"], w["dtb"], w["alog"], w["nrm"]]
    scratch = [
        pltpu.VMEM((bt, GDN_HEADS, GDN_DK, GDN_DV), f32),
        pltpu.VMEM((bt, CONV_PAD + c, GDN_CONV_DIM), f32),
        pltpu.VMEM((bt, c, BRANCH_W), f32),
    ] + [pltpu.VMEM((bt, GDN_HEADS, c, LANES), f32)] * 5
    return _mixer_call(_gdn_body, "gdn", u, consts, [s0, conv0], c, bt,
                       [(b, CONV_PAD, GDN_CONV_DIM)], scratch, slot)


def _pad_lanes(v, offset=0):
    row = jnp.zeros((1, LANES), f32)
    return row.at[0, offset:offset + v.shape[0]].set(v.astype(f32))


def _layer_weights(l, norm_g, ffn_w_in, ffn_w_out, w_in, w_branch, w_out, mb_conv_w, mb_conv_b, mb_a_log,
                   mb_dt_bias, mb_d, mb_norm, lb_all, hg_norm, gdn_conv_w, gdn_a_log, gdn_dt_bias, gdn_norm):
    wi = w_in[l]
    col = lambda n: wi[:, IN_OFFSETS[n]:IN_OFFSETS[n + 1]]
    colb = lambda n: col(n).astype(bf16)
    pad_cols = lambda a: jnp.pad(a, ((0, 0), (0, LANES - a.shape[1]))).astype(bf16)
    hw = RET_DK // 2
    perm = np.arange(RET_HEADS * RET_DK).reshape(RET_HEADS, 2, hw).transpose(1, 0, 2).reshape(-1)
    row = lambda v: v.astype(f32).reshape(1, -1)
    return {
        "g_ffn1": norm_g[l, 0:3], "g_post": norm_g[l, 3:4], "g_ffn2": norm_g[l, 4:6],
        "layer": l, "ffn_in": ffn_w_in, "ffn_out": ffn_w_out,
        "mamba": {"wz": colb(0), "wx": colb(1), "wdt": pad_cols(col(2)), "cw": mb_conv_w[l],
                  "cb": row(mb_conv_b[l]),
                  "dtb": _pad_lanes(mb_dt_bias[l]), "alog": _pad_lanes(mb_a_log[l]),
                  "d": row(jnp.repeat(mb_d[l], MB_HEADDIM)), "nrm": row(mb_norm[l])},
        "hgrn": {"wq": colb(3), "wf": colb(4), "wi": colb(5), "wg": colb(6), "lb": row(lb_all[l]),
                 "nrm": row(hg_norm[l])},
        "ret": {"wq": col(7)[:, perm].astype(bf16), "wk": col(8)[:, perm].astype(bf16),
                "wv": colb(9), "wg": colb(10)},
        "gdn": {"wqkv": colb(11), "wz": colb(12),
                "wab": pad_cols(jnp.concatenate([col(13), col(14)], axis=1)),
                "cw": gdn_conv_w[l], "dtb": _pad_lanes(gdn_dt_bias[l]), "alog": _pad_lanes(gdn_a_log[l]),
                "nrm": row(gdn_norm[l])},
        "w_gate": colb(15), "w_branch": w_branch[l].astype(bf16), "w_out": w_out[l].astype(bf16),
    }


def _rope_tables(pos):
    half = RET_DK // 2
    inv_freq = 1.0 / (ROPE_BASE ** jnp.linspace(0.0, 1.0, half, dtype=f32))
    ang = pos.astype(f32)[:, None] * inv_freq[None, :]
    reps = LANES // half
    return jnp.tile(jnp.cos(ang), (1, reps)), jnp.tile(jnp.sin(ang), (1, reps))


def _pad_conv(buf):
    return jnp.pad(buf, ((0, 0), (CONV_PAD - (CONV_W - 1), 0), (0, 0)))


def _layer(h, w, states, rope, lg, chunks, bt, depth, stacked):
    b, l, d = h.shape
    s_ssm, s_ssm_conv, s_hg, s_ret, s_gdn, s_gdn_conv = states
    slot = lambda n: (w["layer"], depth, None if stacked is None else stacked[n])
    h, u2 = _ffn(h.reshape(b * l, d), w["g_ffn1"], w["ffn_in"], w["ffn_out"], w["layer"], 0, emit_normed=True)
    u = u2.reshape(b, l, d)
    y_mb, n_ssm, n_ssm_conv = _mamba(u, w["mamba"], s_ssm, _pad_conv(s_ssm_conv), chunks["mamba"], bt, slot(0))
    y_hg, n_hg = _hgrn(u, w["hgrn"], s_hg, chunks["hgrn"], bt, slot(1))
    y_rt, n_ret = _ret(u, w["ret"], rope[0], rope[1], lg, s_ret, chunks["ret"], bt, slot(2))
    y_gd, n_gdn, n_gdn_conv = _gdn(u, w["gdn"], s_gdn, _pad_conv(s_gdn_conv), chunks["gdn"], bt, slot(3))
    ys = [y.reshape(b * l, BRANCH_W) for y in (y_mb, y_hg, y_rt, y_gd)]
    hf = _merge(h, u2, w["g_post"], ys, w["w_gate"], w["w_branch"], w["w_out"])
    hf = _ffn(hf, w["g_ffn2"], w["ffn_in"], w["ffn_out"], w["layer"], 1)
    convs = (n_ssm_conv[:, CONV_PAD - (CONV_W - 1):], n_gdn_conv[:, CONV_PAD - (CONV_W - 1):])
    return hf.reshape(b, l, d), (n_ssm, n_hg, n_ret, n_gdn), convs


def _chunks(l):
    pick = lambda pref: pref if l % pref == 0 else math.gcd(l, pref)
    return {"mamba": pick(128), "ret": pick(128), "hgrn": pick(64), "gdn": pick(64)}


def kernel(x_prompt, x_sample, state_ssm, state_ssm_conv, state_hgrn, state_ret, state_gdn, state_gdn_conv,
           norm_g, ffn_w_in, ffn_w_out, w_in, w_branch, w_out, mb_conv_w, mb_conv_b, mb_a_log, mb_dt_bias, mb_d,
           mb_norm, hg_lb_logits, hg_norm, gdn_conv_w, gdn_a_log, gdn_dt_bias, gdn_norm):
    depth = norm_g.shape[0]
    lb_all = jnp.cumsum(jax.nn.softmax(hg_lb_logits.astype(f32), axis=0), axis=0)
    lb_all = lb_all - lb_all[0:1]
    lg = _pad_lanes(jnp.log(1.0 - jnp.exp2(-5.0 - jnp.arange(RET_HEADS, dtype=f32))))

    bp, lp, _ = x_prompt.shape
    bs, ls, _ = x_sample.shape
    rope_p = _rope_tables(jnp.arange(lp))
    rope_s = _rope_tables(PAST_LEN + jnp.arange(ls))
    zero_states = (
        (jnp.zeros((1, bp, MB_HEADS, MB_STATE, MB_HEADDIM), f32), 0),
        jnp.zeros((bp, CONV_W - 1, MB_CONV_DIM), f32),
        (jnp.zeros((1, bp, HG_HEADS, HG_DK, HG_DV), f32), 0),
        (jnp.zeros((1, bp, RET_HEADS, RET_DK, RET_DV), f32), 0),
        (jnp.zeros((1, bp, GDN_HEADS, GDN_DK, GDN_DV), f32), 0),
        jnp.zeros((bp, CONV_W - 1, GDN_CONV_DIM), f32),
    )
    bt_p = math.gcd(bp, 8)
    bt_s = math.gcd(bs, 16)

    ffn_w_in = ffn_w_in.astype(bf16)
    ffn_w_out = ffn_w_out.astype(bf16)
    hp, hs = x_prompt, x_sample
    st_p = st_s = None
    cv_p, cv_s = [], []
    for l in range(depth):
        w = _layer_weights(l, norm_g, ffn_w_in, ffn_w_out, w_in, w_branch, w_out, mb_conv_w, mb_conv_b,
                           mb_a_log, mb_dt_bias, mb_d, mb_norm, lb_all, hg_norm, gdn_conv_w, gdn_a_log,
                           gdn_dt_bias, gdn_norm)
        hp, st_p, cp = _layer(hp, w, zero_states, rope_p, lg, _chunks(lp), bt_p, depth, st_p)
        past = ((state_ssm, l), state_ssm_conv[l], (state_hgrn, l), (state_ret, l), (state_gdn, l), state_gdn_conv[l])
        hs, st_s, cs = _layer(hs, w, past, rope_s, lg, _chunks(ls), bt_s, depth, st_s)
        cv_p.append(cp)
        cv_s.append(cs)

    stack = lambda cvs, i: jnp.stack([cv[i] for cv in cvs], axis=0)
    return (hp, hs,
            st_p[0], stack(cv_p, 0), st_p[1], st_p[2], st_p[3], stack(cv_p, 1),
            st_s[0], stack(cv_s, 0), st_s[1], st_s[2], st_s[3], stack(cv_s, 1))
```

```python
import math

import numpy as np
import jax
import jax.numpy as jnp
from jax import lax
from jax.experimental import pallas as pl
from jax.experimental.pallas import tpu as pltpu

f32 = jnp.float32
bf16 = jnp.bfloat16

D_MODEL = 1024
D_FF = 2816
BRANCH_W = 512
N_BRANCH = 4
CONV_W = 4
EPS = 1e-6
ROPE_BASE = 10000.0
PAST_LEN = 16384

MB_HEADS, MB_HEADDIM, MB_STATE, MB_GROUPS = 8, 64, 64, 2
MB_CONV_DIM = BRANCH_W + 2 * MB_GROUPS * MB_STATE
HG_HEADS, HG_DK, HG_DV = 4, 128, 128
RET_HEADS, RET_DK, RET_DV = 4, 64, 128
GDN_HEADS, GDN_DK, GDN_DV = 4, 128, 128
GDN_CONV_DIM = 2 * GDN_HEADS * GDN_DK + GDN_HEADS * GDN_DV

IN_SPLITS = (
    BRANCH_W, MB_CONV_DIM, MB_HEADS,
    HG_HEADS * HG_DK, HG_HEADS * HG_DK, BRANCH_W, BRANCH_W,
    RET_HEADS * RET_DK, RET_HEADS * RET_DK, BRANCH_W, BRANCH_W,
    GDN_CONV_DIM, BRANCH_W, GDN_HEADS, GDN_HEADS,
    N_BRANCH * D_MODEL,
)
IN_OFFSETS = tuple(int(v) for v in np.cumsum((0,) + IN_SPLITS))

LANES = 128
SUBLANES = 8
MXU_DIM = 256
STACK_ROWS = 128
CONV_PAD = SUBLANES
VMEM_LIMIT = 56 * 1024 * 1024
FFN_TM = 512
MERGE_TM = 512
HG_SUB = SUBLANES
FRONT_ROWS = 128
FFN_PART_ROWS = 128


def _dot(a, b):
    return jnp.dot(a.astype(bf16), b.astype(bf16), preferred_element_type=f32)


def _dot_nt(a, b):
    return lax.dot_general(a.astype(bf16), b.astype(bf16), (((1,), (1,)), ((), ())), preferred_element_type=f32)


def _dot_tn(a, b):
    return lax.dot_general(a.astype(bf16), b.astype(bf16), (((0,), (0,)), ((), ())), preferred_element_type=f32)


def _split3(x):
    hi = x.astype(bf16)
    r = x - hi.astype(f32)
    mid = r.astype(bf16)
    lo = (r - mid.astype(f32)).astype(bf16)
    return hi, mid, lo


def _dot_sel_left(m01, x):
    hi, mid, lo = _split3(x)
    return (jnp.dot(m01, hi, preferred_element_type=f32) + jnp.dot(m01, mid, preferred_element_type=f32)
            + jnp.dot(m01, lo, preferred_element_type=f32))


def _rms(x):
    return x * lax.rsqrt(jnp.mean(x * x, axis=-1, keepdims=True) + EPS)


def _silu(x):
    return x * jax.nn.sigmoid(x)


def _iota2(n, m, axis):
    return lax.broadcasted_iota(jnp.int32, (n, m), axis)


def _log2(n):
    assert n > 0 and n & (n - 1) == 0, n
    return n.bit_length() - 1


def _blockdiag_ltri(n, c):
    ii, jj = _iota2(n, n, 0), _iota2(n, n, 1)
    lc = _log2(c)
    return jnp.where(((ii >> lc) == (jj >> lc)) & (ii >= jj), 1.0, 0.0).astype(bf16)


def _causal_conv(xp_ref, w_ref, c):
    y = xp_ref[:, pl.ds(CONV_PAD - 3, c), :] * w_ref[0:1, :]
    for j in range(1, CONV_W):
        y = y + xp_ref[:, pl.ds(CONV_PAD - 3 + j, c), :] * w_ref[j:j + 1, :]
    return y


def _stack_geometry(bt, c, heads):
    n = bt * heads * c
    r = min(STACK_ROWS, n)
    assert n % r == 0 and r % c == 0 and r % LANES == 0
    return n, r, n // r


def _row_parts(bt, c, rows=FRONT_ROWS):
    parts = max(1, (bt * c) // rows)
    assert bt % parts == 0
    return parts


def _pipeline_parts(parts, *stages):
    ns = len(stages)
    for t in range(parts + ns - 1):
        for s, stage in enumerate(stages):
            if 0 <= t - s < parts:
                stage(t - s)


def _group_rows(ref, g, r):
    _, heads, c, w = ref.shape
    bpg = r // c
    if bpg <= heads:
        assert heads % bpg == 0
        b, h0 = divmod(g * bpg, heads)
        return ref[b, h0:h0 + bpg].reshape(r, w)
    assert bpg % heads == 0
    nseq = bpg // heads
    return ref[g * nseq:(g + 1) * nseq].reshape(r, w)


def _widen(x, r):
    return x if r == LANES else jnp.concatenate([x] * (r // LANES), axis=-1)


def _ffn_body(x_ref, g_ref, wg_ref, wu_ref, wo_ref, o_ref, *un_ref):
    tm = x_ref.shape[0]
    parts = max(1, tm // FFN_PART_ROWS)
    rp = tm // parts
    xn, act = {}, {}

    def pre_norm(p):
        xn[p] = (_rms(x_ref[p * rp:(p + 1) * rp, :]) * g_ref[0:1, :]).astype(bf16)

    def expand(p):
        x_p = xn.pop(p)
        gate = jnp.dot(x_p, wg_ref[...], preferred_element_type=f32)
        up = jnp.dot(x_p, wu_ref[...], preferred_element_type=f32)
        act[p] = (_silu(gate) * up).astype(bf16)

    def contract(p):
        rows = slice(p * rp, (p + 1) * rp)
        y = jnp.dot(act.pop(p), wo_ref[...], preferred_element_type=f32)
        out = x_ref[rows, :] + 0.5 * (_rms(y) * g_ref[1:2, :])
        o_ref[rows, :] = out
        if un_ref:
            un_ref[0][rows, :] = (_rms(out) * g_ref[2:3, :]).astype(bf16)

    _pipeline_parts(parts, pre_norm, expand, contract)


def _ffn(x, g, w_in, w_out, l, k, emit_normed=False):
    t = x.shape[0]
    tm = min(FFN_TM, t)
    resident = pl.Buffered(1)
    row = pl.BlockSpec((tm, D_MODEL), lambda i: (i, 0))
    out_specs, out_shape = row, jax.ShapeDtypeStruct((t, D_MODEL), f32)
    if emit_normed:
        out_specs, out_shape = [row, row], [out_shape, jax.ShapeDtypeStruct((t, D_MODEL), bf16)]
    return pl.pallas_call(
        _ffn_body,
        grid=(t // tm,),
        in_specs=[
            row,
            pl.BlockSpec(g.shape, lambda i: (0, 0)),
            pl.BlockSpec((None, None, D_MODEL, D_FF), lambda i: (l, k, 0, 0), pipeline_mode=resident),
            pl.BlockSpec((None, None, D_MODEL, D_FF), lambda i: (l, k, 0, 1), pipeline_mode=resident),
            pl.BlockSpec((None, None, D_FF, D_MODEL), lambda i: (l, k, 0, 0), pipeline_mode=resident),
        ],
        out_specs=out_specs,
        out_shape=out_shape,
        compiler_params=pltpu.CompilerParams(dimension_semantics=("arbitrary",), vmem_limit_bytes=VMEM_LIMIT),
        name="ffn",
    )(x, g, w_in, w_in, w_out)


def _merge_body(h_ref, u_ref, g_ref, y0_ref, y1_ref, y2_ref, y3_ref, wg_ref, wb_ref, wo_ref, o_ref):
    tm = h_ref.shape[0]
    parts = max(1, tm // FFN_PART_ROWS)
    rp = tm // parts
    merged_parts = {}

    def gate_and_project(p):
        rows = slice(p * rp, (p + 1) * rp)
        u = u_ref[rows, :]
        merged = jnp.zeros((rp, D_MODEL), f32)
        for n, y_ref in enumerate((y0_ref, y1_ref, y2_ref, y3_ref)):
            gate = jax.nn.sigmoid(jnp.dot(u, wg_ref[:, n * D_MODEL:(n + 1) * D_MODEL], preferred_element_type=f32))
            merged = merged + gate * jnp.dot(y_ref[rows, :], wb_ref[n], preferred_element_type=f32)
        merged_parts[p] = merged.astype(bf16)

    def mix_out(p):
        rows = slice(p * rp, (p + 1) * rp)
        mixed = jnp.dot(merged_parts.pop(p), wo_ref[...], preferred_element_type=f32)
        o_ref[rows, :] = h_ref[rows, :] + _rms(mixed) * g_ref[...]

    _pipeline_parts(parts, gate_and_project, mix_out)


def _merge(h, u, g_post, ys, w_gate, w_branch, w_out):
    t = h.shape[0]
    tm = min(MERGE_TM, t)
    row = lambda i: (i, 0)
    resident = pl.Buffered(1)
    return pl.pallas_call(
        _merge_body,
        grid=(t // tm,),
        in_specs=[pl.BlockSpec((tm, D_MODEL), row), pl.BlockSpec((tm, D_MODEL), row),
                  pl.BlockSpec((1, D_MODEL), lambda i: (0, 0))]
                 + [pl.BlockSpec((tm, BRANCH_W), row)] * N_BRANCH
                 + [pl.BlockSpec((D_MODEL, N_BRANCH * D_MODEL), lambda i: (0, 0), pipeline_mode=resident),
                    pl.BlockSpec((N_BRANCH, BRANCH_W, D_MODEL), lambda i: (0, 0, 0), pipeline_mode=resident),
                    pl.BlockSpec((D_MODEL, D_MODEL), lambda i: (0, 0), pipeline_mode=resident)],
        out_specs=pl.BlockSpec((tm, D_MODEL), row),
        out_shape=jax.ShapeDtypeStruct((t, D_MODEL), f32),
        compiler_params=pltpu.CompilerParams(dimension_semantics=("arbitrary",), vmem_limit_bytes=VMEM_LIMIT),
        name="merge",
    )(h, u, g_post, *ys, w_gate, w_branch, w_out)


def _const_spec(a):
    nd = a.ndim
    return pl.BlockSpec(a.shape, lambda i, j, _nd=nd: (0,) * _nd)


def _batch_spec(bt, a):
    nd = a.ndim
    return pl.BlockSpec((bt,) + a.shape[1:], lambda i, j, _nd=nd: (i,) + (0,) * (_nd - 1))


def _mixer_call(body, name, u, consts, batch_ins, c, bt, batch_outs, scratch, slot, chunk_ins=()):
    b, l, d = u.shape
    layer, depth, prev = slot
    seq = lambda i, j: (i, j, 0)
    if c == l:
        u, u_spec = u.reshape(b * l, d), pl.BlockSpec((bt * c, d), lambda i, j: (i, 0))
    else:
        u_spec = pl.BlockSpec((bt, c, d), seq)
    st_in, st_entry = batch_ins[0]
    st = tuple(st_in.shape[1:])
    st_block = (None, bt) + st[1:]
    out_shape = ([jax.ShapeDtypeStruct((b, l, BRANCH_W), bf16), jax.ShapeDtypeStruct((depth,) + st, f32)]
                 + [jax.ShapeDtypeStruct(s, f32) for s in batch_outs])
    out_specs = ([pl.BlockSpec((bt, c, BRANCH_W), seq),
                  pl.BlockSpec(st_block, lambda i, j: (layer, i) + (0,) * (len(st) - 1))]
                 + [pl.BlockSpec((bt,) + s[1:], lambda i, j, _nd=len(s): (i,) + (0,) * (_nd - 1))
                    for s in batch_outs])
    operands = [u, *consts, *chunk_ins, st_in, *batch_ins[1:]]
    in_specs = ([u_spec] + [_const_spec(a) for a in consts]
                + [pl.BlockSpec((c, a.shape[1]), lambda i, j: (j, 0)) for a in chunk_ins]
                + [pl.BlockSpec(st_block, lambda i, j: (st_entry, i) + (0,) * (len(st) - 1))]
                + [_batch_spec(bt, a) for a in batch_ins[1:]])
    n_in = len(operands)
    aliases = {}
    fn = body
    if prev is not None:
        operands.append(prev)
        in_specs.append(pl.BlockSpec(memory_space=pl.ANY))
        aliases = {n_in: 1}
        fn = lambda *refs: body(*refs[:n_in], *refs[n_in + 1:])
    elif depth > 1:
        out_specs[1] = pl.BlockSpec((depth, bt) + st[1:], lambda i, j: (0, i) + (0,) * (len(st) - 1))

        def fn(*refs):
            so_ref = refs[n_in + 1]

            @pl.when(pl.program_id(1) == 0)
            def _():
                for e in range(depth):
                    if e != layer:
                        so_ref[e] = jnp.zeros(so_ref.shape[1:], f32)

            body(*refs[:n_in + 1], so_ref.at[layer], *refs[n_in + 2:])

    return pl.pallas_call(
        fn,
        grid=(b // bt, l // c),
        in_specs=in_specs,
        out_specs=out_specs,
        out_shape=out_shape,
        scratch_shapes=scratch,
        input_output_aliases=aliases,
        compiler_params=pltpu.CompilerParams(dimension_semantics=("arbitrary", "arbitrary"),
                                             vmem_limit_bytes=VMEM_LIMIT),
        name=name,
    )(*operands)


def _mamba_body(u_ref, wz_ref, wx_ref, wdt_ref, cw_ref, cb_ref, dtb_ref, alog_ref, d_ref, nrm_ref,
                s0_ref, c0_ref, y_ref, so_ref, co_ref, s_scr, xp_scr, z_scr, xs_scr, v_scr, cum_scr):
    bt, c, _ = y_ref.shape
    j = pl.program_id(1)
    nh = MB_HEADS
    npair = nh // 2
    hpg = nh // MB_GROUPS
    n = bt * nh * c
    r = STACK_ROWS if c >= STACK_ROWS else min(MXU_DIM, n)
    assert n % r == 0 and r % c == 0
    ng = n // r

    @pl.when(j == 0)
    def _():
        s_scr[...] = jnp.zeros_like(s_scr)
        for p in range(npair):
            g = (2 * p) // hpg
            s_scr[:, p, g * MB_STATE:(g + 1) * MB_STATE, :] = jnp.concatenate(
                [s0_ref[:, 2 * p], s0_ref[:, 2 * p + 1]], axis=-1)
        xp_scr[:, 0:CONV_PAD, :] = c0_ref[...]

    parts = _row_parts(bt, c, 2 * FRONT_ROWS)
    bp = bt // parts
    u = u_ref[...].reshape(bt * c, D_MODEL)
    ltri = _blockdiag_ltri(bp * c, c)
    lane3 = lax.broadcasted_iota(jnp.int32, (bp, c, LANES), 2)
    lane_bcast = lambda col: jnp.broadcast_to(col, (bp * c, LANES)).reshape(bp, c, LANES)
    dt_parts, bm_parts, cm_parts = {}, {}, {}

    def project(p):
        sq = slice(p * bp, (p + 1) * bp)
        up = u[p * bp * c:(p + 1) * bp * c]
        z_scr[sq] = jnp.dot(up, wz_ref[...], preferred_element_type=f32).reshape(bp, c, BRANCH_W)
        xp_scr[sq, CONV_PAD:CONV_PAD + c, :] = jnp.dot(up, wx_ref[...], preferred_element_type=f32).reshape(
            bp, c, MB_CONV_DIM)
        dt_parts[p] = jnp.dot(up, wdt_ref[...], preferred_element_type=f32)

    def front(p):
        sq = slice(p * bp, (p + 1) * bp)
        dt = jax.nn.softplus(dt_parts.pop(p) + dtb_ref[...])
        cum = _dot_sel_left(ltri, -jnp.exp(alog_ref[...]) * dt)
        xa = _silu(_causal_conv(xp_scr.at[sq], cw_ref, c) + cb_ref[...])
        tail = xp_scr[sq, c:c + CONV_PAD, :]
        xp_scr[sq, 0:CONV_PAD, :] = tail
        co_ref[sq] = tail
        xs = xa[:, :, :BRANCH_W]
        bm_parts[p] = xa[:, :, BRANCH_W:BRANCH_W + LANES]
        cm = xa[:, :, BRANCH_W + LANES:BRANCH_W + 2 * LANES]
        cm_parts[p] = [jnp.where((lane3 >> _log2(MB_STATE)) == g, cm, 0.0) for g in range(MB_GROUPS)]
        xs_scr[sq] = xs
        for h in range(nh):
            own_half = (lane3 >> _log2(MB_HEADDIM)) == (h % 2)
            v_scr[sq, h] = jnp.where(
                own_half, xs[:, :, (h // 2) * LANES:(h // 2 + 1) * LANES] * lane_bcast(dt[:, h:h + 1]), 0.0)
            cum_scr[sq, h] = lane_bcast(cum[:, h:h + 1])

    _pipeline_parts(parts, project, front)
    vf = v_scr[...].reshape(n, LANES)
    cbf = cum_scr[...].reshape(n, LANES)
    blocks_of = lambda first, count: [divmod(first + i, nh) for i in range(count)]
    c_rows = lambda blocks: jnp.concatenate([cm_parts[b // bp][h // hpg][b % bp] for b, h in blocks], axis=0)
    b_rows = lambda blocks: jnp.concatenate([bm_parts[b // bp][b % bp] for b, _ in blocks], axis=0)

    ii, jj = _iota2(r, r, 0), _iota2(r, r, 1)
    incl = ((ii >> _log2(c)) == (jj >> _log2(c))) & (ii >= jj)
    grp = lambda a, g: a[g * r:(g + 1) * r]
    gblocks = [blocks_of(g * (r // c), r // c) for g in range(ng)]
    dec = lambda g: jnp.exp(jnp.where(incl, _widen(grp(cbf, g), r) - grp(cbf, g).T[0:1, :], -jnp.inf))
    o_loc = [_dot(_dot_nt(c_rows(gblocks[g]), b_rows(gblocks[g])) * dec(g), grp(vf, g)) for g in range(ng)]
    o_loc = o_loc[0] if ng == 1 else jnp.concatenate(o_loc, axis=0)

    cb3 = cbf.reshape(n // c, c, LANES)
    k_out = (vf.reshape(n // c, c, LANES) * jnp.exp(cb3[:, c - 1:c, :] - cb3)).reshape(n, LANES)
    e_in = jnp.exp(cbf)
    lo_half = _iota2(1, LANES, 1) < MB_HEADDIM
    row_lo = _iota2(2 * c, LANES, 0) < c
    own = row_lo == (_iota2(2 * c, LANES, 1) < MB_HEADDIM)
    pairs = [(b, p) for b in range(bt) for p in range(npair)]
    prow = lambda a, i: a[i * 2 * c:(i + 1) * 2 * c]
    s_old = [s_scr[b, p] for b, p in pairs]
    pblocks = [blocks_of(i * 2, 2) for i in range(len(pairs))]
    o_st = [jnp.where(own, _dot(c_rows(pblocks[i]), s_old[i]) * prow(e_in, i), 0.0) for i in range(len(pairs))]
    for i, (b, p) in enumerate(pairs):
        cum_i = prow(cbf, i)
        e_all = jnp.exp(jnp.where(lo_half, cum_i[c - 1:c], cum_i[2 * c - 1:2 * c]))
        s_scr[b, p] = s_old[i] * e_all + _dot_tn(b_rows(pblocks[i]), prow(k_out, i))
    for b in range(bt):
        ys = []
        for p in range(npair):
            o2 = prow(o_loc, b * npair + p) + o_st[b * npair + p]
            sl = slice(p * LANES, (p + 1) * LANES)
            ys.append(o2[:c] + o2[c:] + d_ref[:, sl] * xs_scr[b, :, sl])
        y = jnp.concatenate(ys, axis=-1) * _silu(z_scr[b])
        y_ref[b] = (_rms(y) * nrm_ref[...]).astype(bf16)

    @pl.when(j == pl.num_programs(1) - 1)
    def _():
        for p in range(npair):
            g = (2 * p) // hpg
            so_ref[:, 2 * p] = s_scr[:, p, g * MB_STATE:(g + 1) * MB_STATE, 0:MB_HEADDIM]
            so_ref[:, 2 * p + 1] = s_scr[:, p, g * MB_STATE:(g + 1) * MB_STATE, MB_HEADDIM:2 * MB_HEADDIM]


def _mamba(u, w, s0, conv0, c, bt, slot):
    b, l, _ = u.shape
    npair = MB_HEADS // 2
    consts = [w["wz"], w["wx"], w["wdt"], w["cw"], w["cb"], w["dtb"], w["alog"], w["d"], w["nrm"]]
    scratch = [
        pltpu.VMEM((bt, npair, LANES, LANES), f32),
        pltpu.VMEM((bt, CONV_PAD + c, MB_CONV_DIM), f32),
        pltpu.VMEM((bt, c, BRANCH_W), f32),
        pltpu.VMEM((bt, c, BRANCH_W), f32),
    ] + [pltpu.VMEM((bt, MB_HEADS, c, LANES), f32)] * 2
    return _mixer_call(_mamba_body, "mamba", u, consts, [s0, conv0], c, bt,
                       [(b, CONV_PAD, MB_CONV_DIM)], scratch, slot)


def _ret_body(u_ref, wq_ref, wk_ref, wv_ref, wg_ref, lg_ref, cos_ref, sin_ref,
              s0_ref, y_ref, so_ref, s_scr, q_scr, k_scr, v_scr, gt_scr):
    bt, c, _ = y_ref.shape
    j = pl.program_id(1)
    kw = RET_HEADS * RET_DK
    hw = RET_DK // 2

    @pl.when(j == 0)
    def _():
        s_scr[...] = jnp.zeros_like(s_scr)
        for h in range(RET_HEADS):
            s_scr[:, h, h * hw:(h + 1) * hw, :] = s0_ref[:, h, 0:hw, :]
            s_scr[:, h, LANES + h * hw:LANES + (h + 1) * hw, :] = s0_ref[:, h, hw:2 * hw, :]

    u = u_ref[...].reshape(bt * c, D_MODEL)
    cos = cos_ref[...]
    sin = sin_ref[...]

    def rot(x):
        x = x.reshape(-1, c, kw)
        x1, x2 = x[:, :, :LANES], x[:, :, LANES:]
        return jnp.concatenate([x1 * cos - x2 * sin, x1 * sin + x2 * cos], axis=-1)

    nh = RET_HEADS
    n = bt * nh * c
    r = min(MXU_DIM, n)
    assert n % r == 0 and r % c == 0
    ng = n // r
    parts = _row_parts(bt, c, 2 * FRONT_ROWS)
    bp = bt // parts
    klane = lax.broadcasted_iota(jnp.int32, (bp, c, kw), 2)
    raw = {}

    def project(p):
        sq = slice(p * bp, (p + 1) * bp)
        up = u[p * bp * c:(p + 1) * bp * c]
        raw[p] = (jnp.dot(up, wq_ref[...], preferred_element_type=f32),
                  jnp.dot(up, wk_ref[...], preferred_element_type=f32),
                  jnp.dot(up, wv_ref[...], preferred_element_type=f32))
        gt_scr[sq] = jnp.dot(up, wg_ref[...], preferred_element_type=f32).reshape(bp, c, BRANCH_W)

    def front(p):
        sq = slice(p * bp, (p + 1) * bp)
        q_raw, k_raw, v_raw = raw.pop(p)
        q = rot(q_raw)
        k = rot(k_raw) * (RET_DK ** -0.5)
        v = v_raw.reshape(bp, c, BRANCH_W)
        for h in range(nh):
            own = ((klane & (LANES - 1)) >> _log2(hw)) == h
            q_scr[sq, h] = jnp.where(own, q, 0.0)
            k_scr[sq, h] = jnp.where(own, k, 0.0)
            v_scr[sq, h] = v[:, :, h * RET_DV:(h + 1) * RET_DV]

    _pipeline_parts(parts, project, front)
    qf = q_scr[...].reshape(n, kw)
    kf = k_scr[...].reshape(n, kw)
    vf = v_scr[...].reshape(n, RET_DV)

    row = _iota2(n, 1, 0)
    head = (row >> _log2(c)) & (nh - 1)
    pos = (row & (c - 1)).astype(f32)
    lg_row = jnp.zeros((n, 1), f32)
    for h in range(nh):
        lg_row = jnp.where(head == h, lg_ref[:, h:h + 1], lg_row)
    q_in = qf * jnp.exp((pos + 1.0) * lg_row)
    k_out = kf * jnp.exp((float(c) - 1.0 - pos) * lg_row)
    e_all = jnp.exp(float(c) * lg_row)

    ii, jj = _iota2(r, r, 0), _iota2(r, r, 1)
    incl = ((ii >> _log2(c)) == (jj >> _log2(c))) & (ii >= jj)
    dist = (ii - jj).astype(f32)
    grp = lambda a, g: a[g * r:(g + 1) * r]
    dec = {}
    for g in range(ng):
        key = (g * r) % (nh * c)
        if key not in dec:
            dec[key] = jnp.exp(jnp.where(incl, dist * grp(lg_row, g), -jnp.inf))
    o_loc = [_dot(_dot_nt(grp(qf, g), grp(kf, g)) * dec[(g * r) % (nh * c)], grp(vf, g)) for g in range(ng)]
    o_loc = o_loc[0] if ng == 1 else jnp.concatenate(o_loc, axis=0)

    blocks = [(b, h) for b in range(bt) for h in range(nh)]
    rows = lambda a, i: a[i * c:(i + 1) * c]
    s_old = [s_scr[b, h] for b, h in blocks]
    o_st = [_dot(rows(q_in, i), s_old[i]) for i in range(len(blocks))]
    for i, (b, h) in enumerate(blocks):
        s_scr[b, h] = s_old[i] * rows(e_all, i)[0:1] + _dot_tn(rows(k_out, i), rows(vf, i))
    for b in range(bt):
        y = jnp.concatenate([_rms(rows(o_loc, b * nh + h) + o_st[b * nh + h]) for h in range(nh)], axis=-1)
        y_ref[b] = (y * _silu(gt_scr[b])).astype(bf16)

    @pl.when(j == pl.num_programs(1) - 1)
    def _():
        for h in range(RET_HEADS):
            so_ref[:, h, 0:hw, :] = s_scr[:, h, h * hw:(h + 1) * hw, :]
            so_ref[:, h, hw:2 * hw, :] = s_scr[:, h, LANES + h * hw:LANES + (h + 1) * hw, :]


def _ret(u, w, cos, sin, lg, s0, c, bt, slot):
    kw = RET_HEADS * RET_DK
    consts = [w["wq"], w["wk"], w["wv"], w["wg"], lg]
    scratch = [
        pltpu.VMEM((bt, RET_HEADS, kw, RET_DV), f32),
        pltpu.VMEM((bt, RET_HEADS, c, kw), f32),
        pltpu.VMEM((bt, RET_HEADS, c, kw), f32),
        pltpu.VMEM((bt, RET_HEADS, c, RET_DV), f32),
        pltpu.VMEM((bt, c, BRANCH_W), f32),
    ]
    return _mixer_call(_ret_body, "ret", u, consts, [s0], c, bt, [], scratch, slot, chunk_ins=(cos, sin))


def _hgrn_body(u_ref, wq_ref, wf_ref, wi_ref, wg_ref, lb_ref, nrm_ref,
               s0_ref, y_ref, so_ref, st_scr, q_scr, k_scr, v_scr, cum_scr, gt_scr):
    bt, c, _ = y_ref.shape
    j = pl.program_id(1)
    nh = HG_HEADS
    n, r, ng = _stack_geometry(bt, c, nh)
    sub = HG_SUB

    @pl.when(j == 0)
    def _():
        st_scr[...] = s0_ref[...]

    parts = _row_parts(bt, c)
    bp = bt // parts
    u = u_ref[...].reshape(bt * c, D_MODEL)
    ltri = _blockdiag_ltri(bp * c, c)
    lb = lb_ref[...]
    raw = {}

    def project(p):
        sq = slice(p * bp, (p + 1) * bp)
        up = u[p * bp * c:(p + 1) * bp * c]
        raw[p] = (jnp.dot(up, wq_ref[...], preferred_element_type=f32),
                  jnp.dot(up, wf_ref[...], preferred_element_type=f32),
                  jnp.dot(up, wi_ref[...], preferred_element_type=f32))
        gt_scr[sq] = jnp.dot(up, wg_ref[...], preferred_element_type=f32).reshape(bp, c, BRANCH_W)

    def front(p):
        sq = slice(p * bp, (p + 1) * bp)
        q_raw, f_raw, v = raw.pop(p)
        q = _silu(q_raw)
        fg = lb + (1.0 - lb) * jax.nn.sigmoid(f_raw)
        cum = _dot_sel_left(ltri, jnp.log(fg))
        k = 1.0 - fg
        for h in range(nh):
            sl = slice(h * HG_DK, (h + 1) * HG_DK)
            q_scr[sq, h] = q[:, sl].reshape(bp, c, HG_DK)
            k_scr[sq, h] = k[:, sl].reshape(bp, c, HG_DK)
            v_scr[sq, h] = v[:, sl].reshape(bp, c, HG_DV)
            cum_scr[sq, h] = cum[:, sl].reshape(bp, c, HG_DK)

    ii, jj = _iota2(r, r, 0), _iota2(r, r, 1)
    n_p = n // parts
    g_p = n_p // r
    assert n_p % r == 0
    ones = jnp.ones((HG_DK, r), bf16)
    same_sub = (ii >> _log2(sub)) == (jj >> _log2(sub))
    part_scores = {}

    def local(p):
        sq = slice(p * bp, (p + 1) * bp)
        qf, kf, cf = (ref[sq].reshape(n_p, HG_DK) for ref in (q_scr, k_scr, cum_scr))
        scores = [jnp.zeros((r, r), f32) for _ in range(g_p)]
        q3, k3, c3 = (a.reshape(n_p // sub, sub, HG_DK) for a in (qf, kf, cf))
        sub_i = lax.broadcasted_iota(jnp.int32, q3.shape, 1)
        for t in range(sub):
            pt = jnp.where(sub_i >= t, q3 * k3[:, t:t + 1, :] * jnp.exp(c3 - c3[:, t:t + 1, :]), 0.0)
            row_sum = jnp.dot(pt.reshape(n_p, HG_DK).astype(bf16), ones, preferred_element_type=f32)
            place = same_sub & ((jj & (sub - 1)) == t)
            scores = [jnp.where(place, row_sum[g * r:(g + 1) * r], scores[g]) for g in range(g_p)]
        m = sub
        while m < c:
            q4, k4, c4 = (a.reshape(n_p // (2 * m), 2, m, HG_DK) for a in (qf, kf, cf))
            bnd = c4[:, 0:1, m - 1:m, :]
            odd = lax.broadcasted_iota(jnp.int32, q4.shape, 1) == 1
            e = jnp.exp(jnp.where(odd, c4 - bnd, bnd - c4))
            q_t = jnp.where(odd, q4 * e, 0.0).reshape(n_p, HG_DK)
            k_t = jnp.where(odd, 0.0, k4 * e).reshape(n_p, HG_DK)
            same_pair = (ii >> _log2(2 * m)) == (jj >> _log2(2 * m))
            scores = [scores[g] + jnp.where(same_pair, _dot_nt(q_t[g * r:(g + 1) * r], k_t[g * r:(g + 1) * r]), 0.0)
                      for g in range(g_p)]
            m *= 2
        part_scores[p] = scores

    _pipeline_parts(parts, project, front, local)
    scores = [s for p in range(parts) for s in part_scores[p]]
    qf = q_scr[...].reshape(n, HG_DK)
    kf = k_scr[...].reshape(n, HG_DK)
    vf = v_scr[...].reshape(n, HG_DV)
    cf = cum_scr[...].reshape(n, HG_DK)

    cb3 = cf.reshape(n // c, c, HG_DK)
    clast3 = cb3[:, c - 1:c, :]
    q_in = qf * jnp.exp(cf)
    k_out = (kf.reshape(n // c, c, HG_DK) * jnp.exp(clast3 - cb3)).reshape(n, HG_DK)
    nblk = n // c
    e_rows = jnp.exp(cb3[:, c - 1, :])
    if nblk < HG_DK:
        e_rows = jnp.concatenate([e_rows, jnp.zeros((HG_DK - nblk, HG_DK), f32)], axis=0)
    e_cols = e_rows.T
    blocks = [(b, h) for b in range(bt) for h in range(nh)]
    rows = lambda a, i: a[i * c:(i + 1) * c]
    st_old = [st_scr[b, h] for b, h in blocks]
    o_state = [_dot(rows(q_in, i), st_old[i]) for i in range(len(blocks))]
    for i, (b, h) in enumerate(blocks):
        st_scr[b, h] = st_old[i] * e_cols[:, i:i + 1] + _dot_tn(rows(k_out, i), rows(vf, i))
    o = [_dot(scores[g], vf[g * r:(g + 1) * r]) for g in range(ng)]
    o = o[0] if ng == 1 else jnp.concatenate(o, axis=0)
    for b in range(bt):
        y = jnp.concatenate([_rms(rows(o, b * nh + h) + o_state[b * nh + h]) * nrm_ref[...] for h in range(nh)],
                            axis=-1)
        y_ref[b] = (y * _silu(gt_scr[b])).astype(bf16)

    @pl.when(j == pl.num_programs(1) - 1)
    def _():
        so_ref[...] = st_scr[...]


def _hgrn(u, w, s0, c, bt, slot):
    consts = [w["wq"], w["wf"], w["wi"], w["wg"], w["lb"], w["nrm"]]
    scratch = ([pltpu.VMEM((bt, HG_HEADS, HG_DK, HG_DV), f32)] + [pltpu.VMEM((bt, HG_HEADS, c, HG_DK), f32)] * 4
               + [pltpu.VMEM((bt, c, BRANCH_W), f32)])
    return _mixer_call(_hgrn_body, "hgrn", u, consts, [s0], c, bt, [], scratch, slot)


def _unit_lower_inverses(a_list, r, c):
    ii, jj = _iota2(r, r, 0), _iota2(r, r, 1)
    eye = jnp.where(ii == jj, 1.0, 0.0)
    t_list = [eye for _ in a_list]
    m = 1
    while m < c:
        lm = _log2(m)
        join = ((ii >> (lm + 1)) == (jj >> (lm + 1))) & (((ii >> lm) & 1) == 1) & (((jj >> lm) & 1) == 0)
        if m == 1:
            t_list = [eye - jnp.where(join, a, 0.0) for a in a_list]
        else:
            ft_list = [_dot(jnp.where(join, a, 0.0), t) for a, t in zip(a_list, t_list)]
            t_list = [t - _dot(t, ft) for t, ft in zip(t_list, ft_list)]
        m *= 2
    return t_list


def _gdn_body(u_ref, wqkv_ref, wz_ref, wab_ref, cw_ref, dtb_ref, alog_ref, nrm_ref,
              s0_ref, c0_ref, y_ref, so_ref, co_ref, s_scr, xp_scr, z_scr, q_scr, k_scr, v_scr, cum_scr, beta_scr):
    bt, c, _ = y_ref.shape
    j = pl.program_id(1)
    nh = GDN_HEADS
    hk = nh * GDN_DK
    n, r, ng = _stack_geometry(bt, c, nh)

    @pl.when(j == 0)
    def _():
        s_scr[...] = s0_ref[...]
        xp_scr[:, 0:CONV_PAD, :] = c0_ref[...]

    parts = _row_parts(bt, c)
    bp = bt // parts
    ltri = _blockdiag_ltri(bp * c, c)
    u = u_ref[...].reshape(bt * c, D_MODEL)
    lane_bcast = lambda col: jnp.broadcast_to(col, (bp * c, LANES)).reshape(bp, c, LANES)
    ii, jj = _iota2(r, r, 0), _iota2(r, r, 1)
    same = (ii >> _log2(c)) == (jj >> _log2(c))
    incl = same & (ii >= jj)
    strict = same & (ii > jj)
    rows = lambda a, i: a[i * c:(i + 1) * c]
    nb = r // c
    ab_parts = {}

    def project(p):
        sq = slice(p * bp, (p + 1) * bp)
        up = u[p * bp * c:(p + 1) * bp * c]
        xp_scr[sq, CONV_PAD:CONV_PAD + c, :] = jnp.dot(up, wqkv_ref[...], preferred_element_type=f32).reshape(
            bp, c, GDN_CONV_DIM)
        z_scr[sq] = jnp.dot(up, wz_ref[...], preferred_element_type=f32).reshape(bp, c, BRANCH_W)
        ab_parts[p] = jnp.dot(up, wab_ref[...], preferred_element_type=f32)

    def front(p):
        sq = slice(p * bp, (p + 1) * bp)
        ab = ab_parts.pop(p)
        log_g = -jnp.exp(alog_ref[...]) * jax.nn.softplus(ab + dtb_ref[...])
        cum = _dot_sel_left(ltri, log_g)
        beta = jax.nn.sigmoid(ab)
        qkv = _silu(_causal_conv(xp_scr.at[sq], cw_ref, c))
        tail = xp_scr[sq, c:c + CONV_PAD, :]
        xp_scr[sq, 0:CONV_PAD, :] = tail
        co_ref[sq] = tail
        for h in range(nh):
            q = qkv[:, :, h * GDN_DK:(h + 1) * GDN_DK]
            k = qkv[:, :, hk + h * GDN_DK:hk + (h + 1) * GDN_DK]
            q_scr[sq, h] = q * lax.rsqrt(jnp.sum(q * q, axis=-1, keepdims=True) + EPS) * (GDN_DK ** -0.5)
            k_scr[sq, h] = k * lax.rsqrt(jnp.sum(k * k, axis=-1, keepdims=True) + EPS)
            v_scr[sq, h] = qkv[:, :, 2 * hk + h * GDN_DV:2 * hk + (h + 1) * GDN_DV]
            cum_scr[sq, h] = lane_bcast(cum[:, h:h + 1])
            beta_scr[sq, h] = lane_bcast(beta[:, nh + h:nh + h + 1])

    gpp = ng
    loc = range(gpp)
    held = {}

    def build(p):
        groups = range(p * gpp, (p + 1) * gpp)
        grp = lambda ref: [_group_rows(ref, g, r) for g in groups]
        qs, ks, vs, cb, bb = grp(q_scr), grp(k_scr), grp(v_scr), grp(cum_scr), grp(beta_scr)
        dec, a_mat = [], []
        for g in loc:
            dec.append(jnp.exp(jnp.where(incl, _widen(cb[g], r) - cb[g].T[0:1, :], -jnp.inf)))
            a_mat.append(jnp.where(strict, _widen(bb[g], r) * dec[g] * _dot_nt(ks[g], ks[g]), 0.0))
        held[p] = (qs, ks, vs, cb, bb, dec, a_mat)

    def invert(p):
        held[p] = held[p][:6] + (_unit_lower_inverses(held[p][6], r, c),)

    def finish(p):
        qs, ks, vs, cb, bb, dec, t_inv = held.pop(p)
        e_in, sol, scd, q_in = [], [], [], []
        for g in loc:
            e_in.append(jnp.exp(cb[g]))
            sol.append(_dot(t_inv[g], jnp.concatenate([vs[g] * bb[g], ks[g] * (e_in[g] * bb[g])], axis=-1)))
            scd.append(_dot_nt(qs[g], ks[g]) * dec[g])
            q_in.append(qs[g] * e_in[g])
        gb = [(g, i) for g in loc for i in range(nb)]
        blk = lambda g, i: divmod((p * gpp + g) * nb + i, nh)
        s_old = {k: s_scr[blk(*k)] for k in gb}
        proj = {(g, i): _dot(jnp.concatenate([rows(sol[g], i)[:, GDN_DV:], rows(q_in[g], i)], axis=0), s_old[g, i])
                for g, i in gb}
        w_new = {(g, i): rows(sol[g], i)[:, :GDN_DV] - proj[g, i][:c] for g, i in gb}
        for g, i in gb:
            cum_i = rows(cb[g], i)
            clast = cum_i[c - 1:c]
            s_scr[blk(g, i)] = (s_old[g, i] * jnp.exp(clast)
                                + _dot_tn(rows(ks[g], i) * jnp.exp(clast - cum_i), w_new[g, i]))
        o_all = [_dot(scd[g], jnp.concatenate([w_new[g, i] for i in range(nb)], axis=0))
                 + jnp.concatenate([proj[g, i][c:] for i in range(nb)], axis=0) for g in loc]
        o = o_all[0] if gpp == 1 else jnp.concatenate(o_all, axis=0)
        for b in range(bt):
            y = jnp.concatenate([_rms(rows(o, b * nh + h)) * nrm_ref[...] for h in range(nh)], axis=-1)
            y_ref[b] = (y * _silu(z_scr[b])).astype(bf16)

    _pipeline_parts(parts, project, front)
    _pipeline_parts(1, build, invert, finish)

    @pl.when(j == pl.num_programs(1) - 1)
    def _():
        so_ref[...] = s_scr[...]


def _gdn(u, w, s0, conv0, c, bt, slot):
    b, l, _ = u.shape
    consts = [w["wqkv"], w["wz"], w["wab"], w["cw"], w["dtb"], w["alog"], w["nrm"]]
    scratch = [
        pltpu.VMEM((bt, GDN_HEADS, GDN_DK, GDN_DV), f32),
        pltpu.VMEM((bt, CONV_PAD + c, GDN_CONV_DIM), f32),
        pltpu.VMEM((bt, c, BRANCH_W), f32),
    ] + [pltpu.VMEM((bt, GDN_HEADS, c, LANES), f32)] * 5
    return _mixer_call(_gdn_body, "gdn", u, consts, [s0, conv0], c, bt,
                       [(b, CONV_PAD, GDN_CONV_DIM)], scratch, slot)


def _pad_lanes(v, offset=0):
    row = jnp.zeros((1, LANES), f32)
    return row.at[0, offset:offset + v.shape[0]].set(v.astype(f32))


def _layer_weights(l, norm_g, ffn_w_in, ffn_w_out, w_in, w_branch, w_out, mb_conv_w, mb_conv_b, mb_a_log,
                   mb_dt_bias, mb_d, mb_norm, lb_all, hg_norm, gdn_conv_w, gdn_a_log, gdn_dt_bias, gdn_norm):
    wi = w_in[l]
    col = lambda n: wi[:, IN_OFFSETS[n]:IN_OFFSETS[n + 1]]
    colb = lambda n: col(n).astype(bf16)
    pad_cols = lambda a: jnp.pad(a, ((0, 0), (0, LANES - a.shape[1]))).astype(bf16)
    hw = RET_DK // 2
    perm = np.arange(RET_HEADS * RET_DK).reshape(RET_HEADS, 2, hw).transpose(1, 0, 2).reshape(-1)
    row = lambda v: v.astype(f32).reshape(1, -1)
    return {
        "g_ffn1": norm_g[l, 0:3], "g_post": norm_g[l, 3:4], "g_ffn2": norm_g[l, 4:6],
        "layer": l, "ffn_in": ffn_w_in, "ffn_out": ffn_w_out,
        "mamba": {"wz": colb(0), "wx": colb(1), "wdt": pad_cols(col(2)), "cw": mb_conv_w[l],
                  "cb": row(mb_conv_b[l]),
                  "dtb": _pad_lanes(mb_dt_bias[l]), "alog": _pad_lanes(mb_a_log[l]),
                  "d": row(jnp.repeat(mb_d[l], MB_HEADDIM)), "nrm": row(mb_norm[l])},
        "hgrn": {"wq": colb(3), "wf": colb(4), "wi": colb(5), "wg": colb(6), "lb": row(lb_all[l]),
                 "nrm": row(hg_norm[l])},
        "ret": {"wq": col(7)[:, perm].astype(bf16), "wk": col(8)[:, perm].astype(bf16),
                "wv": colb(9), "wg": colb(10)},
        "gdn": {"wqkv": colb(11), "wz": colb(12),
                "wab": pad_cols(jnp.concatenate([col(13), col(14)], axis=1)),
                "cw": gdn_conv_w[l], "dtb": _pad_lanes(gdn_dt_bias[l]), "alog": _pad_lanes(gdn_a_log[l]),
                "nrm": row(gdn_norm[l])},
        "w_gate": colb(15), "w_branch": w_branch[l].astype(bf16), "w_out": w_out[l].astype(bf16),
    }


def _rope_tables(pos):
    half = RET_DK // 2
    inv_freq = 1.0 / (ROPE_BASE ** jnp.linspace(0.0, 1.0, half, dtype=f32))
    ang = pos.astype(f32)[:, None] * inv_freq[None, :]
    reps = LANES // half
    return jnp.tile(jnp.cos(ang), (1, reps)), jnp.tile(jnp.sin(ang), (1, reps))


def _pad_conv(buf):
    return jnp.pad(buf, ((0, 0), (CONV_PAD - (CONV_W - 1), 0), (0, 0)))


def _layer(h, w, states, rope, lg, chunks, bt, depth, stacked):
    b, l, d = h.shape
    s_ssm, s_ssm_conv, s_hg, s_ret, s_gdn, s_gdn_conv = states
    slot = lambda n: (w["layer"], depth, None if stacked is None else stacked[n])
    h, u2 = _ffn(h.reshape(b * l, d), w["g_ffn1"], w["ffn_in"], w["ffn_out"], w["layer"], 0, emit_normed=True)
    u = u2.reshape(b, l, d)
    y_mb, n_ssm, n_ssm_conv = _mamba(u, w["mamba"], s_ssm, _pad_conv(s_ssm_conv), chunks["mamba"], bt, slot(0))
    y_hg, n_hg = _hgrn(u, w["hgrn"], s_hg, chunks["hgrn"], bt, slot(1))
    y_rt, n_ret = _ret(u, w["ret"], rope[0], rope[1], lg, s_ret, chunks["ret"], bt, slot(2))
    y_gd, n_gdn, n_gdn_conv = _gdn(u, w["gdn"], s_gdn, _pad_conv(s_gdn_conv), chunks["gdn"], bt, slot(3))
    ys = [y.reshape(b * l, BRANCH_W) for y in (y_mb, y_hg, y_rt, y_gd)]
    hf = _merge(h, u2, w["g_post"], ys, w["w_gate"], w["w_branch"], w["w_out"])
    hf = _ffn(hf, w["g_ffn2"], w["ffn_in"], w["ffn_out"], w["layer"], 1)
    convs = (n_ssm_conv[:, CONV_PAD - (CONV_W - 1):], n_gdn_conv[:, CONV_PAD - (CONV_W - 1):])
    return hf.reshape(b, l, d), (n_ssm, n_hg, n_ret, n_gdn), convs


def _chunks(l):
    pick = lambda pref: pref if l % pref == 0 else math.gcd(l, pref)
    return {"mamba": pick(128), "ret": pick(128), "hgrn": pick(64), "gdn": pick(64)}


def kernel(x_prompt, x_sample, state_ssm, state_ssm_conv, state_hgrn, state_ret, state_gdn, state_gdn_conv,
           norm_g, ffn_w_in, ffn_w_out, w_in, w_branch, w_out, mb_conv_w, mb_conv_b, mb_a_log, mb_dt_bias, mb_d,
           mb_norm, hg_lb_logits, hg_norm, gdn_conv_w, gdn_a_log, gdn_dt_bias, gdn_norm):
    depth = norm_g.shape[0]
    lb_all = jnp.cumsum(jax.nn.softmax(hg_lb_logits.astype(f32), axis=0), axis=0)
    lb_all = lb_all - lb_all[0:1]
    lg = _pad_lanes(jnp.log(1.0 - jnp.exp2(-5.0 - jnp.arange(RET_HEADS, dtype=f32))))

    bp, lp, _ = x_prompt.shape
    bs, ls, _ = x_sample.shape
    rope_p = _rope_tables(jnp.arange(lp))
    rope_s = _rope_tables(PAST_LEN + jnp.arange(ls))
    zero_states = (
        (jnp.zeros((1, bp, MB_HEADS, MB_STATE, MB_HEADDIM), f32), 0),
        jnp.zeros((bp, CONV_W - 1, MB_CONV_DIM), f32),
        (jnp.zeros((1, bp, HG_HEADS, HG_DK, HG_DV), f32), 0),
        (jnp.zeros((1, bp, RET_HEADS, RET_DK, RET_DV), f32), 0),
        (jnp.zeros((1, bp, GDN_HEADS, GDN_DK, GDN_DV), f32), 0),
        jnp.zeros((bp, CONV_W - 1, GDN_CONV_DIM), f32),
    )
    bt_p = math.gcd(bp, 8)
    bt_s = math.gcd(bs, 16)

    ffn_w_in = ffn_w_in.astype(bf16)
    ffn_w_out = ffn_w_out.astype(bf16)
    hp, hs = x_prompt, x_sample
    st_p = st_s = None
    cv_p, cv_s = [], []
    for l in range(depth):
        w = _layer_weights(l, norm_g, ffn_w_in, ffn_w_out, w_in, w_branch, w_out, mb_conv_w, mb_conv_b,
                           mb_a_log, mb_dt_bias, mb_d, mb_norm, lb_all, hg_norm, gdn_conv_w, gdn_a_log,
                           gdn_dt_bias, gdn_norm)
        hp, st_p, cp = _layer(hp, w, zero_states, rope_p, lg, _chunks(lp), bt_p, depth, st_p)
        past = ((state_ssm, l), state_ssm_conv[l], (state_hgrn, l), (state_ret, l), (state_gdn, l), state_gdn_conv[l])
        hs, st_s, cs = _layer(hs, w, past, rope_s, lg, _chunks(ls), bt_s, depth, st_s)
        cv_p.append(cp)
        cv_s.append(cs)

    stack = lambda cvs, i: jnp.stack([cv[i] for cv in cvs], axis=0)
    return (hp, hs,
            st_p[0], stack(cv_p, 0), st_p[1], st_p[2], st_p[3], stack(cv_p, 1),
            st_s[0], stack(cv_s, 0), st_s[1], st_s[2], st_s[3], stack(cv_s, 1))
```
